```python
import jax, jax.numpy as jnp
from jax import lax
import numpy as np

D_MODEL = 2048
BATCH = 2
SEQ = 4096
DEPTH = 1

CONV_CH = D_MODEL // 2
CONV_WIDTH = 31
N_HEADS = 8
HEAD_DIM = 128
ATTN_WIDTH = N_HEADS * HEAD_DIM
ROPE_DIM = HEAD_DIM // 4
ROPE_THETA = 500000.0
MOBA_BLOCK = 256
MOBA_TOPK = 3
Q_CHUNK = 32
N_GROUPS = 4
EXPERTS_PER_GROUP = 8
TOPK_IN_GROUP = 2
EXPERT_FF = D_MODEL // 4
LN_EPS = 1e-5
NEG_INF = -1e30
ALPHA = (2.0 * DEPTH) ** 0.25
BETA = (8.0 * DEPTH) ** -0.25

OFF_GLU_B = CONV_CH
OFF_Q = OFF_GLU_B + CONV_CH
OFF_K = OFF_Q + ATTN_WIDTH
OFF_V = OFF_K + ATTN_WIDTH
OFF_G_CONV = OFF_V + ATTN_WIDTH
OFF_G_ATTN = OFF_G_CONV + D_MODEL
IN_COLS = OFF_G_ATTN + D_MODEL

kernel_name = "hybrid_conformer_moba_hmoe_block"


def layer_norm(x):
    xf = x.astype(jnp.float32)
    mu = jnp.mean(xf, axis=-1, keepdims=True)
    var = jnp.mean(jnp.square(xf - mu), axis=-1, keepdims=True)
    return ((xf - mu) * lax.rsqrt(var + LN_EPS)).astype(x.dtype)


def partial_rope(x, positions):
    half = ROPE_DIM // 2
    inv_freq = jnp.power(ROPE_THETA, -jnp.arange(half, dtype=jnp.float32) / half)
    ang = positions.astype(jnp.float32)[..., None] * inv_freq
    cos = jnp.cos(ang)[:, :, None, :]
    sin = jnp.sin(ang)[:, :, None, :]
    xr = x[..., :ROPE_DIM].astype(jnp.float32)
    x1, x2 = xr[..., :half], xr[..., half:]
    rot = jnp.concatenate([x1 * cos - x2 * sin, x2 * cos + x1 * sin], axis=-1).astype(x.dtype)
    return jnp.concatenate([rot, x[..., ROPE_DIM:]], axis=-1)


def conformer_conv(a, b, w_dw, b_dw, g_cn, b_cn, w_out, b_out):
    h = a * jax.nn.sigmoid(b)
    h = lax.conv_general_dilated(
        h, w_dw[:, None, :], window_strides=(1,), padding=[(CONV_WIDTH - 1, 0)],
        dimension_numbers=("NWC", "WIO", "NWC"), feature_group_count=CONV_CH) + b_dw
    h = jax.nn.silu(layer_norm(h) * g_cn + b_cn)
    return h @ w_out + b_out


def moba_attention(q, k, v):
    B, T, H, hd = q.shape
    nb = -(-T // MOBA_BLOCK)
    tp = nb * MOBA_BLOCK
    pad = ((0, 0), (0, tp - T), (0, 0), (0, 0))
    qh = jnp.pad(q, pad).transpose(0, 2, 1, 3)
    kh = jnp.pad(k, pad).transpose(0, 2, 1, 3)
    vh = jnp.pad(v, pad).transpose(0, 2, 1, 3)
    kblk = kh.reshape(B, H, nb, MOBA_BLOCK, hd)
    vblk = vh.reshape(B, H, nb, MOBA_BLOCK, hd)
    kmean = jnp.mean(kblk.astype(jnp.float32), axis=3)
    gate = jnp.einsum('bhtd,bhnd->bhtn', qh.astype(jnp.float32), kmean)
    q_block = jnp.arange(tp) // MOBA_BLOCK
    fully_past = jnp.arange(nb)[None, :] < q_block[:, None]
    gate = jnp.where(fully_past, gate, -jnp.inf)
    n_sel = min(MOBA_TOPK, nb)
    _, sel = lax.top_k(gate, n_sel)
    scale = hd ** -0.5
    b_idx = jnp.arange(B)[:, None, None, None]
    h_idx = jnp.arange(H)[None, :, None, None]

    def step(ci):
        s = ci * Q_CHUNK
        blk = s // MOBA_BLOCK
        qc = lax.dynamic_slice_in_dim(qh, s, Q_CHUNK, axis=2)
        selc = lax.dynamic_slice_in_dim(sel, s, Q_CHUNK, axis=2)
        valid = selc < blk
        k_sel = kblk[b_idx, h_idx, selc]
        v_sel = vblk[b_idx, h_idx, selc]
        k_own = lax.dynamic_slice_in_dim(kh, blk * MOBA_BLOCK, MOBA_BLOCK, axis=2)
        v_own = lax.dynamic_slice_in_dim(vh, blk * MOBA_BLOCK, MOBA_BLOCK, axis=2)
        s_sel = jnp.einsum('bhqd,bhqjsd->bhqjs', qc, k_sel).astype(jnp.float32) * scale
        s_sel = jnp.where(valid[..., None], s_sel, NEG_INF).reshape(B, H, Q_CHUNK, n_sel * MOBA_BLOCK)
        s_own = jnp.einsum('bhqd,bhsd->bhqs', qc, k_own).astype(jnp.float32) * scale
        qpos = s + jnp.arange(Q_CHUNK)
        kpos = blk * MOBA_BLOCK + jnp.arange(MOBA_BLOCK)
        s_own = jnp.where(kpos[None, :] <= qpos[:, None], s_own, NEG_INF)
        p = jax.nn.softmax(jnp.concatenate([s_sel, s_own], axis=-1), axis=-1).astype(v.dtype)
        p_sel = p[..., :n_sel * MOBA_BLOCK].reshape(B, H, Q_CHUNK, n_sel, MOBA_BLOCK)
        p_own = p[..., n_sel * MOBA_BLOCK:]
        return (jnp.einsum('bhqjs,bhqjsd->bhqd', p_sel, v_sel)
                + jnp.einsum('bhqs,bhsd->bhqd', p_own, v_own))

    out = lax.map(step, jnp.arange(tp // Q_CHUNK))
    out = out.transpose(1, 0, 3, 2, 4).reshape(B, tp, H, hd)
    return out[:, :T]


def hier_moe(u, w_grp, b_grp, w_er, b_er, w1, w3, w2):
    B, T, D = u.shape
    uf = u.reshape(B * T, D)
    grp_logits = (uf @ w_grp).astype(jnp.float32) + b_grp.astype(jnp.float32)
    grp_prob = jax.nn.softmax(grp_logits, axis=-1)
    p_top, g_top = lax.top_k(grp_prob, 1)
    exp_logits = ((uf @ w_er).astype(jnp.float32) + b_er.astype(jnp.float32)).reshape(
        -1, N_GROUPS, EXPERTS_PER_GROUP)
    in_grp = jnp.take_along_axis(exp_logits, g_top[:, :, None], axis=1)[:, 0]
    e_val, e_idx = lax.top_k(in_grp, TOPK_IN_GROUP)
    e_w = jax.nn.softmax(e_val, axis=-1) * p_top
    exp_gate = jnp.sum(jax.nn.one_hot(e_idx, EXPERTS_PER_GROUP, dtype=jnp.float32) * e_w[..., None], axis=1)
    grp_mask = jax.nn.one_hot(g_top[:, 0], N_GROUPS, dtype=jnp.float32)
    gate = (grp_mask[:, :, None] * exp_gate[:, None, :]).astype(u.dtype)
    y = jnp.zeros_like(uf)
    for g in range(N_GROUPS):
        h = jax.nn.silu(jnp.einsum('nd,edf->nef', uf, w1[g])) * jnp.einsum('nd,edf->nef', uf, w3[g])
        y = y + jnp.einsum('nef,efd->nd', h * gate[:, g, :, None], w2[g])
    return y.reshape(B, T, D)


def setup_inputs(seed: int = 0) -> dict:
    key = jax.random.key(seed)
    ks = jax.random.split(key, 26)
    L, D, G, E, F = DEPTH, D_MODEL, N_GROUPS, EXPERTS_PER_GROUP, EXPERT_FF

    def nrm(k, shape, s):
        return jax.random.normal(k, shape, jnp.float32) * s

    col_scale = jnp.ones((IN_COLS,), jnp.float32).at[OFF_V:OFF_V + ATTN_WIDTH].set(BETA)
    positions = (jnp.arange(SEQ, dtype=jnp.int32)[None, :]
                 + jax.random.randint(ks[2], (BATCH, 1), 0, 1024, dtype=jnp.int32))
    return {
        "x": nrm(ks[0], (BATCH, SEQ, D), 1.0),
        "c": nrm(ks[1], (BATCH, D), 1.0),
        "positions": positions,
        "w_cond": nrm(ks[3], (L, D, 6 * D), 0.5 * D ** -0.5),
        "b_cond": nrm(ks[4], (L, 6 * D), 0.01),
        "w_in": nrm(ks[5], (L, D, IN_COLS), D ** -0.5) * col_scale,
        "b_glu": nrm(ks[6], (L, 2 * CONV_CH), 0.01),
        "w_dw": nrm(ks[7], (L, CONV_WIDTH, CONV_CH), CONV_WIDTH ** -0.5),
        "b_dw": nrm(ks[8], (L, CONV_CH), 0.01),
        "g_cn": 1.0 + nrm(ks[9], (L, CONV_CH), 0.02),
        "b_cn": nrm(ks[10], (L, CONV_CH), 0.01),
        "w_conv_out": nrm(ks[11], (L, CONV_CH, D), BETA * CONV_CH ** -0.5),
        "b_conv_out": nrm(ks[12], (L, D), 0.01),
        "w_attn_out": nrm(ks[13], (L, ATTN_WIDTH, D), BETA * ATTN_WIDTH ** -0.5),
        "w_mix_out": nrm(ks[14], (L, D, D), BETA * D ** -0.5),
        "g_ln1": 1.0 + nrm(ks[15], (L, D), 0.02),
        "b_ln1": nrm(ks[16], (L, D), 0.01),
        "w_grp": nrm(ks[17], (L, D, G), D ** -0.5),
        "b_grp": nrm(ks[18], (L, G), 0.01),
        "w_erouter": nrm(ks[19], (L, D, G * E), D ** -0.5),
        "b_erouter": nrm(ks[20], (L, G * E), 0.01),
        "w1": nrm(ks[21], (L, G, E, D, F), D ** -0.5),
        "w3": nrm(ks[22], (L, G, E, D, F), D ** -0.5),
        "w2": nrm(ks[23], (L, G, E, F, D), BETA * F ** -0.5),
        "g_ln2": 1.0 + nrm(ks[24], (L, D), 0.02),
        "b_ln2": nrm(ks[25], (L, D), 0.01),
    }


def reference(x, c, positions, w_cond, b_cond, w_in, b_glu, w_dw, b_dw, g_cn, b_cn,
              w_conv_out, b_conv_out, w_attn_out, w_mix_out, g_ln1, b_ln1,
              w_grp, b_grp, w_erouter, b_erouter, w1, w3, w2, g_ln2, b_ln2):
    B, T, D = x.shape
    for l in range(DEPTH):
        mod = jax.nn.silu(c) @ w_cond[l] + b_cond[l]
        sh1, sc1, gt1, sh2, sc2, gt2 = jnp.split(mod[:, None, :], 6, axis=-1)

        u = layer_norm(x) * (1 + sc1) + sh1
        z = u @ w_in[l]
        glu_a, glu_b, q, k, v, g_conv, g_attn = jnp.split(
            z, [OFF_GLU_B, OFF_Q, OFF_K, OFF_V, OFF_G_CONV, OFF_G_ATTN], axis=-1)
        ba, bb = jnp.split(b_glu[l], 2)
        y_conv = conformer_conv(glu_a + ba, glu_b + bb, w_dw[l], b_dw[l], g_cn[l], b_cn[l],
                                w_conv_out[l], b_conv_out[l])
        q = partial_rope(q.reshape(B, T, N_HEADS, HEAD_DIM), positions)
        k = partial_rope(k.reshape(B, T, N_HEADS, HEAD_DIM), positions)
        v = v.reshape(B, T, N_HEADS, HEAD_DIM)
        y_attn = moba_attention(q, k, v).reshape(B, T, ATTN_WIDTH) @ w_attn_out[l]
        merged = jax.nn.sigmoid(g_conv) * y_conv + jax.nn.sigmoid(g_attn) * y_attn
        t_out = merged @ w_mix_out[l]
        x = layer_norm(ALPHA * x + (1 + gt1) * t_out) * g_ln1[l] + b_ln1[l]

        u2 = layer_norm(x) * (1 + sc2) + sh2
        f = hier_moe(u2, w_grp[l], b_grp[l], w_erouter[l], b_erouter[l], w1[l], w3[l], w2[l])
        x = layer_norm(ALPHA * x + (1 + gt2) * f) * g_ln2[l] + b_ln2[l]
    return x
```

```python
import functools

import jax
import jax.numpy as jnp
from jax import lax
from jax.experimental import pallas as pl
from jax.experimental.pallas import tpu as pltpu

CONV_WIDTH = 31
N_HEADS = 8
HEAD_DIM = 128
ROPE_DIM = HEAD_DIM // 4
ROPE_THETA = 500000.0
MOBA_BLOCK = 256
MOBA_TOPK = 3
N_GROUPS = 4
EXPERTS_PER_GROUP = 8
N_EXPERTS = N_GROUPS * EXPERTS_PER_GROUP
TOPK_IN_GROUP = 2
LN_EPS = 1e-5
NEG_INF = -1e30

LANES = 128
SUBLANES = 8
VMEM_LIMIT = 56 * 1024 * 1024

PROJ_TM = 512
PROJ_TN = 1024
CONV_TT = 256
CONV_HALO = 32
MIX_TM = 256
EXP_TM = 256
ROW_TM = 256
META_BLK = 512


def _cparams(sem, vmem=VMEM_LIMIT):
    return pltpu.CompilerParams(dimension_semantics=sem, vmem_limit_bytes=vmem)


def _layer_norm_rows(x):
    mu = jnp.mean(x, axis=-1, keepdims=True)
    xc = x - mu
    var = jnp.mean(xc * xc, axis=-1, keepdims=True)
    return xc * lax.rsqrt(var + LN_EPS)


def _sigmoid(x):
    return 1.0 / (1.0 + jnp.exp(-x))


def _dot_nt(a, b):
    return lax.dot_general(a, b, (((1,), (1,)), ((), ())), preferred_element_type=jnp.float32)


def _mod_kernel(cb_ref, w_ref, b_ref, o_ref, s_ref):
    @pl.when(pl.program_id(0) == 0)
    def _():
        cb = cb_ref[...]
        s_ref[...] = cb * _sigmoid(cb)

    nb, d, _ = cb_ref.shape
    tn = w_ref.shape[1]
    for cc in range(tn // LANES):
        sl = slice(cc * LANES, (cc + 1) * LANES)
        wc = w_ref[:, sl]
        for b in range(nb):
            p = (wc * s_ref[b]).reshape(d // SUBLANES, SUBLANES, LANES)
            r = jnp.sum(jnp.sum(p, axis=0), axis=0, keepdims=True)
            o_ref[b:b + 1, sl] = r + b_ref[:, sl]


def _mod(c, w_cond, b_cond, tn=512):
    nb, d = c.shape
    n_out = w_cond.shape[1]
    cb = jnp.broadcast_to(c[:, :, None], (nb, d, LANES))
    return pl.pallas_call(
        _mod_kernel,
        grid=(n_out // tn,),
        in_specs=[pl.BlockSpec((nb, d, LANES), lambda j: (0, 0, 0)),
                  pl.BlockSpec((d, tn), lambda j: (0, j)),
                  pl.BlockSpec((1, tn), lambda j: (0, j))],
        out_specs=pl.BlockSpec((nb, tn), lambda j: (0, j)),
        out_shape=jax.ShapeDtypeStruct((nb, n_out), jnp.float32),
        scratch_shapes=[pltpu.VMEM((nb, d, LANES), jnp.float32)],
        compiler_params=_cparams(("arbitrary",)),
        name="mod",
    )(cb, w_cond, b_cond.reshape(1, n_out))


def _rope_tab_kernel(pos_ref, invf_ref, sign_ref, cos_ref, sin_ref):
    ang = pos_ref[...] * invf_ref[...]
    cos_ref[...] = jnp.cos(ang)
    sin_ref[...] = jnp.sin(ang) * sign_ref[...]


def _rope_tables(positions, tm=1024):
    n = positions.size
    half = ROPE_DIM // 2
    inv_freq = jnp.power(ROPE_THETA, -jnp.arange(half, dtype=jnp.float32) / half)
    invf = jnp.concatenate([inv_freq, inv_freq, jnp.zeros((LANES - ROPE_DIM,), jnp.float32)]).reshape(1, LANES)
    sign = jnp.concatenate([-jnp.ones((half,), jnp.float32),
                            jnp.ones((LANES - half,), jnp.float32)]).reshape(1, LANES)
    posb = jnp.broadcast_to(positions.astype(jnp.float32).reshape(n, 1), (n, LANES))
    vec = pl.BlockSpec((1, LANES), lambda i: (0, 0))
    tab = pl.BlockSpec((tm, LANES), lambda i: (i, 0))
    return pl.pallas_call(
        _rope_tab_kernel,
        grid=(n // tm,),
        in_specs=[tab, vec, vec],
        out_specs=[tab, tab],
        out_shape=[jax.ShapeDtypeStruct((n, LANES), jnp.float32)] * 2,
        compiler_params=_cparams(("parallel",)),
        name="rope_tab",
    )(posb, invf, sign)


def _in_proj_kernel(x_ref, sc_ref, sh_ref, w_ref, bias_ref, cos_ref, sin_ref, o_ref, u_ref,
                    *, rope_tiles, sig_from):
    j = pl.program_id(1)

    @pl.when(j == 0)
    def _():
        xn = _layer_norm_rows(x_ref[...])
        u_ref[...] = (xn * (1.0 + sc_ref[0]) + sh_ref[0]).astype(u_ref.dtype)

    acc = jnp.dot(u_ref[...], w_ref[...], preferred_element_type=jnp.float32) + bias_ref[...]
    is_rope = (j >= rope_tiles[0]) & (j < rope_tiles[1])

    @pl.when(is_rope)
    def _():
        cosf = cos_ref[...]
        sinf = sin_ref[...]
        lane = lax.broadcasted_iota(jnp.int32, cosf.shape, 1)
        first = lane < (ROPE_DIM // 2)
        for h in range(acc.shape[1] // HEAD_DIM):
            sl = slice(h * HEAD_DIM, (h + 1) * HEAD_DIM)
            xh = acc[:, sl]
            partner = jnp.where(first, pltpu.roll(xh, HEAD_DIM - ROPE_DIM // 2, 1),
                                pltpu.roll(xh, ROPE_DIM // 2, 1))
            o_ref[:, sl] = (xh * cosf + partner * sinf).astype(o_ref.dtype)

    @pl.when(j >= sig_from)
    def _():
        o_ref[...] = _sigmoid(acc).astype(o_ref.dtype)

    @pl.when(jnp.logical_not(is_rope) & (j < sig_from))
    def _():
        o_ref[...] = acc.astype(o_ref.dtype)


def _in_proj(x2, sc1, sh1, w_bf, bias, cos_t, sin_t, seq, rope_tiles, sig_from):
    n, d = x2.shape
    cols = w_bf.shape[1]
    tm, tn = PROJ_TM, PROJ_TN
    per_batch = seq // tm
    kern = functools.partial(_in_proj_kernel, rope_tiles=rope_tiles, sig_from=sig_from)
    return pl.pallas_call(
        kern,
        grid=(n // tm, cols // tn),
        in_specs=[pl.BlockSpec((tm, d), lambda i, j: (i, 0)),
                  pl.BlockSpec((1, 1, d), lambda i, j: (i // per_batch, 0, 0)),
                  pl.BlockSpec((1, 1, d), lambda i, j: (i // per_batch, 0, 0)),
                  pl.BlockSpec((d, tn), lambda i, j: (0, j)),
                  pl.BlockSpec((1, tn), lambda i, j: (0, j)),
                  pl.BlockSpec((tm, LANES), lambda i, j: (i, 0)),
                  pl.BlockSpec((tm, LANES), lambda i, j: (i, 0))],
        out_specs=pl.BlockSpec((tm, tn), lambda i, j: (i, j)),
        out_shape=jax.ShapeDtypeStruct((n, cols), jnp.bfloat16),
        scratch_shapes=[pltpu.VMEM((tm, d), jnp.bfloat16)],
        compiler_params=_cparams(("parallel", "arbitrary")),
        name="in_proj",
    )(x2, sc1, sh1, w_bf, bias, cos_t, sin_t)


def _kmean_kernel(k_ref, o_ref):
    k = k_ref[...].astype(jnp.float32)
    o_ref[0] = jnp.sum(k, axis=0, keepdims=True) * (1.0 / k.shape[0])


def _kmean(z, n_blocks, k_tile, width):
    return pl.pallas_call(
        _kmean_kernel,
        grid=(n_blocks,),
        in_specs=[pl.BlockSpec((MOBA_BLOCK, width), lambda i: (i, k_tile))],
        out_specs=pl.BlockSpec((1, 1, width), lambda i: (i, 0, 0)),
        out_shape=jax.ShapeDtypeStruct((n_blocks, 1, width), jnp.float32),
        compiler_params=_cparams(("parallel",)),
        name="kmean",
    )(z)


def _attn_kernel(q_ref, k_ref, v_ref, km_ref, o_ref, sel_ref):
    i = pl.program_id(2)
    bs = MOBA_BLOCK
    scale = HEAD_DIM ** -0.5
    q = q_ref[...]

    lane = lax.broadcasted_iota(jnp.int32, (bs, LANES), 1)
    past = lane < i
    g = jnp.where(past, _dot_nt(q, km_ref[0]), -jnp.inf)
    sel = jnp.zeros((bs, LANES), jnp.float32)
    for _ in range(MOBA_TOPK):
        m = jnp.max(g, axis=-1, keepdims=True)
        idx = jnp.min(jnp.where(g == m, lane, LANES), axis=-1, keepdims=True)
        hit = lane == idx
        sel = jnp.where(hit & past, 1.0, sel)
        g = jnp.where(hit, -jnp.inf, g)
    sel_ref[...] = sel

    own = pl.multiple_of(i * bs, bs)
    s = _dot_nt(q, k_ref[pl.ds(own, bs), :]) * scale
    row = lax.broadcasted_iota(jnp.int32, (bs, bs), 0)
    col = lax.broadcasted_iota(jnp.int32, (bs, bs), 1)
    s = jnp.where(col <= row, s, NEG_INF)
    m0 = jnp.max(s, axis=-1, keepdims=True)
    p = jnp.exp(s - m0)
    l0 = jnp.sum(p, axis=-1, keepdims=True)
    acc0 = jnp.dot(p.astype(v_ref.dtype), v_ref[pl.ds(own, bs), :], preferred_element_type=jnp.float32)

    def past_block(j, carry):
        m, l, acc = carry
        off = pl.multiple_of(j * bs, bs)
        rowsel = jnp.sum(jnp.where(lane == j, sel_ref[...], 0.0), axis=-1, keepdims=True) > 0.5
        sj = _dot_nt(q, k_ref[pl.ds(off, bs), :]) * scale
        sj = jnp.where(rowsel, sj, NEG_INF)
        m_new = jnp.maximum(m, jnp.max(sj, axis=-1, keepdims=True))
        alpha = jnp.exp(m - m_new)
        pj = jnp.exp(sj - m_new)
        l_new = alpha * l + jnp.sum(pj, axis=-1, keepdims=True)
        acc_new = alpha * acc + jnp.dot(pj.astype(v_ref.dtype), v_ref[pl.ds(off, bs), :],
                                        preferred_element_type=jnp.float32)
        return m_new, l_new, acc_new

    _, l, acc = lax.fori_loop(0, i, past_block, (m0, l0, acc0))
    o_ref[...] = (acc / l).astype(o_ref.dtype)


def _attention(z, kmean_p, nb, seq, q_col, k_col, v_col):
    n = z.shape[0]
    nq = seq // MOBA_BLOCK
    return pl.pallas_call(
        _attn_kernel,
        grid=(nb, N_HEADS, nq),
        in_specs=[pl.BlockSpec((MOBA_BLOCK, HEAD_DIM), lambda b, h, i: (b * nq + i, q_col + h)),
                  pl.BlockSpec((seq, HEAD_DIM), lambda b, h, i: (b, k_col + h)),
                  pl.BlockSpec((seq, HEAD_DIM), lambda b, h, i: (b, v_col + h)),
                  pl.BlockSpec((1, LANES, HEAD_DIM), lambda b, h, i: (b, 0, h))],
        out_specs=pl.BlockSpec((MOBA_BLOCK, HEAD_DIM), lambda b, h, i: (b * nq + i, h)),
        out_shape=jax.ShapeDtypeStruct((n, N_HEADS * HEAD_DIM), jnp.bfloat16),
        scratch_shapes=[pltpu.VMEM((MOBA_BLOCK, LANES), jnp.float32)],
        compiler_params=_cparams(("parallel", "parallel", "arbitrary")),
        name="attn",
    )(z, z, z, kmean_p)


def _conv_kernel(a_ref, b_ref, ha_ref, hb_ref, w_ref, bdw_ref, g_ref, bcn_ref, o_ref, hext_ref):
    i = pl.program_id(1)
    tt = a_ref.shape[0]
    halo = ha_ref.shape[0]
    glu_h = ha_ref[...].astype(jnp.float32) * _sigmoid(hb_ref[...].astype(jnp.float32))
    hext_ref[0:halo, :] = jnp.where(i > 0, glu_h, 0.0)
    hext_ref[halo:, :] = a_ref[...].astype(jnp.float32) * _sigmoid(b_ref[...].astype(jnp.float32))

    first = halo - (CONV_WIDTH - 1)
    acc = jnp.zeros(o_ref.shape, jnp.float32) + bdw_ref[...]
    for t in range(CONV_WIDTH):
        acc = acc + w_ref[t:t + 1, :] * hext_ref[pl.ds(first + t, tt), :]
    y = _layer_norm_rows(acc) * g_ref[...] + bcn_ref[...]
    o_ref[...] = (y * _sigmoid(y)).astype(o_ref.dtype)


def _conv(z, w_dw, b_dw, g_cn, b_cn, nb, seq, a_tile, b_tile):
    n = z.shape[0]
    ch = w_dw.shape[1]
    tt, halo = CONV_TT, CONV_HALO
    nt = seq // tt
    r = tt // halo

    def cur(tile):
        return pl.BlockSpec((tt, ch), lambda b, i: (b * nt + i, tile))

    def prev(tile):
        return pl.BlockSpec((halo, ch), lambda b, i: (jnp.maximum((b * nt + i) * r - 1, 0), tile))

    vec = pl.BlockSpec((1, ch), lambda b, i: (0, 0))
    return pl.pallas_call(
        _conv_kernel,
        grid=(nb, nt),
        in_specs=[cur(a_tile), cur(b_tile), prev(a_tile), prev(b_tile),
                  pl.BlockSpec((CONV_WIDTH, ch), lambda b, i: (0, 0)), vec, vec, vec],
        out_specs=pl.BlockSpec((tt, ch), lambda b, i: (b * nt + i, 0)),
        out_shape=jax.ShapeDtypeStruct((n, ch), jnp.bfloat16),
        scratch_shapes=[pltpu.VMEM((tt + halo, ch), jnp.float32)],
        compiler_params=_cparams(("parallel", "parallel")),
        name="conv",
    )(z, z, z, z, w_dw, b_dw.reshape(1, ch), g_cn.reshape(1, ch), b_cn.reshape(1, ch))


def _mix_kernel(hc_ref, o_ref, sgc0_ref, sgc1_ref, sga0_ref, sga1_ref, x_ref, wco_ref, bco_ref, wao_ref, wmx_ref,
                gt_ref, g1_ref, b1_ref, sc_ref, sh_ref, wr_ref, br_ref,
                x1_ref, u2_ref, ri_ref, rw_ref, *, alpha):
    f32 = jnp.float32
    y_conv = jnp.dot(hc_ref[...], wco_ref[...], preferred_element_type=f32) + bco_ref[...]
    y_attn = jnp.dot(o_ref[...], wao_ref[...], preferred_element_type=f32)
    sgc = jnp.concatenate([sgc0_ref[...], sgc1_ref[...]], axis=1).astype(f32)
    sga = jnp.concatenate([sga0_ref[...], sga1_ref[...]], axis=1).astype(f32)
    merged = sgc * y_conv + sga * y_attn
    t_out = jnp.dot(merged.astype(jnp.bfloat16), wmx_ref[...], preferred_element_type=f32)
    x1 = _layer_norm_rows(alpha * x_ref[...] + (1.0 + gt_ref[0]) * t_out) * g1_ref[...] + b1_ref[...]
    x1_ref[...] = x1
    u2 = _layer_norm_rows(x1) * (1.0 + sc_ref[0]) + sh_ref[0]
    u2_ref[...] = u2

    logits = jnp.dot(u2, wr_ref[...], preferred_element_type=f32,
                     precision=lax.Precision.HIGHEST) + br_ref[...]
    lane = lax.broadcasted_iota(jnp.int32, logits.shape, 1)
    is_grp = lane < N_GROUPS
    gl = jnp.where(is_grp, logits, -jnp.inf)
    gmax = jnp.max(gl, axis=-1, keepdims=True)
    gidx = jnp.min(jnp.where(gl == gmax, lane, LANES), axis=-1, keepdims=True)
    p_top = 1.0 / jnp.sum(jnp.where(is_grp, jnp.exp(logits - gmax), 0.0), axis=-1, keepdims=True)
    lo = N_GROUPS + gidx * EXPERTS_PER_GROUP
    el = jnp.where((lane >= lo) & (lane < lo + EXPERTS_PER_GROUP), logits, -jnp.inf)
    v1 = jnp.max(el, axis=-1, keepdims=True)
    i1 = jnp.min(jnp.where(el == v1, lane, LANES), axis=-1, keepdims=True)
    el = jnp.where(lane == i1, -jnp.inf, el)
    v2 = jnp.max(el, axis=-1, keepdims=True)
    i2 = jnp.min(jnp.where(el == v2, lane, LANES), axis=-1, keepdims=True)
    e2 = jnp.exp(v2 - v1)
    w1 = p_top / (1.0 + e2)
    w2 = p_top * e2 / (1.0 + e2)
    ri_ref[...] = jnp.where(lane == 0, i1 - N_GROUPS, jnp.where(lane == 1, i2 - N_GROUPS, 0))
    rw_ref[...] = jnp.where(lane == 0, w1, jnp.where(lane == 1, w2, 0.0))


def _mix(hc, o_attn, z, x2, wco, bco, wao, wmx, gt1, g1, b1, sc2, sh2, w_r, b_r, seq, gc_tile, ga_tile, alpha):
    n, d = x2.shape
    ch = hc.shape[1]
    tm = MIX_TM
    per_batch = seq // tm

    def rows(width):
        return pl.BlockSpec((tm, width), lambda i: (i, 0))

    def const(shape):
        return pl.BlockSpec(shape, lambda i: (0,) * len(shape), pipeline_mode=pl.Buffered(1))

    def per_b():
        return pl.BlockSpec((1, 1, d), lambda i: (i // per_batch, 0, 0))

    return pl.pallas_call(
        functools.partial(_mix_kernel, alpha=alpha),
        grid=(n // tm,),
        in_specs=[rows(ch), rows(ch),
                  pl.BlockSpec((tm, ch), lambda i: (i, gc_tile)),
                  pl.BlockSpec((tm, ch), lambda i: (i, gc_tile + 1)),
                  pl.BlockSpec((tm, ch), lambda i: (i, ga_tile)),
                  pl.BlockSpec((tm, ch), lambda i: (i, ga_tile + 1)),
                  rows(d),
                  const((ch, d)), const((1, d)), const((ch, d)), const((d, d)),
                  per_b(), const((1, d)), const((1, d)), per_b(), per_b(),
                  const((d, LANES)), const((1, LANES))],
        out_specs=[rows(d), rows(d), rows(LANES), rows(LANES)],
        out_shape=[jax.ShapeDtypeStruct((n, d), jnp.float32),
                   jax.ShapeDtypeStruct((n, d), jnp.float32),
                   jax.ShapeDtypeStruct((n, LANES), jnp.int32),
                   jax.ShapeDtypeStruct((n, LANES), jnp.float32)],
        compiler_params=_cparams(("parallel",)),
        name="mix",
    )(hc, o_attn, z, z, z, z, x2, wco, bco, wao, wmx, gt1, g1, b1, sc2, sh2, w_r, b_r)


def _meta_kernel(eid_ref, dest_ref, tile_ref, cnt_ref, base_ref, *, tile_rows):
    ph = pl.program_id(0)
    r = pl.program_id(1)
    f32 = jnp.float32
    blk = eid_ref.shape[2]
    e = eid_ref[0]
    sub = lax.broadcasted_iota(jnp.int32, (N_EXPERTS, blk), 0)
    oh = (sub == e).astype(f32)
    blk_cnt = jnp.broadcast_to(jnp.sum(oh, axis=1, keepdims=True), (N_EXPERTS, LANES))

    @pl.when((ph == 0) & (r == 0))
    def _():
        cnt_ref[...] = jnp.zeros_like(cnt_ref)

    @pl.when(ph == 0)
    def _():
        cnt_ref[...] += blk_cnt

    @pl.when((ph == 1) & (r == 0))
    def _():
        padded = jnp.ceil(cnt_ref[...] * (1.0 / tile_rows)) * tile_rows
        er = lax.broadcasted_iota(jnp.int32, (N_EXPERTS, N_EXPERTS), 0)
        ec = lax.broadcasted_iota(jnp.int32, (N_EXPERTS, N_EXPERTS), 1)
        starts = jnp.dot((ec < er).astype(f32), padded, preferred_element_type=f32,
                         precision=lax.Precision.HIGHEST)
        base_ref[...] = starts
        ends = starts + padded
        lane = lax.broadcasted_iota(jnp.int32, (N_EXPERTS, LANES), 1)
        tile_start = (lane * tile_rows).astype(f32)
        owner = jnp.sum((ends <= tile_start).astype(f32), axis=0, keepdims=True)
        total_tiles = jnp.max(ends, axis=0, keepdims=True) * (1.0 / tile_rows)
        lane1 = lax.broadcasted_iota(jnp.int32, (1, LANES), 1)
        tile_ref[...] = jnp.where(lane1 == LANES - 1, total_tiles, owner).astype(jnp.int32)

    @pl.when(ph == 1)
    def _():
        ar = lax.broadcasted_iota(jnp.int32, (blk, blk), 0)
        ac = lax.broadcasted_iota(jnp.int32, (blk, blk), 1)
        before = (ar < ac).astype(jnp.bfloat16)
        cum = jnp.dot(oh.astype(jnp.bfloat16), before, preferred_element_type=f32)
        slot = jnp.sum(oh * (base_ref[:, 0:1] + cum), axis=0, keepdims=True)
        dest_ref[0] = slot.astype(jnp.int32)
        base_ref[...] += blk_cnt


def _meta(eid_flat, tile_rows):
    na = eid_flat.shape[0]
    nblk = na // META_BLK
    eid3 = eid_flat.reshape(nblk, 1, META_BLK)
    dest, tiles = pl.pallas_call(
        functools.partial(_meta_kernel, tile_rows=tile_rows),
        grid=(2, nblk),
        in_specs=[pl.BlockSpec((1, 1, META_BLK), lambda ph, r: (r, 0, 0))],
        out_specs=[pl.BlockSpec((1, 1, META_BLK), lambda ph, r: (ph * r, 0, 0)),
                   pl.BlockSpec((1, LANES), lambda ph, r: (0, 0))],
        out_shape=[jax.ShapeDtypeStruct((nblk, 1, META_BLK), jnp.int32),
                   jax.ShapeDtypeStruct((1, LANES), jnp.int32)],
        scratch_shapes=[pltpu.VMEM((N_EXPERTS, LANES), jnp.float32),
                        pltpu.VMEM((N_EXPERTS, LANES), jnp.float32)],
        compiler_params=_cparams(("arbitrary", "arbitrary")),
        name="meta",
    )(eid3)
    return dest.reshape(na), tiles.reshape(LANES)


def _dispatch_kernel(dest_ref, u_ref, xs_in_ref, xs_ref, sem, *, n_tok):
    del xs_in_ref
    tm = u_ref.shape[0]
    base = pl.program_id(0) * tm

    def row_copy(r, k):
        d = dest_ref[k * n_tok + base + r]
        return pltpu.make_async_copy(u_ref.at[pl.ds(r, 1), :], xs_ref.at[pl.ds(d, 1), :], sem)

    def issue(r, c):
        for k in range(TOPK_IN_GROUP):
            row_copy(r, k).start()
        return c

    def drain(r, c):
        for k in range(TOPK_IN_GROUP):
            row_copy(r, k).wait()
        return c

    lax.fori_loop(0, tm, issue, 0)
    lax.fori_loop(0, tm, drain, 0)


def _dispatch(dest, u2, n_slots):
    n, d = u2.shape
    tm = ROW_TM
    xs0 = jnp.zeros((n_slots, d), u2.dtype)
    return pl.pallas_call(
        functools.partial(_dispatch_kernel, n_tok=n),
        grid_spec=pltpu.PrefetchScalarGridSpec(
            num_scalar_prefetch=1,
            grid=(n // tm,),
            in_specs=[pl.BlockSpec((tm, d), lambda i, dest: (i, 0)),
                      pl.BlockSpec(memory_space=pl.ANY)],
            out_specs=pl.BlockSpec(memory_space=pl.ANY),
            scratch_shapes=[pltpu.SemaphoreType.DMA(())]),
        out_shape=jax.ShapeDtypeStruct((n_slots, d), u2.dtype),
        input_output_aliases={2: 0},
        compiler_params=_cparams(("arbitrary",)),
        name="dispatch",
    )(dest, u2, xs0)


def _expert_kernel(tile_ref, x_ref, w1_ref, w3_ref, w2_ref, y_ref, w1b, w3b, w2b):
    t = pl.program_id(0)
    n_used = tile_ref[LANES - 1]
    prev = tile_ref[jnp.maximum(t - 1, 0)]
    changed = (t == 0) | (tile_ref[t] != prev)

    @pl.when((t < n_used) & changed)
    def _():
        w1b[...] = w1_ref[0].astype(w1b.dtype)
        w3b[...] = w3_ref[0].astype(w3b.dtype)
        w2b[...] = w2_ref[0].astype(w2b.dtype)

    @pl.when(t < n_used)
    def _():
        xb = x_ref[...].astype(jnp.bfloat16)
        h1 = jnp.dot(xb, w1b[...], preferred_element_type=jnp.float32)
        h3 = jnp.dot(xb, w3b[...], preferred_element_type=jnp.float32)
        hh = (h1 * _sigmoid(h1) * h3).astype(jnp.bfloat16)
        y_ref[...] = jnp.dot(hh, w2b[...], preferred_element_type=jnp.float32)

    @pl.when(t >= n_used)
    def _():
        y_ref[...] = jnp.zeros_like(y_ref)


def _experts(tiles, xs, w1, w3, w2):
    n_slots, d = xs.shape
    ne, _, f = w1.shape
    tm = EXP_TM
    n_tiles = n_slots // tm

    def row_idx(t, tiles):
        return (jnp.minimum(t, tiles[LANES - 1] - 1), 0)

    def w_idx(t, tiles):
        return (jnp.minimum(tiles[jnp.minimum(t, tiles[LANES - 1] - 1)], ne - 1), 0, 0)

    return pl.pallas_call(
        _expert_kernel,
        grid_spec=pltpu.PrefetchScalarGridSpec(
            num_scalar_prefetch=1,
            grid=(n_tiles,),
            in_specs=[pl.BlockSpec((tm, d), row_idx),
                      pl.BlockSpec((1, d, f), w_idx),
                      pl.BlockSpec((1, d, f), w_idx),
                      pl.BlockSpec((1, f, d), w_idx)],
            out_specs=pl.BlockSpec((tm, d), lambda t, tiles: (t, 0)),
            scratch_shapes=[pltpu.VMEM((d, f), jnp.bfloat16),
                            pltpu.VMEM((d, f), jnp.bfloat16),
                            pltpu.VMEM((f, d), jnp.bfloat16)]),
        out_shape=jax.ShapeDtypeStruct((n_slots, d), jnp.float32),
        compiler_params=_cparams(("arbitrary",)),
        name="experts",
    )(tiles, xs, w1, w3, w2)


def _combine_kernel(dest_ref, ys_ref, x1_ref, rw_ref, gt_ref, g2_ref, b2_ref, o_ref, ybuf, sem,
                    *, n_tok, alpha):
    tm = x1_ref.shape[0]
    base = pl.program_id(0) * tm

    def row_copy(r, k):
        d = dest_ref[k * n_tok + base + r]
        return pltpu.make_async_copy(ys_ref.at[pl.ds(d, 1), :], ybuf.at[k, pl.ds(r, 1), :], sem)

    def issue(r, c):
        for k in range(TOPK_IN_GROUP):
            row_copy(r, k).start()
        return c

    def drain(r, c):
        for k in range(TOPK_IN_GROUP):
            row_copy(r, k).wait()
        return c

    lax.fori_loop(0, tm, issue, 0)
    lax.fori_loop(0, tm, drain, 0)
    rw = rw_ref[...]
    f = rw[:, 0:1] * ybuf[0] + rw[:, 1:2] * ybuf[1]
    y = _layer_norm_rows(alpha * x1_ref[...] + (1.0 + gt_ref[0]) * f)
    o_ref[...] = y * g2_ref[...] + b2_ref[...]


def _combine(dest, ys, x1, rw, gt2, g2, b2, seq, alpha):
    n, d = x1.shape
    tm = ROW_TM
    per_batch = seq // tm
    return pl.pallas_call(
        functools.partial(_combine_kernel, n_tok=n, alpha=alpha),
        grid_spec=pltpu.PrefetchScalarGridSpec(
            num_scalar_prefetch=1,
            grid=(n // tm,),
            in_specs=[pl.BlockSpec(memory_space=pl.ANY),
                      pl.BlockSpec((tm, d), lambda i, dest: (i, 0)),
                      pl.BlockSpec((tm, LANES), lambda i, dest: (i, 0)),
                      pl.BlockSpec((1, 1, d), lambda i, dest: (i // per_batch, 0, 0)),
                      pl.BlockSpec((1, d), lambda i, dest: (0, 0)),
                      pl.BlockSpec((1, d), lambda i, dest: (0, 0))],
            out_specs=pl.BlockSpec((tm, d), lambda i, dest: (i, 0)),
            scratch_shapes=[pltpu.VMEM((TOPK_IN_GROUP, tm, d), jnp.float32),
                            pltpu.SemaphoreType.DMA(())]),
        out_shape=jax.ShapeDtypeStruct((n, d), jnp.float32),
        compiler_params=_cparams(("arbitrary",)),
        name="combine",
    )(dest, ys, x1, rw, gt2, g2, b2)


def kernel(x, c, positions, w_cond, b_cond, w_in, b_glu, w_dw, b_dw, g_cn, b_cn, w_conv_out, b_conv_out, w_attn_out, w_mix_out, g_ln1, b_ln1, w_grp, b_grp, w_erouter, b_erouter, w1, w3, w2, g_ln2, b_ln2):
    nb, seq, d = x.shape
    depth = w_cond.shape[0]
    n = nb * seq
    ch = w_dw.shape[2]
    aw = N_HEADS * HEAD_DIM
    in_cols = w_in.shape[2]
    alpha = (2.0 * depth) ** 0.25
    bf16 = jnp.bfloat16
    assert 2 * ch + 3 * aw + 2 * d == in_cols and ch == PROJ_TN and aw == PROJ_TN and d == 2 * PROJ_TN
    assert seq % MOBA_BLOCK == 0 and seq // MOBA_BLOCK <= LANES and N_GROUPS + N_EXPERTS <= LANES
    t_a, t_b, t_q, t_k, t_v, t_gc, t_ga = 0, 1, 2, 3, 4, 5, 7
    heads_per_tile = PROJ_TN // HEAD_DIM
    nblk = seq // MOBA_BLOCK
    n_slots = n * TOPK_IN_GROUP + N_EXPERTS * EXP_TM
    assert n_slots // EXP_TM < LANES

    cos_t, sin_t = _rope_tables(positions)
    x2 = x.reshape(n, d)
    for l in range(depth):
        mod = _mod(c, w_cond[l], b_cond[l]).reshape(nb, 6, 1, d)
        sh1, sc1, gt1, sh2, sc2, gt2 = (mod[:, s] for s in range(6))

        bias = jnp.concatenate([b_glu[l], jnp.zeros((in_cols - 2 * ch,), jnp.float32)]).reshape(1, in_cols)
        z = _in_proj(x2, sc1, sh1, w_in[l].astype(bf16), bias, cos_t, sin_t, seq,
                     rope_tiles=(t_q, t_k + 1), sig_from=t_gc)
        km = _kmean(z, nb * nblk, t_k, aw).reshape(nb, nblk, aw)
        km = jnp.pad(km, ((0, 0), (0, LANES - nblk), (0, 0))).astype(bf16)
        o_attn = _attention(z, km, nb, seq, t_q * heads_per_tile, t_k * heads_per_tile, t_v * heads_per_tile)
        hc = _conv(z, w_dw[l], b_dw[l], g_cn[l], b_cn[l], nb, seq, t_a, t_b)

        w_r = jnp.concatenate([w_grp[l], w_erouter[l],
                               jnp.zeros((d, LANES - N_GROUPS - N_EXPERTS), jnp.float32)], axis=1)
        b_r = jnp.concatenate([b_grp[l], b_erouter[l],
                               jnp.zeros((LANES - N_GROUPS - N_EXPERTS,), jnp.float32)]).reshape(1, LANES)
        x1, u2, ri, rw = _mix(hc, o_attn, z, x2, w_conv_out[l].astype(bf16), b_conv_out[l].reshape(1, d),
                              w_attn_out[l].astype(bf16), w_mix_out[l].astype(bf16), gt1,
                              g_ln1[l].reshape(1, d), b_ln1[l].reshape(1, d), sc2, sh2, w_r, b_r,
                              seq, t_gc, t_ga, alpha)

        eid = ri[:, :TOPK_IN_GROUP].T.reshape(TOPK_IN_GROUP * n)
        dest, tiles = _meta(eid, EXP_TM)
        xs = _dispatch(dest, u2, n_slots)
        f_in = w1.shape[-1]
        ys = _experts(tiles, xs, w1[l].reshape(N_EXPERTS, d, f_in), w3[l].reshape(N_EXPERTS, d, f_in),
                      w2[l].reshape(N_EXPERTS, f_in, d))
        x2 = _combine(dest, ys, x1, rw, gt2, g_ln2[l].reshape(1, d), b_ln2[l].reshape(1, d), seq, alpha)
    return x2.reshape(nb, seq, d)
```

```python
import functools

import jax
import jax.numpy as jnp
from jax import lax
from jax.experimental import pallas as pl
from jax.experimental.pallas import tpu as pltpu

CONV_WIDTH = 31
N_HEADS = 8
HEAD_DIM = 128
ROPE_DIM = HEAD_DIM // 4
ROPE_THETA = 500000.0
MOBA_BLOCK = 256
MOBA_TOPK = 3
N_GROUPS = 4
EXPERTS_PER_GROUP = 8
N_EXPERTS = N_GROUPS * EXPERTS_PER_GROUP
TOPK_IN_GROUP = 2
LN_EPS = 1e-5
NEG_INF = -1e30

LANES = 128
SUBLANES = 8
VMEM_LIMIT = 56 * 1024 * 1024

PROJ_TM = 512
PROJ_TN = 1024
CONV_TT = 256
CONV_HALO = 32
ATTN_GROUP = 4
ATTN_HEADS = 2
MIX_TM = 256
EXP_TM = 256
ROW_TM = 256
META_BLK = 512


def _cparams(sem, vmem=VMEM_LIMIT):
    return pltpu.CompilerParams(dimension_semantics=sem, vmem_limit_bytes=vmem)


def _layer_norm_rows(x):
    mu = jnp.mean(x, axis=-1, keepdims=True)
    xc = x - mu
    var = jnp.mean(xc * xc, axis=-1, keepdims=True)
    return xc * lax.rsqrt(var + LN_EPS)


def _sigmoid(x):
    return 1.0 / (1.0 + jnp.exp(-x))


def _dot_nt(a, b):
    return lax.dot_general(a, b, (((1,), (1,)), ((), ())), preferred_element_type=jnp.float32)


def _mod_kernel(cb_ref, w_ref, b_ref, o_ref, s_ref):
    @pl.when(pl.program_id(0) == 0)
    def _():
        cb = cb_ref[...]
        s_ref[...] = cb * _sigmoid(cb)

    nb, d, _ = cb_ref.shape
    tn = w_ref.shape[1]
    for cc in range(tn // LANES):
        sl = slice(cc * LANES, (cc + 1) * LANES)
        wc = w_ref[:, sl]
        for b in range(nb):
            p = (wc * s_ref[b]).reshape(d // SUBLANES, SUBLANES, LANES)
            r = jnp.sum(jnp.sum(p, axis=0), axis=0, keepdims=True)
            o_ref[b:b + 1, sl] = r + b_ref[:, sl]


def _mod(c, w_cond, b_cond, tn=512):
    nb, d = c.shape
    n_out = w_cond.shape[1]
    cb = jnp.broadcast_to(c[:, :, None], (nb, d, LANES))
    return pl.pallas_call(
        _mod_kernel,
        grid=(n_out // tn,),
        in_specs=[pl.BlockSpec((nb, d, LANES), lambda j: (0, 0, 0)),
                  pl.BlockSpec((d, tn), lambda j: (0, j)),
                  pl.BlockSpec((1, tn), lambda j: (0, j))],
        out_specs=pl.BlockSpec((nb, tn), lambda j: (0, j)),
        out_shape=jax.ShapeDtypeStruct((nb, n_out), jnp.float32),
        scratch_shapes=[pltpu.VMEM((nb, d, LANES), jnp.float32)],
        compiler_params=_cparams(("arbitrary",)),
        name="mod",
    )(cb, w_cond, b_cond.reshape(1, n_out))


def _rope_tab_kernel(pos_ref, invf_ref, sign_ref, cos_ref, sin_ref):
    ang = pos_ref[...] * invf_ref[...]
    cos_ref[...] = jnp.cos(ang)
    sin_ref[...] = jnp.sin(ang) * sign_ref[...]


def _rope_tables(positions, tm=1024):
    n = positions.size
    half = ROPE_DIM // 2
    inv_freq = jnp.power(ROPE_THETA, -jnp.arange(half, dtype=jnp.float32) / half)
    invf = jnp.concatenate([inv_freq, inv_freq, jnp.zeros((LANES - ROPE_DIM,), jnp.float32)]).reshape(1, LANES)
    sign = jnp.concatenate([-jnp.ones((half,), jnp.float32),
                            jnp.ones((LANES - half,), jnp.float32)]).reshape(1, LANES)
    posb = jnp.broadcast_to(positions.astype(jnp.float32).reshape(n, 1), (n, LANES))
    vec = pl.BlockSpec((1, LANES), lambda i: (0, 0))
    tab = pl.BlockSpec((tm, LANES), lambda i: (i, 0))
    return pl.pallas_call(
        _rope_tab_kernel,
        grid=(n // tm,),
        in_specs=[tab, vec, vec],
        out_specs=[tab, tab],
        out_shape=[jax.ShapeDtypeStruct((n, LANES), jnp.float32)] * 2,
        compiler_params=_cparams(("parallel",)),
        name="rope_tab",
    )(posb, invf, sign)


def _in_proj_kernel(x_ref, sc_ref, sh_ref, w_ref, bias_ref, cos_ref, sin_ref, o_ref, u_ref,
                    *, rope_tiles, sig_from):
    j = pl.program_id(1)

    @pl.when(j == 0)
    def _():
        xn = _layer_norm_rows(x_ref[...])
        u_ref[...] = (xn * (1.0 + sc_ref[0]) + sh_ref[0]).astype(u_ref.dtype)

    acc = jnp.dot(u_ref[...], w_ref[...], preferred_element_type=jnp.float32) + bias_ref[...]
    is_rope = (j >= rope_tiles[0]) & (j < rope_tiles[1])

    @pl.when(is_rope)
    def _():
        cosf = cos_ref[...]
        sinf = sin_ref[...]
        lane = lax.broadcasted_iota(jnp.int32, cosf.shape, 1)
        first = lane < (ROPE_DIM // 2)
        for h in range(acc.shape[1] // HEAD_DIM):
            sl = slice(h * HEAD_DIM, (h + 1) * HEAD_DIM)
            xh = acc[:, sl]
            partner = jnp.where(first, pltpu.roll(xh, HEAD_DIM - ROPE_DIM // 2, 1),
                                pltpu.roll(xh, ROPE_DIM // 2, 1))
            o_ref[:, sl] = (xh * cosf + partner * sinf).astype(o_ref.dtype)

    @pl.when(j >= sig_from)
    def _():
        o_ref[...] = _sigmoid(acc).astype(o_ref.dtype)

    @pl.when(jnp.logical_not(is_rope) & (j < sig_from))
    def _():
        o_ref[...] = acc.astype(o_ref.dtype)


def _in_proj(x2, sc1, sh1, w_bf, bias, cos_t, sin_t, seq, rope_tiles, sig_from):
    n, d = x2.shape
    cols = w_bf.shape[1]
    tm, tn = PROJ_TM, PROJ_TN
    per_batch = seq // tm
    kern = functools.partial(_in_proj_kernel, rope_tiles=rope_tiles, sig_from=sig_from)
    return pl.pallas_call(
        kern,
        grid=(n // tm, cols // tn),
        in_specs=[pl.BlockSpec((tm, d), lambda i, j: (i, 0)),
                  pl.BlockSpec((1, 1, d), lambda i, j: (i // per_batch, 0, 0)),
                  pl.BlockSpec((1, 1, d), lambda i, j: (i // per_batch, 0, 0)),
                  pl.BlockSpec((d, tn), lambda i, j: (0, j)),
                  pl.BlockSpec((1, tn), lambda i, j: (0, j)),
                  pl.BlockSpec((tm, LANES), lambda i, j: (i, 0)),
                  pl.BlockSpec((tm, LANES), lambda i, j: (i, 0))],
        out_specs=pl.BlockSpec((tm, tn), lambda i, j: (i, j)),
        out_shape=jax.ShapeDtypeStruct((n, cols), jnp.bfloat16),
        scratch_shapes=[pltpu.VMEM((tm, d), jnp.bfloat16)],
        compiler_params=_cparams(("parallel", "arbitrary")),
        name="in_proj",
    )(x2, sc1, sh1, w_bf, bias, cos_t, sin_t)


def _kmean_kernel(k_ref, o_ref):
    k = k_ref[...].astype(jnp.float32)
    o_ref[0] = jnp.sum(k, axis=0, keepdims=True) * (1.0 / k.shape[0])


def _kmean(z, n_blocks, k_tile, width):
    return pl.pallas_call(
        _kmean_kernel,
        grid=(n_blocks,),
        in_specs=[pl.BlockSpec((MOBA_BLOCK, width), lambda i: (i, k_tile))],
        out_specs=pl.BlockSpec((1, 1, width), lambda i: (i, 0, 0)),
        out_shape=jax.ShapeDtypeStruct((n_blocks, 1, width), jnp.float32),
        compiler_params=_cparams(("parallel",)),
        name="kmean",
    )(z)


def _gate_kernel(q_ref, km_ref, qx_ref):
    i = pl.program_id(1)
    bs = q_ref.shape[0]
    nblk = pl.num_programs(1)
    lane = lax.broadcasted_iota(jnp.int32, (bs, LANES), 1)
    past = lane < i
    for h in range(N_HEADS):
        hs = slice(h * HEAD_DIM, (h + 1) * HEAD_DIM)
        q = q_ref[:, hs]
        g = jnp.where(past, _dot_nt(q, km_ref[0, :, hs]), -jnp.inf)
        bias = jnp.where((lane < nblk) & (lane != i), NEG_INF, 0.0)
        for _ in range(MOBA_TOPK):
            m = jnp.max(g, axis=-1, keepdims=True)
            idx = jnp.min(jnp.where(g == m, lane, LANES), axis=-1, keepdims=True)
            hit = lane == idx
            bias = jnp.where(hit & past, 0.0, bias)
            g = jnp.where(hit, -jnp.inf, g)
        qx_ref[:, 2 * h * HEAD_DIM:(2 * h + 1) * HEAD_DIM] = q
        qx_ref[:, (2 * h + 1) * HEAD_DIM:(2 * h + 2) * HEAD_DIM] = bias.astype(qx_ref.dtype)


def _gate(z, kmean_p, nb, seq, q_tile):
    n = z.shape[0]
    nq = seq // MOBA_BLOCK
    aw = N_HEADS * HEAD_DIM
    return pl.pallas_call(
        _gate_kernel,
        grid=(nb, nq),
        in_specs=[pl.BlockSpec((MOBA_BLOCK, aw), lambda b, i: (b * nq + i, q_tile)),
                  pl.BlockSpec((1, LANES, aw), lambda b, i: (b, 0, 0))],
        out_specs=pl.BlockSpec((MOBA_BLOCK, 2 * aw), lambda b, i: (b * nq + i, 0)),
        out_shape=jax.ShapeDtypeStruct((n, 2 * aw), jnp.bfloat16),
        compiler_params=_cparams(("parallel", "arbitrary")),
        name="gate",
    )(z, kmean_p)


def _attn_kernel(qx_ref, k_ref, v_ref, o_ref, kx_ref, vx_ref, s_ref, p_ref):
    i = pl.program_id(2)
    bs = MOBA_BLOCK
    nblk = k_ref.shape[0] // bs
    wide = HEAD_DIM + LANES
    bf16 = jnp.bfloat16
    f32 = jnp.float32
    exp2_scale = (HEAD_DIM ** -0.5) * 1.4426950408889634
    grp = ATTN_GROUP * bs

    @pl.when(i == 0)
    def _():
        sub_k = lax.broadcasted_iota(jnp.int32, (LANES, bs), 0)
        for hh in range(ATTN_HEADS):
            hs = slice(hh * HEAD_DIM, (hh + 1) * HEAD_DIM)
            vx_ref[hh, :, :HEAD_DIM] = v_ref[:, hs]
            vx_ref[hh, :, HEAD_DIM:] = jnp.ones((nblk * bs, LANES), bf16)
            for jb in range(nblk):
                cs = slice(jb * bs, (jb + 1) * bs)
                kx_ref[hh, :HEAD_DIM, cs] = k_ref[cs, hs].astype(f32).T.astype(bf16)
                kx_ref[hh, HEAD_DIM:, cs] = jnp.where(sub_k == jb, 1.0, 0.0).astype(bf16)

    for c in range(nblk // ATTN_GROUP):
        @pl.when(i // ATTN_GROUP == c)
        def _(c=c):
            nk = (c + 1) * grp
            row = lax.broadcasted_iota(jnp.int32, (bs, LANES), 0) + i * bs
            col = lax.broadcasted_iota(jnp.int32, (bs, LANES), 1)
            for hh in range(ATTN_HEADS):
                s_ref[hh, :, :nk] = jnp.dot(qx_ref[:, hh * wide:(hh + 1) * wide], kx_ref[hh, :, 0:nk],
                                            preferred_element_type=f32)
            for hh in range(ATTN_HEADS):
                mx = None
                for t in range(nk // LANES):
                    ts = slice(t * LANES, (t + 1) * LANES)
                    st = s_ref[hh, :, ts]
                    if t * LANES >= c * grp:
                        st = jnp.where(col + t * LANES <= row, st, NEG_INF)
                        s_ref[hh, :, ts] = st
                    mx = st if mx is None else jnp.maximum(mx, st)
                mb = jnp.broadcast_to(jnp.max(mx, axis=-1, keepdims=True) * exp2_scale, (bs, LANES))
                for t in range(nk // LANES):
                    ts = slice(t * LANES, (t + 1) * LANES)
                    p_ref[hh, :, ts] = jnp.exp2(s_ref[hh, :, ts] * exp2_scale - mb).astype(bf16)
            for hh in range(ATTN_HEADS):
                acc = jnp.dot(p_ref[hh, :, :nk], vx_ref[hh, 0:nk, :], preferred_element_type=f32)
                o_ref[:, hh * HEAD_DIM:(hh + 1) * HEAD_DIM] = (
                    acc[:, :HEAD_DIM] / acc[:, HEAD_DIM:]).astype(o_ref.dtype)


def _attention(qx, z, nb, seq, k_col, v_col):
    n = z.shape[0]
    nq = seq // MOBA_BLOCK
    bs = MOBA_BLOCK
    wide = HEAD_DIM + LANES
    hps = ATTN_HEADS
    assert N_HEADS % hps == 0 and k_col % hps == 0 and v_col % hps == 0
    return pl.pallas_call(
        _attn_kernel,
        grid=(nb, N_HEADS // hps, nq),
        in_specs=[pl.BlockSpec((bs, hps * wide), lambda b, h, i: (b * nq + i, h)),
                  pl.BlockSpec((seq, hps * HEAD_DIM), lambda b, h, i: (b, k_col // hps + h)),
                  pl.BlockSpec((seq, hps * HEAD_DIM), lambda b, h, i: (b, v_col // hps + h))],
        out_specs=pl.BlockSpec((bs, hps * HEAD_DIM), lambda b, h, i: (b * nq + i, h)),
        out_shape=jax.ShapeDtypeStruct((n, N_HEADS * HEAD_DIM), jnp.bfloat16),
        scratch_shapes=[pltpu.VMEM((hps, wide, seq), jnp.bfloat16),
                        pltpu.VMEM((hps, seq, wide), jnp.bfloat16),
                        pltpu.VMEM((hps, bs, seq), jnp.float32),
                        pltpu.VMEM((hps, bs, seq), jnp.bfloat16)],
        compiler_params=_cparams(("parallel", "parallel", "arbitrary")),
        name="attn",
    )(qx, z, z)


def _conv_kernel(a_ref, b_ref, ha_ref, hb_ref, w_ref, bdw_ref, g_ref, bcn_ref, o_ref, hext_ref):
    i = pl.program_id(1)
    tt = a_ref.shape[0]
    halo = ha_ref.shape[0]
    glu_h = ha_ref[...].astype(jnp.float32) * _sigmoid(hb_ref[...].astype(jnp.float32))
    hext_ref[0:halo, :] = jnp.where(i > 0, glu_h, 0.0)
    hext_ref[halo:, :] = a_ref[...].astype(jnp.float32) * _sigmoid(b_ref[...].astype(jnp.float32))

    first = halo - (CONV_WIDTH - 1)
    acc = jnp.zeros(o_ref.shape, jnp.float32) + bdw_ref[...]
    for t in range(CONV_WIDTH):
        acc = acc + w_ref[t:t + 1, :] * hext_ref[pl.ds(first + t, tt), :]
    y = _layer_norm_rows(acc) * g_ref[...] + bcn_ref[...]
    o_ref[...] = (y * _sigmoid(y)).astype(o_ref.dtype)


def _conv(z, w_dw, b_dw, g_cn, b_cn, nb, seq, a_tile, b_tile):
    n = z.shape[0]
    ch = w_dw.shape[1]
    tt, halo = CONV_TT, CONV_HALO
    nt = seq // tt
    r = tt // halo

    def cur(tile):
        return pl.BlockSpec((tt, ch), lambda b, i: (b * nt + i, tile))

    def prev(tile):
        return pl.BlockSpec((halo, ch), lambda b, i: (jnp.maximum((b * nt + i) * r - 1, 0), tile))

    vec = pl.BlockSpec((1, ch), lambda b, i: (0, 0))
    return pl.pallas_call(
        _conv_kernel,
        grid=(nb, nt),
        in_specs=[cur(a_tile), cur(b_tile), prev(a_tile), prev(b_tile),
                  pl.BlockSpec((CONV_WIDTH, ch), lambda b, i: (0, 0)), vec, vec, vec],
        out_specs=pl.BlockSpec((tt, ch), lambda b, i: (b * nt + i, 0)),
        out_shape=jax.ShapeDtypeStruct((n, ch), jnp.bfloat16),
        scratch_shapes=[pltpu.VMEM((tt + halo, ch), jnp.float32)],
        compiler_params=_cparams(("parallel", "parallel")),
        name="conv",
    )(z, z, z, z, w_dw, b_dw.reshape(1, ch), g_cn.reshape(1, ch), b_cn.reshape(1, ch))


def _mix_kernel(hc_ref, o_ref, sgc0_ref, sgc1_ref, sga0_ref, sga1_ref, x_ref, wco_ref, bco_ref, wao_ref, wmx_ref,
                gt_ref, g1_ref, b1_ref, sc_ref, sh_ref, wr_ref, br_ref,
                x1_ref, u2_ref, ri_ref, rw_ref, *, alpha):
    f32 = jnp.float32
    y_conv = jnp.dot(hc_ref[...], wco_ref[...], preferred_element_type=f32) + bco_ref[...]
    y_attn = jnp.dot(o_ref[...], wao_ref[...], preferred_element_type=f32)
    sgc = jnp.concatenate([sgc0_ref[...], sgc1_ref[...]], axis=1).astype(f32)
    sga = jnp.concatenate([sga0_ref[...], sga1_ref[...]], axis=1).astype(f32)
    merged = sgc * y_conv + sga * y_attn
    t_out = jnp.dot(merged.astype(jnp.bfloat16), wmx_ref[...], preferred_element_type=f32)
    x1 = _layer_norm_rows(alpha * x_ref[...] + (1.0 + gt_ref[0]) * t_out) * g1_ref[...] + b1_ref[...]
    x1_ref[...] = x1
    u2 = _layer_norm_rows(x1) * (1.0 + sc_ref[0]) + sh_ref[0]
    u2_ref[...] = u2

    logits = jnp.dot(u2, wr_ref[...], preferred_element_type=f32,
                     precision=lax.Precision.HIGHEST) + br_ref[...]
    lane = lax.broadcasted_iota(jnp.int32, logits.shape, 1)
    is_grp = lane < N_GROUPS
    gl = jnp.where(is_grp, logits, -jnp.inf)
    gmax = jnp.max(gl, axis=-1, keepdims=True)
    gidx = jnp.min(jnp.where(gl == gmax, lane, LANES), axis=-1, keepdims=True)
    p_top = 1.0 / jnp.sum(jnp.where(is_grp, jnp.exp(logits - gmax), 0.0), axis=-1, keepdims=True)
    lo = N_GROUPS + gidx * EXPERTS_PER_GROUP
    el = jnp.where((lane >= lo) & (lane < lo + EXPERTS_PER_GROUP), logits, -jnp.inf)
    v1 = jnp.max(el, axis=-1, keepdims=True)
    i1 = jnp.min(jnp.where(el == v1, lane, LANES), axis=-1, keepdims=True)
    el = jnp.where(lane == i1, -jnp.inf, el)
    v2 = jnp.max(el, axis=-1, keepdims=True)
    i2 = jnp.min(jnp.where(el == v2, lane, LANES), axis=-1, keepdims=True)
    e2 = jnp.exp(v2 - v1)
    w1 = p_top / (1.0 + e2)
    w2 = p_top * e2 / (1.0 + e2)
    ri_ref[...] = jnp.where(lane == 0, i1 - N_GROUPS, jnp.where(lane == 1, i2 - N_GROUPS, 0))
    rw_ref[...] = jnp.where(lane == 0, w1, jnp.where(lane == 1, w2, 0.0))


def _mix(hc, o_attn, z, x2, wco, bco, wao, wmx, gt1, g1, b1, sc2, sh2, w_r, b_r, seq, gc_tile, ga_tile, alpha):
    n, d = x2.shape
    ch = hc.shape[1]
    tm = MIX_TM
    per_batch = seq // tm

    def rows(width):
        return pl.BlockSpec((tm, width), lambda i: (i, 0))

    def const(shape):
        return pl.BlockSpec(shape, lambda i: (0,) * len(shape), pipeline_mode=pl.Buffered(1))

    def per_b():
        return pl.BlockSpec((1, 1, d), lambda i: (i // per_batch, 0, 0))

    return pl.pallas_call(
        functools.partial(_mix_kernel, alpha=alpha),
        grid=(n // tm,),
        in_specs=[rows(ch), rows(ch),
                  pl.BlockSpec((tm, ch), lambda i: (i, gc_tile)),
                  pl.BlockSpec((tm, ch), lambda i: (i, gc_tile + 1)),
                  pl.BlockSpec((tm, ch), lambda i: (i, ga_tile)),
                  pl.BlockSpec((tm, ch), lambda i: (i, ga_tile + 1)),
                  rows(d),
                  const((ch, d)), const((1, d)), const((ch, d)), const((d, d)),
                  per_b(), const((1, d)), const((1, d)), per_b(), per_b(),
                  const((d, LANES)), const((1, LANES))],
        out_specs=[rows(d), rows(d), rows(LANES), rows(LANES)],
        out_shape=[jax.ShapeDtypeStruct((n, d), jnp.float32),
                   jax.ShapeDtypeStruct((n, d), jnp.float32),
                   jax.ShapeDtypeStruct((n, LANES), jnp.int32),
                   jax.ShapeDtypeStruct((n, LANES), jnp.float32)],
        compiler_params=_cparams(("parallel",)),
        name="mix",
    )(hc, o_attn, z, z, z, z, x2, wco, bco, wao, wmx, gt1, g1, b1, sc2, sh2, w_r, b_r)


def _meta_kernel(eid_ref, dest_ref, tile_ref, cnt_ref, base_ref, *, tile_rows):
    ph = pl.program_id(0)
    r = pl.program_id(1)
    f32 = jnp.float32
    blk = eid_ref.shape[2]
    e = eid_ref[0]
    sub = lax.broadcasted_iota(jnp.int32, (N_EXPERTS, blk), 0)
    oh = (sub == e).astype(f32)
    blk_cnt = jnp.broadcast_to(jnp.sum(oh, axis=1, keepdims=True), (N_EXPERTS, LANES))

    @pl.when((ph == 0) & (r == 0))
    def _():
        cnt_ref[...] = jnp.zeros_like(cnt_ref)

    @pl.when(ph == 0)
    def _():
        cnt_ref[...] += blk_cnt

    @pl.when((ph == 1) & (r == 0))
    def _():
        padded = jnp.ceil(cnt_ref[...] * (1.0 / tile_rows)) * tile_rows
        er = lax.broadcasted_iota(jnp.int32, (N_EXPERTS, N_EXPERTS), 0)
        ec = lax.broadcasted_iota(jnp.int32, (N_EXPERTS, N_EXPERTS), 1)
        starts = jnp.dot((ec < er).astype(f32), padded, preferred_element_type=f32,
                         precision=lax.Precision.HIGHEST)
        base_ref[...] = starts
        ends = starts + padded
        lane = lax.broadcasted_iota(jnp.int32, (N_EXPERTS, LANES), 1)
        tile_start = (lane * tile_rows).astype(f32)
        owner = jnp.sum((ends <= tile_start).astype(f32), axis=0, keepdims=True)
        total_tiles = jnp.max(ends, axis=0, keepdims=True) * (1.0 / tile_rows)
        lane1 = lax.broadcasted_iota(jnp.int32, (1, LANES), 1)
        tile_ref[...] = jnp.where(lane1 == LANES - 1, total_tiles, owner).astype(jnp.int32)

    @pl.when(ph == 1)
    def _():
        ar = lax.broadcasted_iota(jnp.int32, (blk, blk), 0)
        ac = lax.broadcasted_iota(jnp.int32, (blk, blk), 1)
        before = (ar < ac).astype(jnp.bfloat16)
        cum = jnp.dot(oh.astype(jnp.bfloat16), before, preferred_element_type=f32)
        slot = jnp.sum(oh * (base_ref[:, 0:1] + cum), axis=0, keepdims=True)
        dest_ref[0] = slot.astype(jnp.int32)
        base_ref[...] += blk_cnt


def _meta(eid_flat, tile_rows):
    na = eid_flat.shape[0]
    nblk = na // META_BLK
    eid3 = eid_flat.reshape(nblk, 1, META_BLK)
    dest, tiles = pl.pallas_call(
        functools.partial(_meta_kernel, tile_rows=tile_rows),
        grid=(2, nblk),
        in_specs=[pl.BlockSpec((1, 1, META_BLK), lambda ph, r: (r, 0, 0))],
        out_specs=[pl.BlockSpec((1, 1, META_BLK), lambda ph, r: (ph * r, 0, 0)),
                   pl.BlockSpec((1, LANES), lambda ph, r: (0, 0))],
        out_shape=[jax.ShapeDtypeStruct((nblk, 1, META_BLK), jnp.int32),
                   jax.ShapeDtypeStruct((1, LANES), jnp.int32)],
        scratch_shapes=[pltpu.VMEM((N_EXPERTS, LANES), jnp.float32),
                        pltpu.VMEM((N_EXPERTS, LANES), jnp.float32)],
        compiler_params=_cparams(("arbitrary", "arbitrary")),
        name="meta",
    )(eid3)
    return dest.reshape(na), tiles.reshape(LANES)


def _dispatch_kernel(dest_ref, u_ref, xs_in_ref, xs_ref, sem, *, n_tok):
    del xs_in_ref
    tm = u_ref.shape[0]
    base = pl.program_id(0) * tm

    def row_copy(r, k):
        d = dest_ref[k * n_tok + base + r]
        return pltpu.make_async_copy(u_ref.at[pl.ds(r, 1), :], xs_ref.at[pl.ds(d, 1), :], sem)

    def issue(r, c):
        for k in range(TOPK_IN_GROUP):
            row_copy(r, k).start()
        return c

    def drain(r, c):
        for k in range(TOPK_IN_GROUP):
            row_copy(r, k).wait()
        return c

    lax.fori_loop(0, tm, issue, 0)
    lax.fori_loop(0, tm, drain, 0)


def _dispatch(dest, u2, n_slots):
    n, d = u2.shape
    tm = ROW_TM
    xs0 = jnp.zeros((n_slots, d), u2.dtype)
    return pl.pallas_call(
        functools.partial(_dispatch_kernel, n_tok=n),
        grid_spec=pltpu.PrefetchScalarGridSpec(
            num_scalar_prefetch=1,
            grid=(n // tm,),
            in_specs=[pl.BlockSpec((tm, d), lambda i, dest: (i, 0)),
                      pl.BlockSpec(memory_space=pl.ANY)],
            out_specs=pl.BlockSpec(memory_space=pl.ANY),
            scratch_shapes=[pltpu.SemaphoreType.DMA(())]),
        out_shape=jax.ShapeDtypeStruct((n_slots, d), u2.dtype),
        input_output_aliases={2: 0},
        compiler_params=_cparams(("arbitrary",)),
        name="dispatch",
    )(dest, u2, xs0)


def _expert_kernel(tile_ref, x_ref, w1_ref, w3_ref, w2_ref, y_ref, w1b, w3b, w2b):
    t = pl.program_id(0)
    n_used = tile_ref[LANES - 1]
    prev = tile_ref[jnp.maximum(t - 1, 0)]
    changed = (t == 0) | (tile_ref[t] != prev)

    @pl.when((t < n_used) & changed)
    def _():
        w1b[...] = w1_ref[0].astype(w1b.dtype)
        w3b[...] = w3_ref[0].astype(w3b.dtype)
        w2b[...] = w2_ref[0].astype(w2b.dtype)

    @pl.when(t < n_used)
    def _():
        xb = x_ref[...].astype(jnp.bfloat16)
        h1 = jnp.dot(xb, w1b[...], preferred_element_type=jnp.float32)
        h3 = jnp.dot(xb, w3b[...], preferred_element_type=jnp.float32)
        hh = (h1 * _sigmoid(h1) * h3).astype(jnp.bfloat16)
        y_ref[...] = jnp.dot(hh, w2b[...], preferred_element_type=jnp.float32)

    @pl.when(t >= n_used)
    def _():
        y_ref[...] = jnp.zeros_like(y_ref)


def _experts(tiles, xs, w1, w3, w2):
    n_slots, d = xs.shape
    ne, _, f = w1.shape
    tm = EXP_TM
    n_tiles = n_slots // tm

    def row_idx(t, tiles):
        return (jnp.minimum(t, tiles[LANES - 1] - 1), 0)

    def w_idx(t, tiles):
        return (jnp.minimum(tiles[jnp.minimum(t, tiles[LANES - 1] - 1)], ne - 1), 0, 0)

    return pl.pallas_call(
        _expert_kernel,
        grid_spec=pltpu.PrefetchScalarGridSpec(
            num_scalar_prefetch=1,
            grid=(n_tiles,),
            in_specs=[pl.BlockSpec((tm, d), row_idx),
                      pl.BlockSpec((1, d, f), w_idx),
                      pl.BlockSpec((1, d, f), w_idx),
                      pl.BlockSpec((1, f, d), w_idx)],
            out_specs=pl.BlockSpec((tm, d), lambda t, tiles: (t, 0)),
            scratch_shapes=[pltpu.VMEM((d, f), jnp.bfloat16),
                            pltpu.VMEM((d, f), jnp.bfloat16),
                            pltpu.VMEM((f, d), jnp.bfloat16)]),
        out_shape=jax.ShapeDtypeStruct((n_slots, d), jnp.float32),
        compiler_params=_cparams(("arbitrary",)),
        name="experts",
    )(tiles, xs, w1, w3, w2)


def _combine_kernel(dest_ref, ys_ref, x1_ref, rw_ref, gt_ref, g2_ref, b2_ref, o_ref, ybuf, sem,
                    *, n_tok, alpha):
    tm = x1_ref.shape[0]
    base = pl.program_id(0) * tm

    def row_copy(r, k):
        d = dest_ref[k * n_tok + base + r]
        return pltpu.make_async_copy(ys_ref.at[pl.ds(d, 1), :], ybuf.at[k, pl.ds(r, 1), :], sem)

    def issue(r, c):
        for k in range(TOPK_IN_GROUP):
            row_copy(r, k).start()
        return c

    def drain(r, c):
        for k in range(TOPK_IN_GROUP):
            row_copy(r, k).wait()
        return c

    lax.fori_loop(0, tm, issue, 0)
    lax.fori_loop(0, tm, drain, 0)
    rw = rw_ref[...]
    f = rw[:, 0:1] * ybuf[0] + rw[:, 1:2] * ybuf[1]
    y = _layer_norm_rows(alpha * x1_ref[...] + (1.0 + gt_ref[0]) * f)
    o_ref[...] = y * g2_ref[...] + b2_ref[...]


def _combine(dest, ys, x1, rw, gt2, g2, b2, seq, alpha):
    n, d = x1.shape
    tm = ROW_TM
    per_batch = seq // tm
    return pl.pallas_call(
        functools.partial(_combine_kernel, n_tok=n, alpha=alpha),
        grid_spec=pltpu.PrefetchScalarGridSpec(
            num_scalar_prefetch=1,
            grid=(n // tm,),
            in_specs=[pl.BlockSpec(memory_space=pl.ANY),
                      pl.BlockSpec((tm, d), lambda i, dest: (i, 0)),
                      pl.BlockSpec((tm, LANES), lambda i, dest: (i, 0)),
                      pl.BlockSpec((1, 1, d), lambda i, dest: (i // per_batch, 0, 0)),
                      pl.BlockSpec((1, d), lambda i, dest: (0, 0)),
                      pl.BlockSpec((1, d), lambda i, dest: (0, 0))],
            out_specs=pl.BlockSpec((tm, d), lambda i, dest: (i, 0)),
            scratch_shapes=[pltpu.VMEM((TOPK_IN_GROUP, tm, d), jnp.float32),
                            pltpu.SemaphoreType.DMA(())]),
        out_shape=jax.ShapeDtypeStruct((n, d), jnp.float32),
        compiler_params=_cparams(("arbitrary",)),
        name="combine",
    )(dest, ys, x1, rw, gt2, g2, b2)


def kernel(x, c, positions, w_cond, b_cond, w_in, b_glu, w_dw, b_dw, g_cn, b_cn, w_conv_out, b_conv_out, w_attn_out, w_mix_out, g_ln1, b_ln1, w_grp, b_grp, w_erouter, b_erouter, w1, w3, w2, g_ln2, b_ln2):
    nb, seq, d = x.shape
    depth = w_cond.shape[0]
    n = nb * seq
    ch = w_dw.shape[2]
    aw = N_HEADS * HEAD_DIM
    in_cols = w_in.shape[2]
    alpha = (2.0 * depth) ** 0.25
    bf16 = jnp.bfloat16
    assert 2 * ch + 3 * aw + 2 * d == in_cols and ch == PROJ_TN and aw == PROJ_TN and d == 2 * PROJ_TN
    assert seq % MOBA_BLOCK == 0 and seq // MOBA_BLOCK <= LANES and N_GROUPS + N_EXPERTS <= LANES
    t_a, t_b, t_q, t_k, t_v, t_gc, t_ga = 0, 1, 2, 3, 4, 5, 7
    heads_per_tile = PROJ_TN // HEAD_DIM
    nblk = seq // MOBA_BLOCK
    n_slots = n * TOPK_IN_GROUP + N_EXPERTS * EXP_TM
    assert n_slots // EXP_TM < LANES

    cos_t, sin_t = _rope_tables(positions)
    x2 = x.reshape(n, d)
    for l in range(depth):
        mod = _mod(c, w_cond[l], b_cond[l]).reshape(nb, 6, 1, d)
        sh1, sc1, gt1, sh2, sc2, gt2 = (mod[:, s] for s in range(6))

        bias = jnp.concatenate([b_glu[l], jnp.zeros((in_cols - 2 * ch,), jnp.float32)]).reshape(1, in_cols)
        z = _in_proj(x2, sc1, sh1, w_in[l].astype(bf16), bias, cos_t, sin_t, seq,
                     rope_tiles=(t_q, t_k + 1), sig_from=t_gc)
        km = _kmean(z, nb * nblk, t_k, aw).reshape(nb, nblk, aw)
        km = jnp.pad(km, ((0, 0), (0, LANES - nblk), (0, 0))).astype(bf16)
        qx = _gate(z, km, nb, seq, t_q)
        o_attn = _attention(qx, z, nb, seq, t_k * heads_per_tile, t_v * heads_per_tile)
        hc = _conv(z, w_dw[l], b_dw[l], g_cn[l], b_cn[l], nb, seq, t_a, t_b)

        w_r = jnp.concatenate([w_grp[l], w_erouter[l],
                               jnp.zeros((d, LANES - N_GROUPS - N_EXPERTS), jnp.float32)], axis=1)
        b_r = jnp.concatenate([b_grp[l], b_erouter[l],
                               jnp.zeros((LANES - N_GROUPS - N_EXPERTS,), jnp.float32)]).reshape(1, LANES)
        x1, u2, ri, rw = _mix(hc, o_attn, z, x2, w_conv_out[l].astype(bf16), b_conv_out[l].reshape(1, d),
                              w_attn_out[l].astype(bf16), w_mix_out[l].astype(bf16), gt1,
                              g_ln1[l].reshape(1, d), b_ln1[l].reshape(1, d), sc2, sh2, w_r, b_r,
                              seq, t_gc, t_ga, alpha)

        eid = ri[:, :TOPK_IN_GROUP].T.reshape(TOPK_IN_GROUP * n)
        dest, tiles = _meta(eid, EXP_TM)
        xs = _dispatch(dest, u2, n_slots)
        f_in = w1.shape[-1]
        ys = _experts(tiles, xs, w1[l].reshape(N_EXPERTS, d, f_in), w3[l].reshape(N_EXPERTS, d, f_in),
                      w2[l].reshape(N_EXPERTS, f_in, d))
        x2 = _combine(dest, ys, x1, rw, gt2, g_ln2[l].reshape(1, d), b_ln2[l].reshape(1, d), seq, alpha)
    return x2.reshape(nb, seq, d)
```

```python
import functools

import jax
import jax.numpy as jnp
from jax import lax
from jax.experimental import pallas as pl
from jax.experimental.pallas import tpu as pltpu

CONV_WIDTH = 31
N_HEADS = 8
HEAD_DIM = 128
ROPE_DIM = HEAD_DIM // 4
ROPE_THETA = 500000.0
MOBA_BLOCK = 256
MOBA_TOPK = 3
N_GROUPS = 4
EXPERTS_PER_GROUP = 8
N_EXPERTS = N_GROUPS * EXPERTS_PER_GROUP
TOPK_IN_GROUP = 2
LN_EPS = 1e-5
NEG_INF = -1e30

LANES = 128
SUBLANES = 8
VMEM_LIMIT = 56 * 1024 * 1024

PROJ_TM = 512
PROJ_TN = 1024
PROJ_CHUNK = 256
CONV_TT = 256
CONV_HALO = 32
ATTN_GROUP = 4
ATTN_HEADS = 2
MIX_TM = 256
EXP_TM = 256
ROW_TM = 256
META_BLK = 512


def _cparams(sem, vmem=VMEM_LIMIT):
    return pltpu.CompilerParams(dimension_semantics=sem, vmem_limit_bytes=vmem)


def _layer_norm_rows(x):
    mu = jnp.mean(x, axis=-1, keepdims=True)
    xc = x - mu
    var = jnp.mean(xc * xc, axis=-1, keepdims=True)
    return xc * lax.rsqrt(var + LN_EPS)


def _sigmoid(x):
    return 1.0 / (1.0 + jnp.exp(-x))


def _dot_nt(a, b):
    return lax.dot_general(a, b, (((1,), (1,)), ((), ())), preferred_element_type=jnp.float32)


def _mod_kernel(cb_ref, w_ref, b_ref, o_ref, s_ref):
    @pl.when(pl.program_id(0) == 0)
    def _():
        cb = cb_ref[...]
        s_ref[...] = cb * _sigmoid(cb)

    nb, d, _ = cb_ref.shape
    tn = w_ref.shape[1]
    for cc in range(tn // LANES):
        sl = slice(cc * LANES, (cc + 1) * LANES)
        wc = w_ref[:, sl]
        for b in range(nb):
            p = (wc * s_ref[b]).reshape(d // SUBLANES, SUBLANES, LANES)
            r = jnp.sum(jnp.sum(p, axis=0), axis=0, keepdims=True)
            o_ref[b:b + 1, sl] = r + b_ref[:, sl]


def _mod(c, w_cond, b_cond, tn=512):
    nb, d = c.shape
    n_out = w_cond.shape[1]
    cb = jnp.broadcast_to(c[:, :, None], (nb, d, LANES))
    return pl.pallas_call(
        _mod_kernel,
        grid=(n_out // tn,),
        in_specs=[pl.BlockSpec((nb, d, LANES), lambda j: (0, 0, 0)),
                  pl.BlockSpec((d, tn), lambda j: (0, j)),
                  pl.BlockSpec((1, tn), lambda j: (0, j))],
        out_specs=pl.BlockSpec((nb, tn), lambda j: (0, j)),
        out_shape=jax.ShapeDtypeStruct((nb, n_out), jnp.float32),
        scratch_shapes=[pltpu.VMEM((nb, d, LANES), jnp.float32)],
        compiler_params=_cparams(("arbitrary",)),
        name="mod",
    )(cb, w_cond, b_cond.reshape(1, n_out))


def _rope_tab_kernel(pos_ref, invf_ref, sign_ref, cos_ref, sin_ref):
    ang = pos_ref[...] * invf_ref[...]
    cos_ref[...] = jnp.cos(ang)
    sin_ref[...] = jnp.sin(ang) * sign_ref[...]


def _rope_tables(positions, tm=1024):
    n = positions.size
    half = ROPE_DIM // 2
    inv_freq = jnp.power(ROPE_THETA, -jnp.arange(half, dtype=jnp.float32) / half)
    invf = jnp.concatenate([inv_freq, inv_freq, jnp.zeros((LANES - ROPE_DIM,), jnp.float32)]).reshape(1, LANES)
    sign = jnp.concatenate([-jnp.ones((half,), jnp.float32),
                            jnp.ones((LANES - half,), jnp.float32)]).reshape(1, LANES)
    posb = jnp.broadcast_to(positions.astype(jnp.float32).reshape(n, 1), (n, LANES))
    vec = pl.BlockSpec((1, LANES), lambda i: (0, 0))
    tab = pl.BlockSpec((tm, LANES), lambda i: (i, 0))
    return pl.pallas_call(
        _rope_tab_kernel,
        grid=(n // tm,),
        in_specs=[tab, vec, vec],
        out_specs=[tab, tab],
        out_shape=[jax.ShapeDtypeStruct((n, LANES), jnp.float32)] * 2,
        compiler_params=_cparams(("parallel",)),
        name="rope_tab",
    )(posb, invf, sign)


def _in_proj_kernel(x_ref, sc_ref, sh_ref, w_ref, bias_ref, cos_ref, sin_ref, o_ref, u_ref,
                    *, rope_tiles, sig_from):
    j = pl.program_id(1)

    @pl.when(j == 0)
    def _():
        xn = _layer_norm_rows(x_ref[...])
        u_ref[...] = (xn * (1.0 + sc_ref[0]) + sh_ref[0]).astype(u_ref.dtype)

    is_rope = (j >= rope_tiles[0]) & (j < rope_tiles[1])

    def project(epilogue):
        for n in range(w_ref.shape[1] // PROJ_CHUNK):
            cs = slice(n * PROJ_CHUNK, (n + 1) * PROJ_CHUNK)
            acc = jnp.dot(u_ref[...], w_ref[:, cs], preferred_element_type=jnp.float32) + bias_ref[:, cs]
            o_ref[:, cs] = epilogue(acc).astype(o_ref.dtype)

    @pl.when(is_rope)
    def _():
        cosf = cos_ref[...]
        sinf = sin_ref[...]
        lane = lax.broadcasted_iota(jnp.int32, cosf.shape, 1)
        first = lane < (ROPE_DIM // 2)

        def rope(acc):
            heads = []
            for h in range(acc.shape[1] // HEAD_DIM):
                xh = acc[:, h * HEAD_DIM:(h + 1) * HEAD_DIM]
                partner = jnp.where(first, pltpu.roll(xh, HEAD_DIM - ROPE_DIM // 2, 1),
                                    pltpu.roll(xh, ROPE_DIM // 2, 1))
                heads.append(xh * cosf + partner * sinf)
            return jnp.concatenate(heads, axis=1)

        project(rope)

    @pl.when(j >= sig_from)
    def _():
        project(_sigmoid)

    @pl.when(jnp.logical_not(is_rope) & (j < sig_from))
    def _():
        project(lambda acc: acc)


def _in_proj(x2, sc1, sh1, w_bf, bias, cos_t, sin_t, seq, rope_tiles, sig_from):
    n, d = x2.shape
    cols = w_bf.shape[1]
    tm, tn = PROJ_TM, PROJ_TN
    per_batch = seq // tm
    kern = functools.partial(_in_proj_kernel, rope_tiles=rope_tiles, sig_from=sig_from)
    return pl.pallas_call(
        kern,
        grid=(n // tm, cols // tn),
        in_specs=[pl.BlockSpec((tm, d), lambda i, j: (i, 0)),
                  pl.BlockSpec((1, 1, d), lambda i, j: (i // per_batch, 0, 0)),
                  pl.BlockSpec((1, 1, d), lambda i, j: (i // per_batch, 0, 0)),
                  pl.BlockSpec((d, tn), lambda i, j: (0, j)),
                  pl.BlockSpec((1, tn), lambda i, j: (0, j)),
                  pl.BlockSpec((tm, LANES), lambda i, j: (i, 0)),
                  pl.BlockSpec((tm, LANES), lambda i, j: (i, 0))],
        out_specs=pl.BlockSpec((tm, tn), lambda i, j: (i, j)),
        out_shape=jax.ShapeDtypeStruct((n, cols), jnp.bfloat16),
        scratch_shapes=[pltpu.VMEM((tm, d), jnp.bfloat16)],
        compiler_params=_cparams(("parallel", "arbitrary")),
        name="in_proj",
    )(x2, sc1, sh1, w_bf, bias, cos_t, sin_t)


def _kmean_kernel(k_ref, o_ref):
    k = k_ref[...].astype(jnp.float32)
    o_ref[0] = (jnp.sum(k, axis=0, keepdims=True) * (1.0 / k.shape[0])).astype(o_ref.dtype)


def _kmean(z, n_blocks, k_tile, width):
    return pl.pallas_call(
        _kmean_kernel,
        grid=(n_blocks,),
        in_specs=[pl.BlockSpec((MOBA_BLOCK, width), lambda i: (i, k_tile))],
        out_specs=pl.BlockSpec((1, 1, width), lambda i: (i, 0, 0)),
        out_shape=jax.ShapeDtypeStruct((n_blocks, 1, width), jnp.bfloat16),
        compiler_params=_cparams(("parallel",)),
        name="kmean",
    )(z)


def _gate_kernel(q_ref, km_ref, qx_ref):
    i = pl.program_id(1)
    bs = q_ref.shape[0]
    nblk = km_ref.shape[1]
    sub = lax.broadcasted_iota(jnp.int32, (nblk, bs), 0)
    past = sub < i
    for h in range(N_HEADS):
        hs = slice(h * HEAD_DIM, (h + 1) * HEAD_DIM)
        q = q_ref[:, hs]
        g = jnp.where(past, _dot_nt(km_ref[0, :, hs], q), -jnp.inf)
        attend = sub == i
        for _ in range(MOBA_TOPK):
            m = jnp.max(g, axis=0, keepdims=True)
            idx = jnp.min(jnp.where(g == m, sub, nblk), axis=0, keepdims=True)
            hit = sub == idx
            attend = attend | (hit & past)
            g = jnp.where(hit, -jnp.inf, g)
        att = jnp.concatenate([jnp.where(attend, 1.0, 0.0), jnp.ones((LANES - nblk, bs), jnp.float32)], axis=0)
        bias = jnp.where(att.T > 0.5, 0.0, NEG_INF)
        qx_ref[:, 2 * h * HEAD_DIM:(2 * h + 1) * HEAD_DIM] = q
        qx_ref[:, (2 * h + 1) * HEAD_DIM:(2 * h + 2) * HEAD_DIM] = bias.astype(qx_ref.dtype)


def _gate(z, kmean, nb, seq, q_tile):
    n = z.shape[0]
    nq = seq // MOBA_BLOCK
    aw = N_HEADS * HEAD_DIM
    return pl.pallas_call(
        _gate_kernel,
        grid=(nb, nq),
        in_specs=[pl.BlockSpec((MOBA_BLOCK, aw), lambda b, i: (b * nq + i, q_tile)),
                  pl.BlockSpec((1, nq, aw), lambda b, i: (b, 0, 0))],
        out_specs=pl.BlockSpec((MOBA_BLOCK, 2 * aw), lambda b, i: (b * nq + i, 0)),
        out_shape=jax.ShapeDtypeStruct((n, 2 * aw), jnp.bfloat16),
        compiler_params=_cparams(("parallel", "arbitrary")),
        name="gate",
    )(z, kmean)


def _attn_kernel(qx_ref, k_ref, v_ref, o_ref, kx_ref, vx_ref, s_ref, p_ref):
    i = pl.program_id(2)
    bs = MOBA_BLOCK
    nblk = k_ref.shape[0] // bs
    wide = HEAD_DIM + LANES
    bf16 = jnp.bfloat16
    f32 = jnp.float32
    exp2_scale = (HEAD_DIM ** -0.5) * 1.4426950408889634
    grp = ATTN_GROUP * bs

    @pl.when(i == 0)
    def _():
        sub_k = lax.broadcasted_iota(jnp.int32, (LANES, bs), 0)
        for hh in range(ATTN_HEADS):
            hs = slice(hh * HEAD_DIM, (hh + 1) * HEAD_DIM)
            vx_ref[hh, :, :HEAD_DIM] = v_ref[:, hs]
            vx_ref[hh, :, HEAD_DIM:] = jnp.ones((nblk * bs, LANES), bf16)
            for jb in range(nblk):
                cs = slice(jb * bs, (jb + 1) * bs)
                kx_ref[hh, :HEAD_DIM, cs] = k_ref[cs, hs].astype(f32).T.astype(bf16)
                kx_ref[hh, HEAD_DIM:, cs] = jnp.where(sub_k == jb, 1.0, 0.0).astype(bf16)

    for c in range(nblk // ATTN_GROUP):
        @pl.when(i // ATTN_GROUP == c)
        def _(c=c):
            nk = (c + 1) * grp
            row = lax.broadcasted_iota(jnp.int32, (bs, LANES), 0) + i * bs
            col = lax.broadcasted_iota(jnp.int32, (bs, LANES), 1)
            for hh in range(ATTN_HEADS):
                s_ref[hh, :, :nk] = jnp.dot(qx_ref[:, hh * wide:(hh + 1) * wide], kx_ref[hh, :, 0:nk],
                                            preferred_element_type=f32)
            for hh in range(ATTN_HEADS):
                mx = None
                for t in range(nk // LANES):
                    ts = slice(t * LANES, (t + 1) * LANES)
                    st = s_ref[hh, :, ts]
                    if t * LANES >= c * grp:
                        st = jnp.where(col + t * LANES <= row, st, NEG_INF)
                        s_ref[hh, :, ts] = st
                    mx = st if mx is None else jnp.maximum(mx, st)
                mb = jnp.broadcast_to(jnp.max(mx, axis=-1, keepdims=True) * exp2_scale, (bs, LANES))
                for t in range(nk // LANES):
                    ts = slice(t * LANES, (t + 1) * LANES)
                    p_ref[hh, :, ts] = jnp.exp2(s_ref[hh, :, ts] * exp2_scale - mb).astype(bf16)
            for hh in range(ATTN_HEADS):
                acc = jnp.dot(p_ref[hh, :, :nk], vx_ref[hh, 0:nk, :], preferred_element_type=f32)
                o_ref[:, hh * HEAD_DIM:(hh + 1) * HEAD_DIM] = (
                    acc[:, :HEAD_DIM] / acc[:, HEAD_DIM:]).astype(o_ref.dtype)


def _attention(qx, z, nb, seq, k_col, v_col):
    n = z.shape[0]
    nq = seq // MOBA_BLOCK
    bs = MOBA_BLOCK
    wide = HEAD_DIM + LANES
    hps = ATTN_HEADS
    assert N_HEADS % hps == 0 and k_col % hps == 0 and v_col % hps == 0
    return pl.pallas_call(
        _attn_kernel,
        grid=(nb, N_HEADS // hps, nq),
        in_specs=[pl.BlockSpec((bs, hps * wide), lambda b, h, i: (b * nq + i, h)),
                  pl.BlockSpec((seq, hps * HEAD_DIM), lambda b, h, i: (b, k_col // hps + h)),
                  pl.BlockSpec((seq, hps * HEAD_DIM), lambda b, h, i: (b, v_col // hps + h))],
        out_specs=pl.BlockSpec((bs, hps * HEAD_DIM), lambda b, h, i: (b * nq + i, h)),
        out_shape=jax.ShapeDtypeStruct((n, N_HEADS * HEAD_DIM), jnp.bfloat16),
        scratch_shapes=[pltpu.VMEM((hps, wide, seq), jnp.bfloat16),
                        pltpu.VMEM((hps, seq, wide), jnp.bfloat16),
                        pltpu.VMEM((hps, bs, seq), jnp.float32),
                        pltpu.VMEM((hps, bs, seq), jnp.bfloat16)],
        compiler_params=_cparams(("parallel", "parallel", "arbitrary")),
        name="attn",
    )(qx, z, z)


def _conv_kernel(a_ref, b_ref, ha_ref, hb_ref, w_ref, bdw_ref, g_ref, bcn_ref, o_ref, hext_ref, grp_ref):
    i = pl.program_id(1)
    tt = a_ref.shape[0]
    halo = ha_ref.shape[0]
    ch = a_ref.shape[1]
    glu_h = ha_ref[...].astype(jnp.float32) * _sigmoid(hb_ref[...].astype(jnp.float32))
    hext_ref[0:halo, :] = jnp.where(i > 0, glu_h, 0.0)
    hext_ref[halo:halo + tt, :] = a_ref[...].astype(jnp.float32) * _sigmoid(b_ref[...].astype(jnp.float32))
    hext_ref[halo + tt:, :] = jnp.zeros((SUBLANES, ch), jnp.float32)

    first = halo - (CONV_WIDTH - 1)
    acc = jnp.zeros(o_ref.shape, jnp.float32) + bdw_ref[...]
    for s in range(SUBLANES):
        g = None
        for j in range(CONV_WIDTH):
            off = first + j
            if off % SUBLANES != s:
                continue
            term = w_ref[j:j + 1, :] * hext_ref[pl.ds(off - s, tt + SUBLANES), :]
            g = term if g is None else g + term
        if g is None:
            continue
        if s == 0:
            acc = acc + g[:tt]
        else:
            grp_ref[s] = g
            acc = acc + grp_ref[s, pl.ds(s, tt), :]
    y = _layer_norm_rows(acc) * g_ref[...] + bcn_ref[...]
    o_ref[...] = (y * _sigmoid(y)).astype(o_ref.dtype)


def _conv(z, w_dw, b_dw, g_cn, b_cn, nb, seq, a_tile, b_tile):
    n = z.shape[0]
    ch = w_dw.shape[1]
    tt, halo = CONV_TT, CONV_HALO
    nt = seq // tt
    r = tt // halo

    def cur(tile):
        return pl.BlockSpec((tt, ch), lambda b, i: (b * nt + i, tile))

    def prev(tile):
        return pl.BlockSpec((halo, ch), lambda b, i: (jnp.maximum((b * nt + i) * r - 1, 0), tile))

    vec = pl.BlockSpec((1, ch), lambda b, i: (0, 0))
    return pl.pallas_call(
        _conv_kernel,
        grid=(nb, nt),
        in_specs=[cur(a_tile), cur(b_tile), prev(a_tile), prev(b_tile),
                  pl.BlockSpec((CONV_WIDTH, ch), lambda b, i: (0, 0)), vec, vec, vec],
        out_specs=pl.BlockSpec((tt, ch), lambda b, i: (b * nt + i, 0)),
        out_shape=jax.ShapeDtypeStruct((n, ch), jnp.bfloat16),
        scratch_shapes=[pltpu.VMEM((halo + tt + SUBLANES, ch), jnp.float32),
                        pltpu.VMEM((SUBLANES, tt + SUBLANES, ch), jnp.float32)],
        compiler_params=_cparams(("parallel", "parallel")),
        name="conv",
    )(z, z, z, z, w_dw, b_dw.reshape(1, ch), g_cn.reshape(1, ch), b_cn.reshape(1, ch))


def _mix_kernel(hc_ref, o_ref, sgc0_ref, sgc1_ref, sga0_ref, sga1_ref, x_ref, wco_ref, bco_ref, wao_ref, wmx_ref,
                gt_ref, g1_ref, b1_ref, sc_ref, sh_ref, wr_ref, br_ref,
                x1_ref, u2_ref, ri_ref, rw_ref, *, alpha):
    f32 = jnp.float32
    y_conv = jnp.dot(hc_ref[...], wco_ref[...], preferred_element_type=f32) + bco_ref[...]
    y_attn = jnp.dot(o_ref[...], wao_ref[...], preferred_element_type=f32)
    sgc = jnp.concatenate([sgc0_ref[...], sgc1_ref[...]], axis=1).astype(f32)
    sga = jnp.concatenate([sga0_ref[...], sga1_ref[...]], axis=1).astype(f32)
    merged = sgc * y_conv + sga * y_attn
    t_out = jnp.dot(merged.astype(jnp.bfloat16), wmx_ref[...], preferred_element_type=f32)
    x1 = _layer_norm_rows(alpha * x_ref[...] + (1.0 + gt_ref[0]) * t_out) * g1_ref[...] + b1_ref[...]
    x1_ref[...] = x1
    u2 = _layer_norm_rows(x1) * (1.0 + sc_ref[0]) + sh_ref[0]
    u2_ref[...] = u2

    hi = u2.astype(jnp.bfloat16)
    lo = (u2 - hi.astype(f32)).astype(jnp.bfloat16)
    logits = jnp.dot(jnp.concatenate([hi, lo, hi], axis=1), wr_ref[...], preferred_element_type=f32) + br_ref[...]
    lane = lax.broadcasted_iota(jnp.int32, logits.shape, 1)
    is_grp = lane < N_GROUPS
    gl = jnp.where(is_grp, logits, -jnp.inf)
    gmax = jnp.max(gl, axis=-1, keepdims=True)
    gidx = jnp.min(jnp.where(gl == gmax, lane, LANES), axis=-1, keepdims=True)
    p_top = 1.0 / jnp.sum(jnp.where(is_grp, jnp.exp(logits - gmax), 0.0), axis=-1, keepdims=True)
    lo = N_GROUPS + gidx * EXPERTS_PER_GROUP
    el = jnp.where((lane >= lo) & (lane < lo + EXPERTS_PER_GROUP), logits, -jnp.inf)
    v1 = jnp.max(el, axis=-1, keepdims=True)
    i1 = jnp.min(jnp.where(el == v1, lane, LANES), axis=-1, keepdims=True)
    el = jnp.where(lane == i1, -jnp.inf, el)
    v2 = jnp.max(el, axis=-1, keepdims=True)
    i2 = jnp.min(jnp.where(el == v2, lane, LANES), axis=-1, keepdims=True)
    e2 = jnp.exp(v2 - v1)
    w1 = p_top / (1.0 + e2)
    w2 = p_top * e2 / (1.0 + e2)
    ri_ref[...] = jnp.where(lane == 0, i1 - N_GROUPS, jnp.where(lane == 1, i2 - N_GROUPS, 0))
    rw_ref[...] = jnp.where(lane == 0, w1, jnp.where(lane == 1, w2, 0.0))


def _mix(hc, o_attn, z, x2, wco, bco, wao, wmx, gt1, g1, b1, sc2, sh2, w_r, b_r, seq, gc_tile, ga_tile, alpha):
    n, d = x2.shape
    ch = hc.shape[1]
    tm = MIX_TM
    per_batch = seq // tm

    def rows(width):
        return pl.BlockSpec((tm, width), lambda i: (i, 0))

    def const(shape):
        return pl.BlockSpec(shape, lambda i: (0,) * len(shape), pipeline_mode=pl.Buffered(1))

    def per_b():
        return pl.BlockSpec((1, 1, d), lambda i: (i // per_batch, 0, 0))

    return pl.pallas_call(
        functools.partial(_mix_kernel, alpha=alpha),
        grid=(n // tm,),
        in_specs=[rows(ch), rows(ch),
                  pl.BlockSpec((tm, ch), lambda i: (i, gc_tile)),
                  pl.BlockSpec((tm, ch), lambda i: (i, gc_tile + 1)),
                  pl.BlockSpec((tm, ch), lambda i: (i, ga_tile)),
                  pl.BlockSpec((tm, ch), lambda i: (i, ga_tile + 1)),
                  rows(d),
                  const((ch, d)), const((1, d)), const((ch, d)), const((d, d)),
                  per_b(), const((1, d)), const((1, d)), per_b(), per_b(),
                  const((3 * d, LANES)), const((1, LANES))],
        out_specs=[rows(d), rows(d), rows(LANES), rows(LANES)],
        out_shape=[jax.ShapeDtypeStruct((n, d), jnp.float32),
                   jax.ShapeDtypeStruct((n, d), jnp.float32),
                   jax.ShapeDtypeStruct((n, LANES), jnp.int32),
                   jax.ShapeDtypeStruct((n, LANES), jnp.float32)],
        compiler_params=_cparams(("parallel",)),
        name="mix",
    )(hc, o_attn, z, z, z, z, x2, wco, bco, wao, wmx, gt1, g1, b1, sc2, sh2, w_r, b_r)


def _meta_kernel(eid_ref, dest_ref, tile_ref, cnt_ref, base_ref, *, tile_rows):
    ph = pl.program_id(0)
    r = pl.program_id(1)
    f32 = jnp.float32
    blk = eid_ref.shape[2]
    e = eid_ref[0]
    sub = lax.broadcasted_iota(jnp.int32, (N_EXPERTS, blk), 0)
    oh = (sub == e).astype(f32)
    blk_cnt = jnp.broadcast_to(jnp.sum(oh, axis=1, keepdims=True), (N_EXPERTS, LANES))

    @pl.when((ph == 0) & (r == 0))
    def _():
        cnt_ref[...] = jnp.zeros_like(cnt_ref)

    @pl.when(ph == 0)
    def _():
        cnt_ref[...] += blk_cnt

    @pl.when((ph == 1) & (r == 0))
    def _():
        padded = jnp.ceil(cnt_ref[...] * (1.0 / tile_rows)) * tile_rows
        er = lax.broadcasted_iota(jnp.int32, (N_EXPERTS, N_EXPERTS), 0)
        ec = lax.broadcasted_iota(jnp.int32, (N_EXPERTS, N_EXPERTS), 1)
        starts = jnp.dot((ec < er).astype(f32), padded, preferred_element_type=f32,
                         precision=lax.Precision.HIGHEST)
        base_ref[...] = starts
        ends = starts + padded
        lane = lax.broadcasted_iota(jnp.int32, (N_EXPERTS, LANES), 1)
        tile_start = (lane * tile_rows).astype(f32)
        owner = jnp.sum((ends <= tile_start).astype(f32), axis=0, keepdims=True)
        total_tiles = jnp.max(ends, axis=0, keepdims=True) * (1.0 / tile_rows)
        lane1 = lax.broadcasted_iota(jnp.int32, (1, LANES), 1)
        tile_ref[...] = jnp.where(lane1 == LANES - 1, total_tiles, owner).astype(jnp.int32)

    @pl.when(ph == 1)
    def _():
        ar = lax.broadcasted_iota(jnp.int32, (blk, blk), 0)
        ac = lax.broadcasted_iota(jnp.int32, (blk, blk), 1)
        before = (ar < ac).astype(jnp.bfloat16)
        cum = jnp.dot(oh.astype(jnp.bfloat16), before, preferred_element_type=f32)
        slot = jnp.sum(oh * (base_ref[:, 0:1] + cum), axis=0, keepdims=True)
        dest_ref[0] = slot.astype(jnp.int32)
        base_ref[...] += blk_cnt


def _meta(eid_flat, tile_rows):
    na = eid_flat.shape[0]
    nblk = na // META_BLK
    eid3 = eid_flat.reshape(nblk, 1, META_BLK)
    dest, tiles = pl.pallas_call(
        functools.partial(_meta_kernel, tile_rows=tile_rows),
        grid=(2, nblk),
        in_specs=[pl.BlockSpec((1, 1, META_BLK), lambda ph, r: (r, 0, 0))],
        out_specs=[pl.BlockSpec((1, 1, META_BLK), lambda ph, r: (ph * r, 0, 0)),
                   pl.BlockSpec((1, LANES), lambda ph, r: (0, 0))],
        out_shape=[jax.ShapeDtypeStruct((nblk, 1, META_BLK), jnp.int32),
                   jax.ShapeDtypeStruct((1, LANES), jnp.int32)],
        scratch_shapes=[pltpu.VMEM((N_EXPERTS, LANES), jnp.float32),
                        pltpu.VMEM((N_EXPERTS, LANES), jnp.float32)],
        compiler_params=_cparams(("arbitrary", "arbitrary")),
        name="meta",
    )(eid3)
    return dest.reshape(na), tiles.reshape(LANES)


def _invert_kernel(dest_ref, inv_ref, *, n_tok):
    def clear(s, c):
        inv_ref[s] = 0
        return c

    lax.fori_loop(0, inv_ref.shape[0], clear, 0, unroll=8)
    for k in range(TOPK_IN_GROUP):
        def put(tok, c, k=k):
            inv_ref[dest_ref[k * n_tok + tok]] = tok
            return c

        lax.fori_loop(0, n_tok, put, 0, unroll=8)


def _invert(dest, n_tok, n_inv):
    return pl.pallas_call(
        functools.partial(_invert_kernel, n_tok=n_tok),
        in_specs=[pl.BlockSpec(memory_space=pltpu.SMEM)],
        out_specs=pl.BlockSpec(memory_space=pltpu.SMEM),
        out_shape=jax.ShapeDtypeStruct((n_inv,), jnp.int32),
        name="invert",
    )(dest)


def _expert_kernel(tile_ref, inv_ref, u_ref, w1_ref, w3_ref, w2_ref, y_ref,
                   xbuf0, xbuf1, wf1, wf3, wf2, w1b, w3b, w2b, wslot_ref, xsem, wsem):
    t = pl.program_id(0)
    tm = xbuf0.shape[0]
    bf16 = jnp.bfloat16
    n_used = tile_ref[LANES - 1]
    e = tile_ref[t]
    changed = (t == 0) | (e != tile_ref[jnp.maximum(t - 1, 0)])
    active = t < n_used

    xbufs = (xbuf0, xbuf1)

    def row_copy(tok, buf, r):
        return pltpu.make_async_copy(u_ref.at[pl.ds(tok, 1), :], xbufs[buf].at[pl.ds(r, 1), :], xsem.at[buf])

    def gather(tile, buf):
        for r in range(tm):
            row_copy(inv_ref[tile * tm + r], buf, r).start()

    def gather_wait(buf):
        def one(r, c):
            row_copy(0, buf, r).wait()
            return c

        lax.fori_loop(0, tm, one, 0, unroll=8)

    def weight_copies(ex, ws):
        return (pltpu.make_async_copy(w1_ref.at[ex], wf1.at[ws], wsem.at[ws, 0]),
                pltpu.make_async_copy(w3_ref.at[ex], wf3.at[ws], wsem.at[ws, 1]),
                pltpu.make_async_copy(w2_ref.at[ex], wf2.at[ws], wsem.at[ws, 2]))

    @pl.when(t == 0)
    def _():
        gather(0, 0)
        for cp in weight_copies(e, 0):
            cp.start()
        wslot_ref[0] = 0

    @pl.when(active & changed)
    def _():
        ws = wslot_ref[0]
        for cp in weight_copies(e, ws):
            cp.wait()
        nxt = lax.while_loop(lambda t2: (t2 < n_used) & (tile_ref[t2] == e), lambda t2: t2 + 1, t + 1)

        @pl.when(nxt < n_used)
        def _():
            for cp in weight_copies(tile_ref[nxt], 1 - ws):
                cp.start()

        w1b[...] = wf1[ws].astype(bf16)
        w3b[...] = wf3[ws].astype(bf16)
        w2b[...] = wf2[ws].astype(bf16)
        wslot_ref[0] = 1 - ws

    for buf in range(2):
        @pl.when(active & (t % 2 == buf))
        def _(buf=buf):
            gather_wait(buf)
            gather(t + 1, 1 - buf)
            xb = xbufs[buf][...].astype(bf16)
            h1 = jnp.dot(xb, w1b[...], preferred_element_type=jnp.float32)
            h3 = jnp.dot(xb, w3b[...], preferred_element_type=jnp.float32)
            hh = (h1 * _sigmoid(h1) * h3).astype(bf16)
            y_ref[...] = jnp.dot(hh, w2b[...], preferred_element_type=jnp.float32)

        @pl.when((t == n_used) & (t % 2 == buf))
        def _(buf=buf):
            gather_wait(buf)

    @pl.when(jnp.logical_not(active))
    def _():
        y_ref[...] = jnp.zeros_like(y_ref)


def _experts(tiles, inv, u2, w1, w3, w2, n_tiles):
    d = u2.shape[1]
    f = w1.shape[2]
    tm = EXP_TM
    any_spec = pl.BlockSpec(memory_space=pl.ANY)
    return pl.pallas_call(
        _expert_kernel,
        grid_spec=pltpu.PrefetchScalarGridSpec(
            num_scalar_prefetch=2,
            grid=(n_tiles,),
            in_specs=[any_spec, any_spec, any_spec, any_spec],
            out_specs=pl.BlockSpec((tm, d), lambda t, tiles, inv: (t, 0)),
            scratch_shapes=[pltpu.VMEM((tm, d), jnp.float32),
                            pltpu.VMEM((tm, d), jnp.float32),
                            pltpu.VMEM((2, d, f), jnp.float32),
                            pltpu.VMEM((2, d, f), jnp.float32),
                            pltpu.VMEM((2, f, d), jnp.float32),
                            pltpu.VMEM((d, f), jnp.bfloat16),
                            pltpu.VMEM((d, f), jnp.bfloat16),
                            pltpu.VMEM((f, d), jnp.bfloat16),
                            pltpu.SMEM((1,), jnp.int32),
                            pltpu.SemaphoreType.DMA((2,)),
                            pltpu.SemaphoreType.DMA((2, 3))]),
        out_shape=jax.ShapeDtypeStruct((n_tiles * tm, d), jnp.float32),
        compiler_params=_cparams(("arbitrary",)),
        name="experts",
    )(tiles, inv, u2, w1, w3, w2)


def _combine_kernel(dest_ref, ys_ref, x1_ref, rw_ref, gt_ref, g2_ref, b2_ref, o_ref, ybuf, sem,
                    *, n_tok, alpha):
    tm = x1_ref.shape[0]
    base = pl.program_id(0) * tm

    def row_copy(r, k):
        d = dest_ref[k * n_tok + base + r]
        return pltpu.make_async_copy(ys_ref.at[pl.ds(d, 1), :], ybuf.at[k, pl.ds(r, 1), :], sem)

    def issue(r, c):
        for k in range(TOPK_IN_GROUP):
            row_copy(r, k).start()
        return c

    def drain(r, c):
        for k in range(TOPK_IN_GROUP):
            row_copy(r, k).wait()
        return c

    lax.fori_loop(0, tm, issue, 0, unroll=8)
    lax.fori_loop(0, tm, drain, 0, unroll=8)
    rw = rw_ref[...]
    f = rw[:, 0:1] * ybuf[0] + rw[:, 1:2] * ybuf[1]
    y = _layer_norm_rows(alpha * x1_ref[...] + (1.0 + gt_ref[0]) * f)
    o_ref[...] = y * g2_ref[...] + b2_ref[...]


def _combine(dest, ys, x1, rw, gt2, g2, b2, seq, alpha):
    n, d = x1.shape
    tm = ROW_TM
    per_batch = seq // tm
    return pl.pallas_call(
        functools.partial(_combine_kernel, n_tok=n, alpha=alpha),
        grid_spec=pltpu.PrefetchScalarGridSpec(
            num_scalar_prefetch=1,
            grid=(n // tm,),
            in_specs=[pl.BlockSpec(memory_space=pl.ANY),
                      pl.BlockSpec((tm, d), lambda i, dest: (i, 0)),
                      pl.BlockSpec((tm, LANES), lambda i, dest: (i, 0)),
                      pl.BlockSpec((1, 1, d), lambda i, dest: (i // per_batch, 0, 0)),
                      pl.BlockSpec((1, d), lambda i, dest: (0, 0)),
                      pl.BlockSpec((1, d), lambda i, dest: (0, 0))],
            out_specs=pl.BlockSpec((tm, d), lambda i, dest: (i, 0)),
            scratch_shapes=[pltpu.VMEM((TOPK_IN_GROUP, tm, d), jnp.float32),
                            pltpu.SemaphoreType.DMA(())]),
        out_shape=jax.ShapeDtypeStruct((n, d), jnp.float32),
        compiler_params=_cparams(("arbitrary",)),
        name="combine",
    )(dest, ys, x1, rw, gt2, g2, b2)


def kernel(x, c, positions, w_cond, b_cond, w_in, b_glu, w_dw, b_dw, g_cn, b_cn, w_conv_out, b_conv_out, w_attn_out, w_mix_out, g_ln1, b_ln1, w_grp, b_grp, w_erouter, b_erouter, w1, w3, w2, g_ln2, b_ln2):
    nb, seq, d = x.shape
    depth = w_cond.shape[0]
    n = nb * seq
    ch = w_dw.shape[2]
    aw = N_HEADS * HEAD_DIM
    in_cols = w_in.shape[2]
    alpha = (2.0 * depth) ** 0.25
    bf16 = jnp.bfloat16
    assert 2 * ch + 3 * aw + 2 * d == in_cols and ch == PROJ_TN and aw == PROJ_TN and d == 2 * PROJ_TN
    assert seq % MOBA_BLOCK == 0 and seq // MOBA_BLOCK <= LANES and N_GROUPS + N_EXPERTS <= LANES
    t_a, t_b, t_q, t_k, t_v, t_gc, t_ga = 0, 1, 2, 3, 4, 5, 7
    heads_per_tile = PROJ_TN // HEAD_DIM
    nblk = seq // MOBA_BLOCK
    n_tiles = (n * TOPK_IN_GROUP) // EXP_TM + N_EXPERTS + 1
    assert (n * TOPK_IN_GROUP) % EXP_TM == 0 and n_tiles < LANES

    cos_t, sin_t = _rope_tables(positions)
    x2 = x.reshape(n, d)
    for l in range(depth):
        mod = _mod(c, w_cond[l], b_cond[l]).reshape(nb, 6, 1, d)
        sh1, sc1, gt1, sh2, sc2, gt2 = (mod[:, s] for s in range(6))

        bias = jnp.concatenate([b_glu[l], jnp.zeros((in_cols - 2 * ch,), jnp.float32)]).reshape(1, in_cols)
        z = _in_proj(x2, sc1, sh1, w_in[l].astype(bf16), bias, cos_t, sin_t, seq,
                     rope_tiles=(t_q, t_k + 1), sig_from=t_gc)
        km = _kmean(z, nb * nblk, t_k, aw).reshape(nb, nblk, aw)
        qx = _gate(z, km, nb, seq, t_q)
        o_attn = _attention(qx, z, nb, seq, t_k * heads_per_tile, t_v * heads_per_tile)
        hc = _conv(z, w_dw[l], b_dw[l], g_cn[l], b_cn[l], nb, seq, t_a, t_b)

        w_r = jnp.concatenate([w_grp[l], w_erouter[l],
                               jnp.zeros((d, LANES - N_GROUPS - N_EXPERTS), jnp.float32)], axis=1)
        b_r = jnp.concatenate([b_grp[l], b_erouter[l],
                               jnp.zeros((LANES - N_GROUPS - N_EXPERTS,), jnp.float32)]).reshape(1, LANES)
        w_r_hi = w_r.astype(bf16)
        w_r_lo = (w_r - w_r_hi.astype(jnp.float32)).astype(bf16)
        w_r = jnp.concatenate([w_r_hi, w_r_hi, w_r_lo], axis=0)
        x1, u2, ri, rw = _mix(hc, o_attn, z, x2, w_conv_out[l].astype(bf16), b_conv_out[l].reshape(1, d),
                              w_attn_out[l].astype(bf16), w_mix_out[l].astype(bf16), gt1,
                              g_ln1[l].reshape(1, d), b_ln1[l].reshape(1, d), sc2, sh2, w_r, b_r,
                              seq, t_gc, t_ga, alpha)

        eid = ri[:, :TOPK_IN_GROUP].T.reshape(TOPK_IN_GROUP * n)
        dest, tiles = _meta(eid, EXP_TM)
        inv = _invert(dest, n, (n_tiles + 1) * EXP_TM)
        f_in = w1.shape[-1]
        ys = _experts(tiles, inv, u2, w1[l].reshape(N_EXPERTS, d, f_in), w3[l].reshape(N_EXPERTS, d, f_in),
                      w2[l].reshape(N_EXPERTS, f_in, d), n_tiles)
        x2 = _combine(dest, ys, x1, rw, gt2, g_ln2[l].reshape(1, d), b_ln2[l].reshape(1, d), seq, alpha)
    return x2.reshape(nb, seq, d)
```

```python
import functools

import jax
import jax.numpy as jnp
from jax import lax
from jax.experimental import pallas as pl
from jax.experimental.pallas import tpu as pltpu

CONV_WIDTH = 31
N_HEADS = 8
HEAD_DIM = 128
ROPE_DIM = HEAD_DIM // 4
ROPE_THETA = 500000.0
MOBA_BLOCK = 256
MOBA_TOPK = 3
N_GROUPS = 4
EXPERTS_PER_GROUP = 8
N_EXPERTS = N_GROUPS * EXPERTS_PER_GROUP
TOPK_IN_GROUP = 2
LN_EPS = 1e-5
NEG_INF = -1e30

LANES = 128
SUBLANES = 8
VMEM_LIMIT = 56 * 1024 * 1024

PROJ_TM = 1024
PROJ_TN = 1024
PROJ_CHUNK = 256
CONV_TT = 256
CONV_HALO = 32
ATTN_GROUP = 4
ATTN_HEADS = 2
ATTN_STRIP = 128
MIX_TM = 256
EXP_TM = 256
ROW_TM = 256
META_BLK = 512


def _cparams(sem, vmem=VMEM_LIMIT):
    return pltpu.CompilerParams(dimension_semantics=sem, vmem_limit_bytes=vmem)


def _layer_norm_rows(x):
    mu = jnp.mean(x, axis=-1, keepdims=True)
    xc = x - mu
    var = jnp.mean(xc * xc, axis=-1, keepdims=True)
    return xc * lax.rsqrt(var + LN_EPS)


def _sigmoid(x):
    return 1.0 / (1.0 + jnp.exp(-x))


def _store_slabs(ref, val):
    m, w = val.shape
    c = w // LANES
    for k in range(c):
        ref[pl.ds(k, m, stride=c), :] = val[:, k * LANES:(k + 1) * LANES]


def _load_slabs(ref, m):
    c = ref.shape[0] // m
    return jnp.concatenate([ref[pl.ds(k, m, stride=c), :] for k in range(c)], axis=1)


def _dot_nt(a, b):
    return lax.dot_general(a, b, (((1,), (1,)), ((), ())), preferred_element_type=jnp.float32)


def _mod_kernel(cb_ref, w_ref, b_ref, o_ref, s_ref):
    @pl.when(pl.program_id(0) == 0)
    def _():
        cb = cb_ref[...]
        s_ref[...] = cb * _sigmoid(cb)

    nb, d, _ = cb_ref.shape
    tn = w_ref.shape[1]
    for cc in range(tn // LANES):
        sl = slice(cc * LANES, (cc + 1) * LANES)
        wc = w_ref[:, sl]
        for b in range(nb):
            p = (wc * s_ref[b]).reshape(d // SUBLANES, SUBLANES, LANES)
            r = jnp.sum(jnp.sum(p, axis=0), axis=0, keepdims=True)
            o_ref[b:b + 1, sl] = r + b_ref[:, sl]


def _mod(c, w_cond, b_cond, tn=512):
    nb, d = c.shape
    n_out = w_cond.shape[1]
    cb = jnp.broadcast_to(c[:, :, None], (nb, d, LANES))
    return pl.pallas_call(
        _mod_kernel,
        grid=(n_out // tn,),
        in_specs=[pl.BlockSpec((nb, d, LANES), lambda j: (0, 0, 0)),
                  pl.BlockSpec((d, tn), lambda j: (0, j)),
                  pl.BlockSpec((1, tn), lambda j: (0, j))],
        out_specs=pl.BlockSpec((nb, tn), lambda j: (0, j)),
        out_shape=jax.ShapeDtypeStruct((nb, n_out), jnp.float32),
        scratch_shapes=[pltpu.VMEM((nb, d, LANES), jnp.float32)],
        compiler_params=_cparams(("arbitrary",)),
        name="mod",
    )(cb, w_cond, b_cond.reshape(1, n_out))


def _rope_tab_kernel(pos_ref, invf_ref, sign_ref, cos_ref, sin_ref):
    ang = pos_ref[...] * invf_ref[...]
    cos_ref[...] = jnp.cos(ang)
    sin_ref[...] = jnp.sin(ang) * sign_ref[...]


def _rope_tables(positions, tm=1024):
    n = positions.size
    half = ROPE_DIM // 2
    inv_freq = jnp.power(ROPE_THETA, -jnp.arange(half, dtype=jnp.float32) / half)
    invf = jnp.concatenate([inv_freq, inv_freq, jnp.zeros((LANES - ROPE_DIM,), jnp.float32)]).reshape(1, LANES)
    sign = jnp.concatenate([-jnp.ones((half,), jnp.float32),
                            jnp.ones((LANES - half,), jnp.float32)]).reshape(1, LANES)
    posb = jnp.broadcast_to(positions.astype(jnp.float32).reshape(n, 1), (n, LANES))
    vec = pl.BlockSpec((1, LANES), lambda i: (0, 0))
    tab = pl.BlockSpec((tm, LANES), lambda i: (i, 0))
    return pl.pallas_call(
        _rope_tab_kernel,
        grid=(n // tm,),
        in_specs=[tab, vec, vec],
        out_specs=[tab, tab],
        out_shape=[jax.ShapeDtypeStruct((n, LANES), jnp.float32)] * 2,
        compiler_params=_cparams(("parallel",)),
        name="rope_tab",
    )(posb, invf, sign)


def _in_proj_kernel(x_ref, sc_ref, sh_ref, w_ref, bias_ref, cos_ref, sin_ref, o_ref, u_ref,
                    *, rope_tiles, sig_from):
    j = pl.program_id(1)

    @pl.when(j == 0)
    def _():
        xn = _layer_norm_rows(x_ref[...])
        u_ref[...] = (xn * (1.0 + sc_ref[0]) + sh_ref[0]).astype(u_ref.dtype)

    is_rope = (j >= rope_tiles[0]) & (j < rope_tiles[1])

    def project(epilogue):
        for n in range(w_ref.shape[1] // PROJ_CHUNK):
            cs = slice(n * PROJ_CHUNK, (n + 1) * PROJ_CHUNK)
            acc = jnp.dot(u_ref[...], w_ref[:, cs], preferred_element_type=jnp.float32) + bias_ref[:, cs]
            o_ref[:, cs] = epilogue(acc).astype(o_ref.dtype)

    @pl.when(is_rope)
    def _():
        cosf = cos_ref[...]
        sinf = sin_ref[...]
        lane = lax.broadcasted_iota(jnp.int32, cosf.shape, 1)
        first = lane < (ROPE_DIM // 2)

        def rope(acc):
            heads = []
            for h in range(acc.shape[1] // HEAD_DIM):
                xh = acc[:, h * HEAD_DIM:(h + 1) * HEAD_DIM]
                partner = jnp.where(first, pltpu.roll(xh, HEAD_DIM - ROPE_DIM // 2, 1),
                                    pltpu.roll(xh, ROPE_DIM // 2, 1))
                heads.append(xh * cosf + partner * sinf)
            return jnp.concatenate(heads, axis=1)

        project(rope)

    @pl.when(j >= sig_from)
    def _():
        project(_sigmoid)

    @pl.when(jnp.logical_not(is_rope) & (j < sig_from))
    def _():
        project(lambda acc: acc)


def _in_proj(x2, sc1, sh1, w_bf, bias, cos_t, sin_t, seq, rope_tiles, sig_from):
    n, d = x2.shape
    cols = w_bf.shape[1]
    tm, tn = PROJ_TM, PROJ_TN
    per_batch = seq // tm
    kern = functools.partial(_in_proj_kernel, rope_tiles=rope_tiles, sig_from=sig_from)
    return pl.pallas_call(
        kern,
        grid=(n // tm, cols // tn),
        in_specs=[pl.BlockSpec((tm, d), lambda i, j: (i, 0)),
                  pl.BlockSpec((1, 1, d), lambda i, j: (i // per_batch, 0, 0)),
                  pl.BlockSpec((1, 1, d), lambda i, j: (i // per_batch, 0, 0)),
                  pl.BlockSpec((d, tn), lambda i, j: (0, j)),
                  pl.BlockSpec((1, tn), lambda i, j: (0, j)),
                  pl.BlockSpec((tm, LANES), lambda i, j: (i, 0)),
                  pl.BlockSpec((tm, LANES), lambda i, j: (i, 0))],
        out_specs=pl.BlockSpec((tm, tn), lambda i, j: (i, j)),
        out_shape=jax.ShapeDtypeStruct((n, cols), jnp.bfloat16),
        scratch_shapes=[pltpu.VMEM((tm, d), jnp.bfloat16)],
        compiler_params=_cparams(("parallel", "arbitrary")),
        name="in_proj",
    )(x2, sc1, sh1, w_bf, bias, cos_t, sin_t)


def _kmean_kernel(k_ref, o_ref):
    k = k_ref[...].astype(jnp.float32)
    o_ref[0] = (jnp.sum(k, axis=0, keepdims=True) * (1.0 / k.shape[0])).astype(o_ref.dtype)


def _kmean(z, n_blocks, k_tile, width):
    return pl.pallas_call(
        _kmean_kernel,
        grid=(n_blocks,),
        in_specs=[pl.BlockSpec((MOBA_BLOCK, width), lambda i: (i, k_tile))],
        out_specs=pl.BlockSpec((1, 1, width), lambda i: (i, 0, 0)),
        out_shape=jax.ShapeDtypeStruct((n_blocks, 1, width), jnp.bfloat16),
        compiler_params=_cparams(("parallel",)),
        name="kmean",
    )(z)


def _gate_kernel(q_ref, km_ref, qx_ref):
    i = pl.program_id(1)
    bs = q_ref.shape[0]
    nblk = km_ref.shape[1]
    sub = lax.broadcasted_iota(jnp.int32, (nblk, bs), 0)
    past = sub < i
    for h in range(N_HEADS):
        hs = slice(h * HEAD_DIM, (h + 1) * HEAD_DIM)
        q = q_ref[:, hs]
        g = jnp.where(past, _dot_nt(km_ref[0, :, hs], q), -jnp.inf)
        attend = sub == i
        for _ in range(MOBA_TOPK):
            m = jnp.max(g, axis=0, keepdims=True)
            idx = jnp.min(jnp.where(g == m, sub, nblk), axis=0, keepdims=True)
            hit = sub == idx
            attend = attend | (hit & past)
            g = jnp.where(hit, -jnp.inf, g)
        att = jnp.concatenate([jnp.where(attend, 1.0, 0.0), jnp.ones((LANES - nblk, bs), jnp.float32)], axis=0)
        bias = jnp.where(att.T > 0.5, 0.0, NEG_INF)
        qx_ref[:, 2 * h * HEAD_DIM:(2 * h + 1) * HEAD_DIM] = q
        qx_ref[:, (2 * h + 1) * HEAD_DIM:(2 * h + 2) * HEAD_DIM] = bias.astype(qx_ref.dtype)


def _gate(z, kmean, nb, seq, q_tile):
    n = z.shape[0]
    nq = seq // MOBA_BLOCK
    aw = N_HEADS * HEAD_DIM
    return pl.pallas_call(
        _gate_kernel,
        grid=(nb, nq),
        in_specs=[pl.BlockSpec((MOBA_BLOCK, aw), lambda b, i: (b * nq + i, q_tile)),
                  pl.BlockSpec((1, nq, aw), lambda b, i: (b, 0, 0))],
        out_specs=pl.BlockSpec((MOBA_BLOCK, 2 * aw), lambda b, i: (b * nq + i, 0)),
        out_shape=jax.ShapeDtypeStruct((n, 2 * aw), jnp.bfloat16),
        compiler_params=_cparams(("parallel", "arbitrary")),
        name="gate",
    )(z, kmean)


def _attn_kernel(qx_ref, k_ref, v_ref, o_ref, kx_ref, vx_ref, s_ref, p_ref):
    i = pl.program_id(2)
    bs = MOBA_BLOCK
    nblk = k_ref.shape[0] // bs
    wide = HEAD_DIM + LANES
    bf16 = jnp.bfloat16
    f32 = jnp.float32
    exp2_scale = (HEAD_DIM ** -0.5) * 1.4426950408889634
    grp = ATTN_GROUP * bs

    @pl.when(i == 0)
    def _():
        sub_k = lax.broadcasted_iota(jnp.int32, (LANES, bs), 0)
        for hh in range(ATTN_HEADS):
            hs = slice(hh * HEAD_DIM, (hh + 1) * HEAD_DIM)
            vx_ref[hh, :, :HEAD_DIM] = v_ref[:, hs]
            vx_ref[hh, :, HEAD_DIM:] = jnp.ones((nblk * bs, LANES), bf16)
            for jb in range(nblk):
                cs = slice(jb * bs, (jb + 1) * bs)
                kx_ref[hh, :HEAD_DIM, cs] = k_ref[cs, hs].astype(f32).T.astype(bf16)
                kx_ref[hh, HEAD_DIM:, cs] = jnp.where(sub_k == jb, 1.0, 0.0).astype(bf16)

    for c in range(nblk // ATTN_GROUP):
        @pl.when(i // ATTN_GROUP == c)
        def _(c=c):
            nk = (c + 1) * grp
            row = lax.broadcasted_iota(jnp.int32, (ATTN_STRIP, LANES), 0) + i * bs
            col = lax.broadcasted_iota(jnp.int32, (ATTN_STRIP, LANES), 1)
            for hh in range(ATTN_HEADS):
                s_ref[hh, :, :nk] = jnp.dot(qx_ref[:, hh * wide:(hh + 1) * wide], kx_ref[hh, :, 0:nk],
                                            preferred_element_type=f32)
            for hh in range(ATTN_HEADS):
                for r0 in range(0, bs, ATTN_STRIP):
                    rs = slice(r0, r0 + ATTN_STRIP)
                    mx = None
                    for t in range(nk // LANES):
                        ts = slice(t * LANES, (t + 1) * LANES)
                        st = s_ref[hh, rs, ts]
                        if t * LANES >= c * grp:
                            st = jnp.where(col + t * LANES <= row + r0, st, NEG_INF)
                            s_ref[hh, rs, ts] = st
                        mx = st if mx is None else jnp.maximum(mx, st)
                    mb = jnp.broadcast_to(jnp.max(mx, axis=-1, keepdims=True) * exp2_scale, (ATTN_STRIP, LANES))
                    for t in range(nk // LANES):
                        ts = slice(t * LANES, (t + 1) * LANES)
                        p_ref[hh, rs, ts] = jnp.exp2(s_ref[hh, rs, ts] * exp2_scale - mb).astype(bf16)
            for hh in range(ATTN_HEADS):
                acc = jnp.dot(p_ref[hh, :, :nk], vx_ref[hh, 0:nk, :], preferred_element_type=f32)
                o_ref[:, hh * HEAD_DIM:(hh + 1) * HEAD_DIM] = (
                    acc[:, :HEAD_DIM] / acc[:, HEAD_DIM:]).astype(o_ref.dtype)


def _attention(qx, z, nb, seq, k_col, v_col):
    n = z.shape[0]
    nq = seq // MOBA_BLOCK
    bs = MOBA_BLOCK
    wide = HEAD_DIM + LANES
    hps = ATTN_HEADS
    assert N_HEADS % hps == 0 and k_col % hps == 0 and v_col % hps == 0
    return pl.pallas_call(
        _attn_kernel,
        grid=(nb, N_HEADS // hps, nq),
        in_specs=[pl.BlockSpec((bs, hps * wide), lambda b, h, i: (b * nq + i, h)),
                  pl.BlockSpec((seq, hps * HEAD_DIM), lambda b, h, i: (b, k_col // hps + h)),
                  pl.BlockSpec((seq, hps * HEAD_DIM), lambda b, h, i: (b, v_col // hps + h))],
        out_specs=pl.BlockSpec((bs, hps * HEAD_DIM), lambda b, h, i: (b * nq + i, h)),
        out_shape=jax.ShapeDtypeStruct((n, N_HEADS * HEAD_DIM), jnp.bfloat16),
        scratch_shapes=[pltpu.VMEM((hps, wide, seq), jnp.bfloat16),
                        pltpu.VMEM((hps, seq, wide), jnp.bfloat16),
                        pltpu.VMEM((hps, bs, seq), jnp.float32),
                        pltpu.VMEM((hps, bs, seq), jnp.bfloat16)],
        compiler_params=_cparams(("parallel", "parallel", "arbitrary")),
        name="attn",
    )(qx, z, z)


def _conv_kernel(a_ref, b_ref, ha_ref, hb_ref, w_ref, bdw_ref, g_ref, bcn_ref, o_ref, hext_ref, grp_ref):
    i = pl.program_id(1)
    tt = a_ref.shape[0]
    halo = ha_ref.shape[0]
    ch = a_ref.shape[1]
    glu_h = ha_ref[...].astype(jnp.float32) * _sigmoid(hb_ref[...].astype(jnp.float32))
    hext_ref[0:halo, :] = jnp.where(i > 0, glu_h, 0.0)
    hext_ref[halo:halo + tt, :] = a_ref[...].astype(jnp.float32) * _sigmoid(b_ref[...].astype(jnp.float32))
    hext_ref[halo + tt:, :] = jnp.zeros((SUBLANES, ch), jnp.float32)

    first = halo - (CONV_WIDTH - 1)
    acc = jnp.zeros(o_ref.shape, jnp.float32) + bdw_ref[...]
    for s in range(SUBLANES):
        g = None
        for j in range(CONV_WIDTH):
            off = first + j
            if off % SUBLANES != s:
                continue
            term = w_ref[j:j + 1, :] * hext_ref[pl.ds(off - s, tt + SUBLANES), :]
            g = term if g is None else g + term
        if g is None:
            continue
        if s == 0:
            acc = acc + g[:tt]
        else:
            grp_ref[s] = g
            acc = acc + grp_ref[s, pl.ds(s, tt), :]
    y = _layer_norm_rows(acc) * g_ref[...] + bcn_ref[...]
    o_ref[...] = (y * _sigmoid(y)).astype(o_ref.dtype)


def _conv(z, w_dw, b_dw, g_cn, b_cn, nb, seq, a_tile, b_tile):
    n = z.shape[0]
    ch = w_dw.shape[1]
    tt, halo = CONV_TT, CONV_HALO
    nt = seq // tt
    r = tt // halo

    def cur(tile):
        return pl.BlockSpec((tt, ch), lambda b, i: (b * nt + i, tile))

    def prev(tile):
        return pl.BlockSpec((halo, ch), lambda b, i: (jnp.maximum((b * nt + i) * r - 1, 0), tile))

    vec = pl.BlockSpec((1, ch), lambda b, i: (0, 0))
    return pl.pallas_call(
        _conv_kernel,
        grid=(nb, nt),
        in_specs=[cur(a_tile), cur(b_tile), prev(a_tile), prev(b_tile),
                  pl.BlockSpec((CONV_WIDTH, ch), lambda b, i: (0, 0)), vec, vec, vec],
        out_specs=pl.BlockSpec((tt, ch), lambda b, i: (b * nt + i, 0)),
        out_shape=jax.ShapeDtypeStruct((n, ch), jnp.bfloat16),
        scratch_shapes=[pltpu.VMEM((halo + tt + SUBLANES, ch), jnp.float32),
                        pltpu.VMEM((SUBLANES, tt + SUBLANES, ch), jnp.float32)],
        compiler_params=_cparams(("parallel", "parallel")),
        name="conv",
    )(z, z, z, z, w_dw, b_dw.reshape(1, ch), g_cn.reshape(1, ch), b_cn.reshape(1, ch))


def _mix_kernel(hc_ref, o_ref, sgc0_ref, sgc1_ref, sga0_ref, sga1_ref, x_ref, wco_ref, bco_ref, wao_ref, wmx_ref,
                gt_ref, g1_ref, b1_ref, sc_ref, sh_ref, wr_ref, br_ref,
                x1_ref, u2_ref, ri_ref, rw_ref, *, alpha):
    f32 = jnp.float32
    y_conv = jnp.dot(hc_ref[...], wco_ref[...], preferred_element_type=f32) + bco_ref[...]
    y_attn = jnp.dot(o_ref[...], wao_ref[...], preferred_element_type=f32)
    sgc = jnp.concatenate([sgc0_ref[...], sgc1_ref[...]], axis=1).astype(f32)
    sga = jnp.concatenate([sga0_ref[...], sga1_ref[...]], axis=1).astype(f32)
    merged = sgc * y_conv + sga * y_attn
    t_out = jnp.dot(merged.astype(jnp.bfloat16), wmx_ref[...], preferred_element_type=f32)
    x1 = _layer_norm_rows(alpha * x_ref[...] + (1.0 + gt_ref[0]) * t_out) * g1_ref[...] + b1_ref[...]
    x1_ref[...] = x1
    u2 = _layer_norm_rows(x1) * (1.0 + sc_ref[0]) + sh_ref[0]
    _store_slabs(u2_ref, u2)

    bf16 = jnp.bfloat16
    u_hi = u2.astype(bf16)
    u_lo = (u2 - u_hi.astype(f32)).astype(bf16)
    w_hi = wr_ref[...].astype(bf16)
    w_lo = (wr_ref[...] - w_hi.astype(f32)).astype(bf16)
    logits = jnp.dot(jnp.concatenate([u_hi, u_lo, u_hi], axis=1), jnp.concatenate([w_hi, w_hi, w_lo], axis=0),
                     preferred_element_type=f32) + br_ref[...]
    lane = lax.broadcasted_iota(jnp.int32, logits.shape, 1)
    is_grp = lane < N_GROUPS
    gl = jnp.where(is_grp, logits, -jnp.inf)
    gmax = jnp.max(gl, axis=-1, keepdims=True)
    gidx = jnp.min(jnp.where(gl == gmax, lane, LANES), axis=-1, keepdims=True)
    p_top = 1.0 / jnp.sum(jnp.where(is_grp, jnp.exp(logits - gmax), 0.0), axis=-1, keepdims=True)
    lo = N_GROUPS + gidx * EXPERTS_PER_GROUP
    el = jnp.where((lane >= lo) & (lane < lo + EXPERTS_PER_GROUP), logits, -jnp.inf)
    v1 = jnp.max(el, axis=-1, keepdims=True)
    i1 = jnp.min(jnp.where(el == v1, lane, LANES), axis=-1, keepdims=True)
    el = jnp.where(lane == i1, -jnp.inf, el)
    v2 = jnp.max(el, axis=-1, keepdims=True)
    i2 = jnp.min(jnp.where(el == v2, lane, LANES), axis=-1, keepdims=True)
    e2 = jnp.exp(v2 - v1)
    w1 = p_top / (1.0 + e2)
    w2 = p_top * e2 / (1.0 + e2)
    ri_ref[...] = jnp.where(lane == 0, i1 - N_GROUPS, jnp.where(lane == 1, i2 - N_GROUPS, 0))
    rw_ref[...] = jnp.where(lane == 0, w1, jnp.where(lane == 1, w2, 0.0))


def _mix(hc, o_attn, z, x2, wco, bco, wao, wmx, gt1, g1, b1, sc2, sh2, w_r, b_r, seq, gc_tile, ga_tile, alpha):
    n, d = x2.shape
    ch = hc.shape[1]
    tm = MIX_TM
    per_batch = seq // tm

    def rows(width):
        return pl.BlockSpec((tm, width), lambda i: (i, 0))

    def const(shape):
        return pl.BlockSpec(shape, lambda i: (0,) * len(shape), pipeline_mode=pl.Buffered(1))

    def per_b():
        return pl.BlockSpec((1, 1, d), lambda i: (i // per_batch, 0, 0))

    return pl.pallas_call(
        functools.partial(_mix_kernel, alpha=alpha),
        grid=(n // tm,),
        in_specs=[rows(ch), rows(ch),
                  pl.BlockSpec((tm, ch), lambda i: (i, gc_tile)),
                  pl.BlockSpec((tm, ch), lambda i: (i, gc_tile + 1)),
                  pl.BlockSpec((tm, ch), lambda i: (i, ga_tile)),
                  pl.BlockSpec((tm, ch), lambda i: (i, ga_tile + 1)),
                  rows(d),
                  const((ch, d)), const((1, d)), const((ch, d)), const((d, d)),
                  per_b(), const((1, d)), const((1, d)), per_b(), per_b(),
                  const((d, LANES)), const((1, LANES))],
        out_specs=[rows(d), pl.BlockSpec((tm * (d // LANES), LANES), lambda i: (i, 0)), rows(LANES), rows(LANES)],
        out_shape=[jax.ShapeDtypeStruct((n, d), jnp.float32),
                   jax.ShapeDtypeStruct((n * (d // LANES), LANES), jnp.float32),
                   jax.ShapeDtypeStruct((n, LANES), jnp.int32),
                   jax.ShapeDtypeStruct((n, LANES), jnp.float32)],
        compiler_params=_cparams(("parallel",)),
        name="mix",
    )(hc, o_attn, z, z, z, z, x2, wco, bco, wao, wmx, gt1, g1, b1, sc2, sh2, w_r, b_r)


def _meta_kernel(eid_ref, dest_ref, tile_ref, cnt_ref, base_ref, *, tile_rows):
    ph = pl.program_id(0)
    r = pl.program_id(1)
    f32 = jnp.float32
    blk = eid_ref.shape[2]
    e = eid_ref[0]
    sub = lax.broadcasted_iota(jnp.int32, (N_EXPERTS, blk), 0)
    oh = (sub == e).astype(f32)
    blk_cnt = jnp.broadcast_to(jnp.sum(oh, axis=1, keepdims=True), (N_EXPERTS, LANES))

    @pl.when((ph == 0) & (r == 0))
    def _():
        cnt_ref[...] = jnp.zeros_like(cnt_ref)

    @pl.when(ph == 0)
    def _():
        cnt_ref[...] += blk_cnt

    @pl.when((ph == 1) & (r == 0))
    def _():
        padded = jnp.ceil(cnt_ref[...] * (1.0 / tile_rows)) * tile_rows
        er = lax.broadcasted_iota(jnp.int32, (N_EXPERTS, N_EXPERTS), 0)
        ec = lax.broadcasted_iota(jnp.int32, (N_EXPERTS, N_EXPERTS), 1)
        starts = jnp.dot((ec < er).astype(f32), padded, preferred_element_type=f32,
                         precision=lax.Precision.HIGHEST)
        base_ref[...] = starts
        ends = starts + padded
        lane = lax.broadcasted_iota(jnp.int32, (N_EXPERTS, LANES), 1)
        tile_start = (lane * tile_rows).astype(f32)
        owner = jnp.sum((ends <= tile_start).astype(f32), axis=0, keepdims=True)
        total_tiles = jnp.max(ends, axis=0, keepdims=True) * (1.0 / tile_rows)
        lane1 = lax.broadcasted_iota(jnp.int32, (1, LANES), 1)
        tile_ref[...] = jnp.where(lane1 == LANES - 1, total_tiles, owner).astype(jnp.int32)

    @pl.when(ph == 1)
    def _():
        ar = lax.broadcasted_iota(jnp.int32, (blk, blk), 0)
        ac = lax.broadcasted_iota(jnp.int32, (blk, blk), 1)
        before = (ar < ac).astype(jnp.bfloat16)
        cum = jnp.dot(oh.astype(jnp.bfloat16), before, preferred_element_type=f32)
        slot = jnp.sum(oh * (base_ref[:, 0:1] + cum), axis=0, keepdims=True)
        dest_ref[0] = slot.astype(jnp.int32)
        base_ref[...] += blk_cnt


def _meta(eid_flat, tile_rows):
    na = eid_flat.shape[0]
    nblk = na // META_BLK
    eid3 = eid_flat.reshape(nblk, 1, META_BLK)
    dest, tiles = pl.pallas_call(
        functools.partial(_meta_kernel, tile_rows=tile_rows),
        grid=(2, nblk),
        in_specs=[pl.BlockSpec((1, 1, META_BLK), lambda ph, r: (r, 0, 0))],
        out_specs=[pl.BlockSpec((1, 1, META_BLK), lambda ph, r: (ph * r, 0, 0)),
                   pl.BlockSpec((1, LANES), lambda ph, r: (0, 0))],
        out_shape=[jax.ShapeDtypeStruct((nblk, 1, META_BLK), jnp.int32),
                   jax.ShapeDtypeStruct((1, LANES), jnp.int32)],
        scratch_shapes=[pltpu.VMEM((N_EXPERTS, LANES), jnp.float32),
                        pltpu.VMEM((N_EXPERTS, LANES), jnp.float32)],
        compiler_params=_cparams(("arbitrary", "arbitrary")),
        name="meta",
    )(eid3)
    return dest.reshape(na), tiles.reshape(LANES)


def _invert_kernel(dest_ref, inv_ref, *, n_tok):
    def clear(s, c):
        inv_ref[s] = 0
        return c

    lax.fori_loop(0, inv_ref.shape[0], clear, 0, unroll=8)
    for k in range(TOPK_IN_GROUP):
        def put(tok, c, k=k):
            inv_ref[dest_ref[k * n_tok + tok]] = tok
            return c

        lax.fori_loop(0, n_tok, put, 0, unroll=8)


def _invert(dest, n_tok, n_inv):
    return pl.pallas_call(
        functools.partial(_invert_kernel, n_tok=n_tok),
        in_specs=[pl.BlockSpec(memory_space=pltpu.SMEM)],
        out_specs=pl.BlockSpec(memory_space=pltpu.SMEM),
        out_shape=jax.ShapeDtypeStruct((n_inv,), jnp.int32),
        name="invert",
    )(dest)


def _expert_kernel(tile_ref, inv_ref, u_ref, w1_ref, w3_ref, w2_ref, y_ref,
                   xbuf0, xbuf1, wf1, wf3, wf2, w1b, w3b, w2b, wslot_ref, xsem, wsem):
    t = pl.program_id(0)
    slab = u_ref.shape[1]
    tm = xbuf0.shape[0] // slab
    bf16 = jnp.bfloat16
    n_used = tile_ref[LANES - 1]
    e = tile_ref[t]
    changed = (t == 0) | (e != tile_ref[jnp.maximum(t - 1, 0)])
    active = t < n_used

    xbufs = (xbuf0, xbuf1)

    def row_copy(tok, buf, r):
        return pltpu.make_async_copy(u_ref.at[tok], xbufs[buf].at[pl.ds(r * slab, slab), :], xsem.at[buf])

    def gather(tile, buf):
        for r in range(tm):
            row_copy(inv_ref[tile * tm + r], buf, r).start()

    def gather_wait(buf):
        def one(r, c):
            row_copy(0, buf, r).wait()
            return c

        lax.fori_loop(0, tm, one, 0, unroll=8)

    def weight_copies(ex, ws):
        return (pltpu.make_async_copy(w1_ref.at[ex], wf1.at[ws], wsem.at[ws, 0]),
                pltpu.make_async_copy(w3_ref.at[ex], wf3.at[ws], wsem.at[ws, 1]),
                pltpu.make_async_copy(w2_ref.at[ex], wf2.at[ws], wsem.at[ws, 2]))

    @pl.when(t == 0)
    def _():
        gather(0, 0)
        for cp in weight_copies(e, 0):
            cp.start()
        wslot_ref[0] = 0

    @pl.when(active & changed)
    def _():
        ws = wslot_ref[0]
        for cp in weight_copies(e, ws):
            cp.wait()
        nxt = lax.while_loop(lambda t2: (t2 < n_used) & (tile_ref[t2] == e), lambda t2: t2 + 1, t + 1)

        @pl.when(nxt < n_used)
        def _():
            for cp in weight_copies(tile_ref[nxt], 1 - ws):
                cp.start()

        w1b[...] = wf1[ws].astype(bf16)
        w3b[...] = wf3[ws].astype(bf16)
        w2b[...] = wf2[ws].astype(bf16)
        wslot_ref[0] = 1 - ws

    for buf in range(2):
        @pl.when(active & (t % 2 == buf))
        def _(buf=buf):
            gather_wait(buf)
            gather(t + 1, 1 - buf)
            xb = _load_slabs(xbufs[buf], tm).astype(bf16)
            h1 = jnp.dot(xb, w1b[...], preferred_element_type=jnp.float32)
            h3 = jnp.dot(xb, w3b[...], preferred_element_type=jnp.float32)
            hh = (h1 * _sigmoid(h1) * h3).astype(bf16)
            _store_slabs(y_ref, jnp.dot(hh, w2b[...], preferred_element_type=jnp.float32))

        @pl.when((t == n_used) & (t % 2 == buf))
        def _(buf=buf):
            gather_wait(buf)

    @pl.when(jnp.logical_not(active))
    def _():
        y_ref[...] = jnp.zeros_like(y_ref)


def _experts(tiles, inv, u3, w1, w3, w2, n_tiles):
    slab = u3.shape[1]
    d = slab * LANES
    f = w1.shape[2]
    tm = EXP_TM
    any_spec = pl.BlockSpec(memory_space=pl.ANY)
    return pl.pallas_call(
        _expert_kernel,
        grid_spec=pltpu.PrefetchScalarGridSpec(
            num_scalar_prefetch=2,
            grid=(n_tiles,),
            in_specs=[any_spec, any_spec, any_spec, any_spec],
            out_specs=pl.BlockSpec((tm * slab, LANES), lambda t, tiles, inv: (t, 0)),
            scratch_shapes=[pltpu.VMEM((tm * slab, LANES), jnp.float32),
                            pltpu.VMEM((tm * slab, LANES), jnp.float32),
                            pltpu.VMEM((2, d, f), jnp.float32),
                            pltpu.VMEM((2, d, f), jnp.float32),
                            pltpu.VMEM((2, f, d), jnp.float32),
                            pltpu.VMEM((d, f), jnp.bfloat16),
                            pltpu.VMEM((d, f), jnp.bfloat16),
                            pltpu.VMEM((f, d), jnp.bfloat16),
                            pltpu.SMEM((1,), jnp.int32),
                            pltpu.SemaphoreType.DMA((2,)),
                            pltpu.SemaphoreType.DMA((2, 3))]),
        out_shape=jax.ShapeDtypeStruct((n_tiles * tm * slab, LANES), jnp.float32),
        compiler_params=_cparams(("arbitrary",)),
        name="experts",
    )(tiles, inv, u3, w1, w3, w2)


def _combine_kernel(dest_ref, ys_ref, x1_ref, rw_ref, gt_ref, g2_ref, b2_ref, o_ref, ybuf, sem,
                    *, n_tok, alpha):
    tm = x1_ref.shape[0]
    slab = ys_ref.shape[1]
    base = pl.program_id(0) * tm

    def row_copy(r, k):
        d = dest_ref[k * n_tok + base + r]
        return pltpu.make_async_copy(ys_ref.at[d], ybuf.at[k, pl.ds(r * slab, slab), :], sem)

    def issue(r, c):
        for k in range(TOPK_IN_GROUP):
            row_copy(r, k).start()
        return c

    def drain(r, c):
        for k in range(TOPK_IN_GROUP):
            row_copy(r, k).wait()
        return c

    lax.fori_loop(0, tm, issue, 0, unroll=8)
    lax.fori_loop(0, tm, drain, 0, unroll=8)
    rw = rw_ref[...]
    f = rw[:, 0:1] * _load_slabs(ybuf.at[0], tm) + rw[:, 1:2] * _load_slabs(ybuf.at[1], tm)
    y = _layer_norm_rows(alpha * x1_ref[...] + (1.0 + gt_ref[0]) * f)
    o_ref[...] = y * g2_ref[...] + b2_ref[...]


def _combine(dest, ys, x1, rw, gt2, g2, b2, seq, alpha):
    n, d = x1.shape
    tm = ROW_TM
    per_batch = seq // tm
    return pl.pallas_call(
        functools.partial(_combine_kernel, n_tok=n, alpha=alpha),
        grid_spec=pltpu.PrefetchScalarGridSpec(
            num_scalar_prefetch=1,
            grid=(n // tm,),
            in_specs=[pl.BlockSpec(memory_space=pl.ANY),
                      pl.BlockSpec((tm, d), lambda i, dest: (i, 0)),
                      pl.BlockSpec((tm, LANES), lambda i, dest: (i, 0)),
                      pl.BlockSpec((1, 1, d), lambda i, dest: (i // per_batch, 0, 0)),
                      pl.BlockSpec((1, d), lambda i, dest: (0, 0)),
                      pl.BlockSpec((1, d), lambda i, dest: (0, 0))],
            out_specs=pl.BlockSpec((tm, d), lambda i, dest: (i, 0)),
            scratch_shapes=[pltpu.VMEM((TOPK_IN_GROUP, tm * (d // LANES), LANES), jnp.float32),
                            pltpu.SemaphoreType.DMA(())]),
        out_shape=jax.ShapeDtypeStruct((n, d), jnp.float32),
        compiler_params=_cparams(("arbitrary",)),
        name="combine",
    )(dest, ys, x1, rw, gt2, g2, b2)


def kernel(x, c, positions, w_cond, b_cond, w_in, b_glu, w_dw, b_dw, g_cn, b_cn, w_conv_out, b_conv_out, w_attn_out, w_mix_out, g_ln1, b_ln1, w_grp, b_grp, w_erouter, b_erouter, w1, w3, w2, g_ln2, b_ln2):
    nb, seq, d = x.shape
    depth = w_cond.shape[0]
    n = nb * seq
    ch = w_dw.shape[2]
    aw = N_HEADS * HEAD_DIM
    in_cols = w_in.shape[2]
    alpha = (2.0 * depth) ** 0.25
    bf16 = jnp.bfloat16
    assert 2 * ch + 3 * aw + 2 * d == in_cols and ch == PROJ_TN and aw == PROJ_TN and d == 2 * PROJ_TN
    assert seq % MOBA_BLOCK == 0 and seq // MOBA_BLOCK <= LANES and N_GROUPS + N_EXPERTS <= LANES
    t_a, t_b, t_q, t_k, t_v, t_gc, t_ga = 0, 1, 2, 3, 4, 5, 7
    heads_per_tile = PROJ_TN // HEAD_DIM
    nblk = seq // MOBA_BLOCK
    n_tiles = (n * TOPK_IN_GROUP) // EXP_TM + N_EXPERTS + 1
    assert (n * TOPK_IN_GROUP) % EXP_TM == 0 and n_tiles < LANES

    cos_t, sin_t = _rope_tables(positions)
    x2 = x.reshape(n, d)
    for l in range(depth):
        mod = _mod(c, w_cond[l], b_cond[l]).reshape(nb, 6, 1, d)
        sh1, sc1, gt1, sh2, sc2, gt2 = (mod[:, s] for s in range(6))

        bias = jnp.concatenate([b_glu[l], jnp.zeros((in_cols - 2 * ch,), jnp.float32)]).reshape(1, in_cols)
        z = _in_proj(x2, sc1, sh1, w_in[l].astype(bf16), bias, cos_t, sin_t, seq,
                     rope_tiles=(t_q, t_k + 1), sig_from=t_gc)
        km = _kmean(z, nb * nblk, t_k, aw).reshape(nb, nblk, aw)
        qx = _gate(z, km, nb, seq, t_q)
        o_attn = _attention(qx, z, nb, seq, t_k * heads_per_tile, t_v * heads_per_tile)
        hc = _conv(z, w_dw[l], b_dw[l], g_cn[l], b_cn[l], nb, seq, t_a, t_b)

        w_r = jnp.concatenate([w_grp[l], w_erouter[l],
                               jnp.zeros((d, LANES - N_GROUPS - N_EXPERTS), jnp.float32)], axis=1)
        b_r = jnp.concatenate([b_grp[l], b_erouter[l],
                               jnp.zeros((LANES - N_GROUPS - N_EXPERTS,), jnp.float32)]).reshape(1, LANES)
        x1, u2, ri, rw = _mix(hc, o_attn, z, x2, w_conv_out[l].astype(bf16), b_conv_out[l].reshape(1, d),
                              w_attn_out[l].astype(bf16), w_mix_out[l].astype(bf16), gt1,
                              g_ln1[l].reshape(1, d), b_ln1[l].reshape(1, d), sc2, sh2, w_r, b_r,
                              seq, t_gc, t_ga, alpha)

        eid = ri[:, :TOPK_IN_GROUP].T.reshape(TOPK_IN_GROUP * n)
        dest, tiles = _meta(eid, EXP_TM)
        inv = _invert(dest, n, (n_tiles + 1) * EXP_TM)
        f_in = w1.shape[-1]
        slab = d // LANES
        ys = _experts(tiles, inv, u2.reshape(n, slab, LANES), w1[l].reshape(N_EXPERTS, d, f_in), w3[l].reshape(N_EXPERTS, d, f_in),
                      w2[l].reshape(N_EXPERTS, f_in, d), n_tiles)
        x2 = _combine(dest, ys.reshape(n_tiles * EXP_TM, slab, LANES), x1, rw, gt2, g_ln2[l].reshape(1, d), b_ln2[l].reshape(1, d), seq, alpha)
    return x2.reshape(nb, seq, d)
```

```python
import functools

import jax
import jax.numpy as jnp
from jax import lax
from jax.experimental import pallas as pl
from jax.experimental.pallas import tpu as pltpu

CONV_WIDTH = 31
N_HEADS = 8
HEAD_DIM = 128
ROPE_DIM = HEAD_DIM // 4
ROPE_THETA = 500000.0
MOBA_BLOCK = 256
MOBA_TOPK = 3
N_GROUPS = 4
EXPERTS_PER_GROUP = 8
N_EXPERTS = N_GROUPS * EXPERTS_PER_GROUP
TOPK_IN_GROUP = 2
LN_EPS = 1e-5
NEG_INF = -1e30

LANES = 128
SUBLANES = 8
VMEM_LIMIT = 56 * 1024 * 1024

PROJ_TM = 1024
PROJ_TN = 1024
PROJ_CHUNK = 256
CONV_TT = 256
CONV_HALO = 32
ATTN_GROUP = 4
ATTN_HEADS = 2
ATTN_STRIP = 128
MIX_TM = 256
EXP_TM = 256
ROW_TM = 256
META_BLK = 512


def _cparams(sem, vmem=VMEM_LIMIT):
    return pltpu.CompilerParams(dimension_semantics=sem, vmem_limit_bytes=vmem)


def _layer_norm_rows(x):
    mu = jnp.mean(x, axis=-1, keepdims=True)
    xc = x - mu
    var = jnp.mean(xc * xc, axis=-1, keepdims=True)
    return xc * lax.rsqrt(var + LN_EPS)


def _sigmoid(x):
    return 1.0 / (1.0 + jnp.exp(-x))


def _store_slabs(ref, val):
    m, w = val.shape
    c = w // LANES
    for k in range(c):
        ref[pl.ds(k, m, stride=c), :] = val[:, k * LANES:(k + 1) * LANES]


def _load_slabs(ref, m):
    c = ref.shape[0] // m
    return jnp.concatenate([ref[pl.ds(k, m, stride=c), :] for k in range(c)], axis=1)


def _dot_nt(a, b):
    return lax.dot_general(a, b, (((1,), (1,)), ((), ())), preferred_element_type=jnp.float32)


def _mod_kernel(cb_ref, w_ref, b_ref, o_ref, s_ref):
    @pl.when(pl.program_id(0) == 0)
    def _():
        cb = cb_ref[...]
        s_ref[...] = cb * _sigmoid(cb)

    nb, d, _ = cb_ref.shape
    tn = w_ref.shape[1]
    for cc in range(tn // LANES):
        sl = slice(cc * LANES, (cc + 1) * LANES)
        wc = w_ref[:, sl]
        for b in range(nb):
            p = (wc * s_ref[b]).reshape(d // SUBLANES, SUBLANES, LANES)
            r = jnp.sum(jnp.sum(p, axis=0), axis=0, keepdims=True)
            o_ref[b:b + 1, sl] = r + b_ref[:, sl]


def _mod(c, w_cond, b_cond, tn=512):
    nb, d = c.shape
    n_out = w_cond.shape[1]
    cb = jnp.broadcast_to(c[:, :, None], (nb, d, LANES))
    return pl.pallas_call(
        _mod_kernel,
        grid=(n_out // tn,),
        in_specs=[pl.BlockSpec((nb, d, LANES), lambda j: (0, 0, 0)),
                  pl.BlockSpec((d, tn), lambda j: (0, j)),
                  pl.BlockSpec((1, tn), lambda j: (0, j))],
        out_specs=pl.BlockSpec((nb, tn), lambda j: (0, j)),
        out_shape=jax.ShapeDtypeStruct((nb, n_out), jnp.float32),
        scratch_shapes=[pltpu.VMEM((nb, d, LANES), jnp.float32)],
        compiler_params=_cparams(("arbitrary",)),
        name="mod",
    )(cb, w_cond, b_cond.reshape(1, n_out))


def _rope_tab_kernel(pos_ref, invf_ref, sign_ref, cos_ref, sin_ref):
    ang = pos_ref[...] * invf_ref[...]
    cos_ref[...] = jnp.cos(ang)
    sin_ref[...] = jnp.sin(ang) * sign_ref[...]


def _rope_tables(positions, tm=1024):
    n = positions.size
    half = ROPE_DIM // 2
    inv_freq = jnp.power(ROPE_THETA, -jnp.arange(half, dtype=jnp.float32) / half)
    invf = jnp.concatenate([inv_freq, inv_freq, jnp.zeros((LANES - ROPE_DIM,), jnp.float32)]).reshape(1, LANES)
    sign = jnp.concatenate([-jnp.ones((half,), jnp.float32),
                            jnp.ones((LANES - half,), jnp.float32)]).reshape(1, LANES)
    posb = jnp.broadcast_to(positions.astype(jnp.float32).reshape(n, 1), (n, LANES))
    vec = pl.BlockSpec((1, LANES), lambda i: (0, 0))
    tab = pl.BlockSpec((tm, LANES), lambda i: (i, 0))
    return pl.pallas_call(
        _rope_tab_kernel,
        grid=(n // tm,),
        in_specs=[tab, vec, vec],
        out_specs=[tab, tab],
        out_shape=[jax.ShapeDtypeStruct((n, LANES), jnp.float32)] * 2,
        compiler_params=_cparams(("parallel",)),
        name="rope_tab",
    )(posb, invf, sign)


def _in_proj_kernel(x_ref, sc_ref, sh_ref, w_ref, bias_ref, cos_ref, sin_ref, o_ref, u_ref,
                    *, rope_tiles, sig_from):
    j = pl.program_id(1)

    @pl.when(j == 0)
    def _():
        xn = _layer_norm_rows(x_ref[...])
        u_ref[...] = (xn * (1.0 + sc_ref[0]) + sh_ref[0]).astype(u_ref.dtype)

    is_rope = (j >= rope_tiles[0]) & (j < rope_tiles[1])

    def project(epilogue):
        for n in range(w_ref.shape[1] // PROJ_CHUNK):
            cs = slice(n * PROJ_CHUNK, (n + 1) * PROJ_CHUNK)
            acc = jnp.dot(u_ref[...], w_ref[:, cs], preferred_element_type=jnp.float32) + bias_ref[:, cs]
            o_ref[:, cs] = epilogue(acc).astype(o_ref.dtype)

    @pl.when(is_rope)
    def _():
        cosf = cos_ref[...]
        sinf = sin_ref[...]
        lane = lax.broadcasted_iota(jnp.int32, cosf.shape, 1)
        first = lane < (ROPE_DIM // 2)

        def rope(acc):
            heads = []
            for h in range(acc.shape[1] // HEAD_DIM):
                xh = acc[:, h * HEAD_DIM:(h + 1) * HEAD_DIM]
                partner = jnp.where(first, pltpu.roll(xh, HEAD_DIM - ROPE_DIM // 2, 1),
                                    pltpu.roll(xh, ROPE_DIM // 2, 1))
                heads.append(xh * cosf + partner * sinf)
            return jnp.concatenate(heads, axis=1)

        project(rope)

    @pl.when(j >= sig_from)
    def _():
        project(_sigmoid)

    @pl.when(jnp.logical_not(is_rope) & (j < sig_from))
    def _():
        project(lambda acc: acc)


def _in_proj(x2, sc1, sh1, w_bf, bias, cos_t, sin_t, seq, rope_tiles, sig_from):
    n, d = x2.shape
    cols = w_bf.shape[1]
    tm, tn = PROJ_TM, PROJ_TN
    per_batch = seq // tm
    kern = functools.partial(_in_proj_kernel, rope_tiles=rope_tiles, sig_from=sig_from)
    return pl.pallas_call(
        kern,
        grid=(n // tm, cols // tn),
        in_specs=[pl.BlockSpec((tm, d), lambda i, j: (i, 0)),
                  pl.BlockSpec((1, 1, d), lambda i, j: (i // per_batch, 0, 0)),
                  pl.BlockSpec((1, 1, d), lambda i, j: (i // per_batch, 0, 0)),
                  pl.BlockSpec((d, tn), lambda i, j: (0, j)),
                  pl.BlockSpec((1, tn), lambda i, j: (0, j)),
                  pl.BlockSpec((tm, LANES), lambda i, j: (i, 0)),
                  pl.BlockSpec((tm, LANES), lambda i, j: (i, 0))],
        out_specs=pl.BlockSpec((tm, tn), lambda i, j: (i, j)),
        out_shape=jax.ShapeDtypeStruct((n, cols), jnp.bfloat16),
        scratch_shapes=[pltpu.VMEM((tm, d), jnp.bfloat16)],
        compiler_params=_cparams(("parallel", "arbitrary")),
        name="in_proj",
    )(x2, sc1, sh1, w_bf, bias, cos_t, sin_t)


def _kmean_kernel(k_ref, o_ref):
    k = k_ref[...].astype(jnp.float32)
    o_ref[0] = (jnp.sum(k, axis=0, keepdims=True) * (1.0 / k.shape[0])).astype(o_ref.dtype)


def _kmean(z, n_blocks, k_tile, width):
    return pl.pallas_call(
        _kmean_kernel,
        grid=(n_blocks,),
        in_specs=[pl.BlockSpec((MOBA_BLOCK, width), lambda i: (i, k_tile))],
        out_specs=pl.BlockSpec((1, 1, width), lambda i: (i, 0, 0)),
        out_shape=jax.ShapeDtypeStruct((n_blocks, 1, width), jnp.bfloat16),
        compiler_params=_cparams(("parallel",)),
        name="kmean",
    )(z)


def _gate_kernel(q_ref, km_ref, qx_ref):
    i = pl.program_id(1)
    bs = q_ref.shape[0]
    nblk = km_ref.shape[1]
    sub = lax.broadcasted_iota(jnp.int32, (nblk, bs), 0)
    past = sub < i
    for h in range(N_HEADS):
        hs = slice(h * HEAD_DIM, (h + 1) * HEAD_DIM)
        q = q_ref[:, hs]
        g = jnp.where(past, _dot_nt(km_ref[0, :, hs], q), -jnp.inf)
        attend = sub == i
        for _ in range(MOBA_TOPK):
            m = jnp.max(g, axis=0, keepdims=True)
            idx = jnp.min(jnp.where(g == m, sub, nblk), axis=0, keepdims=True)
            hit = sub == idx
            attend = attend | (hit & past)
            g = jnp.where(hit, -jnp.inf, g)
        att = jnp.concatenate([jnp.where(attend, 1.0, 0.0), jnp.ones((LANES - nblk, bs), jnp.float32)], axis=0)
        bias = jnp.where(att.T > 0.5, 0.0, NEG_INF)
        qx_ref[:, 2 * h * HEAD_DIM:(2 * h + 1) * HEAD_DIM] = q
        qx_ref[:, (2 * h + 1) * HEAD_DIM:(2 * h + 2) * HEAD_DIM] = bias.astype(qx_ref.dtype)


def _gate(z, kmean, nb, seq, q_tile):
    n = z.shape[0]
    nq = seq // MOBA_BLOCK
    aw = N_HEADS * HEAD_DIM
    return pl.pallas_call(
        _gate_kernel,
        grid=(nb, nq),
        in_specs=[pl.BlockSpec((MOBA_BLOCK, aw), lambda b, i: (b * nq + i, q_tile)),
                  pl.BlockSpec((1, nq, aw), lambda b, i: (b, 0, 0))],
        out_specs=pl.BlockSpec((MOBA_BLOCK, 2 * aw), lambda b, i: (b * nq + i, 0)),
        out_shape=jax.ShapeDtypeStruct((n, 2 * aw), jnp.bfloat16),
        compiler_params=_cparams(("parallel", "arbitrary")),
        name="gate",
    )(z, kmean)


def _attn_kernel(qx_ref, k_ref, v_ref, o_ref, kx_ref, vx_ref, s_ref, p_ref):
    i = pl.program_id(2)
    bs = MOBA_BLOCK
    nblk = k_ref.shape[0] // bs
    wide = HEAD_DIM + LANES
    bf16 = jnp.bfloat16
    f32 = jnp.float32
    exp2_scale = (HEAD_DIM ** -0.5) * 1.4426950408889634
    grp = ATTN_GROUP * bs

    @pl.when(i == 0)
    def _():
        sub_k = lax.broadcasted_iota(jnp.int32, (LANES, bs), 0)
        for hh in range(ATTN_HEADS):
            hs = slice(hh * HEAD_DIM, (hh + 1) * HEAD_DIM)
            vx_ref[hh, :, :HEAD_DIM] = v_ref[:, hs]
            vx_ref[hh, :, HEAD_DIM:] = jnp.ones((nblk * bs, LANES), bf16)
            for jb in range(nblk):
                cs = slice(jb * bs, (jb + 1) * bs)
                kx_ref[hh, :HEAD_DIM, cs] = k_ref[cs, hs].astype(f32).T.astype(bf16)
                kx_ref[hh, HEAD_DIM:, cs] = jnp.where(sub_k == jb, 1.0, 0.0).astype(bf16)

    for c in range(nblk // ATTN_GROUP):
        @pl.when(i // ATTN_GROUP == c)
        def _(c=c):
            nk = (c + 1) * grp
            row = lax.broadcasted_iota(jnp.int32, (ATTN_STRIP, LANES), 0) + i * bs
            col = lax.broadcasted_iota(jnp.int32, (ATTN_STRIP, LANES), 1)
            for hh in range(ATTN_HEADS):
                s_ref[hh, :, :nk] = jnp.dot(qx_ref[:, hh * wide:(hh + 1) * wide], kx_ref[hh, :, 0:nk],
                                            preferred_element_type=f32)
            for hh in range(ATTN_HEADS):
                for r0 in range(0, bs, ATTN_STRIP):
                    rs = slice(r0, r0 + ATTN_STRIP)
                    mx = None
                    for t in range(nk // LANES):
                        ts = slice(t * LANES, (t + 1) * LANES)
                        st = s_ref[hh, rs, ts]
                        if t * LANES >= c * grp:
                            st = jnp.where(col + t * LANES <= row + r0, st, NEG_INF)
                            s_ref[hh, rs, ts] = st
                        mx = st if mx is None else jnp.maximum(mx, st)
                    mb = jnp.broadcast_to(jnp.max(mx, axis=-1, keepdims=True) * exp2_scale, (ATTN_STRIP, LANES))
                    for t in range(nk // LANES):
                        ts = slice(t * LANES, (t + 1) * LANES)
                        p_ref[hh, rs, ts] = jnp.exp2(s_ref[hh, rs, ts] * exp2_scale - mb).astype(bf16)
            for hh in range(ATTN_HEADS):
                acc = jnp.dot(p_ref[hh, :, :nk], vx_ref[hh, 0:nk, :], preferred_element_type=f32)
                o_ref[:, hh * HEAD_DIM:(hh + 1) * HEAD_DIM] = (
                    acc[:, :HEAD_DIM] / acc[:, HEAD_DIM:]).astype(o_ref.dtype)


def _attention(qx, z, nb, seq, k_col, v_col):
    n = z.shape[0]
    nq = seq // MOBA_BLOCK
    bs = MOBA_BLOCK
    wide = HEAD_DIM + LANES
    hps = ATTN_HEADS
    assert N_HEADS % hps == 0 and k_col % hps == 0 and v_col % hps == 0
    return pl.pallas_call(
        _attn_kernel,
        grid=(nb, N_HEADS // hps, nq),
        in_specs=[pl.BlockSpec((bs, hps * wide), lambda b, h, i: (b * nq + i, h)),
                  pl.BlockSpec((seq, hps * HEAD_DIM), lambda b, h, i: (b, k_col // hps + h)),
                  pl.BlockSpec((seq, hps * HEAD_DIM), lambda b, h, i: (b, v_col // hps + h))],
        out_specs=pl.BlockSpec((bs, hps * HEAD_DIM), lambda b, h, i: (b * nq + i, h)),
        out_shape=jax.ShapeDtypeStruct((n, N_HEADS * HEAD_DIM), jnp.bfloat16),
        scratch_shapes=[pltpu.VMEM((hps, wide, seq), jnp.bfloat16),
                        pltpu.VMEM((hps, seq, wide), jnp.bfloat16),
                        pltpu.VMEM((hps, bs, seq), jnp.float32),
                        pltpu.VMEM((hps, bs, seq), jnp.bfloat16)],
        compiler_params=_cparams(("parallel", "parallel", "arbitrary")),
        name="attn",
    )(qx, z, z)


def _conv_kernel(a_ref, b_ref, ha_ref, hb_ref, w_ref, bdw_ref, g_ref, bcn_ref, o_ref, hext_ref, grp_ref):
    i = pl.program_id(1)
    tt = a_ref.shape[0]
    halo = ha_ref.shape[0]
    ch = a_ref.shape[1]
    glu_h = ha_ref[...].astype(jnp.float32) * _sigmoid(hb_ref[...].astype(jnp.float32))
    hext_ref[0:halo, :] = jnp.where(i > 0, glu_h, 0.0)
    hext_ref[halo:halo + tt, :] = a_ref[...].astype(jnp.float32) * _sigmoid(b_ref[...].astype(jnp.float32))
    hext_ref[halo + tt:, :] = jnp.zeros((SUBLANES, ch), jnp.float32)

    first = halo - (CONV_WIDTH - 1)
    acc = jnp.zeros(o_ref.shape, jnp.float32) + bdw_ref[...]
    for s in range(SUBLANES):
        g = None
        for j in range(CONV_WIDTH):
            off = first + j
            if off % SUBLANES != s:
                continue
            term = w_ref[j:j + 1, :] * hext_ref[pl.ds(off - s, tt + SUBLANES), :]
            g = term if g is None else g + term
        if g is None:
            continue
        if s == 0:
            acc = acc + g[:tt]
        else:
            grp_ref[s] = g
            acc = acc + grp_ref[s, pl.ds(s, tt), :]
    y = _layer_norm_rows(acc) * g_ref[...] + bcn_ref[...]
    o_ref[...] = (y * _sigmoid(y)).astype(o_ref.dtype)


def _conv(z, w_dw, b_dw, g_cn, b_cn, nb, seq, a_tile, b_tile):
    n = z.shape[0]
    ch = w_dw.shape[1]
    tt, halo = CONV_TT, CONV_HALO
    nt = seq // tt
    r = tt // halo

    def cur(tile):
        return pl.BlockSpec((tt, ch), lambda b, i: (b * nt + i, tile))

    def prev(tile):
        return pl.BlockSpec((halo, ch), lambda b, i: (jnp.maximum((b * nt + i) * r - 1, 0), tile))

    vec = pl.BlockSpec((1, ch), lambda b, i: (0, 0))
    return pl.pallas_call(
        _conv_kernel,
        grid=(nb, nt),
        in_specs=[cur(a_tile), cur(b_tile), prev(a_tile), prev(b_tile),
                  pl.BlockSpec((CONV_WIDTH, ch), lambda b, i: (0, 0)), vec, vec, vec],
        out_specs=pl.BlockSpec((tt, ch), lambda b, i: (b * nt + i, 0)),
        out_shape=jax.ShapeDtypeStruct((n, ch), jnp.bfloat16),
        scratch_shapes=[pltpu.VMEM((halo + tt + SUBLANES, ch), jnp.float32),
                        pltpu.VMEM((SUBLANES, tt + SUBLANES, ch), jnp.float32)],
        compiler_params=_cparams(("parallel", "parallel")),
        name="conv",
    )(z, z, z, z, w_dw, b_dw.reshape(1, ch), g_cn.reshape(1, ch), b_cn.reshape(1, ch))


def _mix_kernel(hc_ref, o_ref, sgc0_ref, sgc1_ref, sga0_ref, sga1_ref, x_ref, wco_ref, bco_ref, wao_ref, wmx_ref,
                gt_ref, g1_ref, b1_ref, sc_ref, sh_ref, wr_ref, br_ref,
                x1_ref, u2_ref, ri_ref, rw_ref, *, alpha):
    f32 = jnp.float32
    y_conv = jnp.dot(hc_ref[...], wco_ref[...], preferred_element_type=f32) + bco_ref[...]
    y_attn = jnp.dot(o_ref[...], wao_ref[...], preferred_element_type=f32)
    sgc = jnp.concatenate([sgc0_ref[...], sgc1_ref[...]], axis=1).astype(f32)
    sga = jnp.concatenate([sga0_ref[...], sga1_ref[...]], axis=1).astype(f32)
    merged = sgc * y_conv + sga * y_attn
    t_out = jnp.dot(merged.astype(jnp.bfloat16), wmx_ref[...], preferred_element_type=f32)
    x1 = _layer_norm_rows(alpha * x_ref[...] + (1.0 + gt_ref[0]) * t_out) * g1_ref[...] + b1_ref[...]
    x1_ref[...] = x1
    u2 = _layer_norm_rows(x1) * (1.0 + sc_ref[0]) + sh_ref[0]
    _store_slabs(u2_ref, u2)

    bf16 = jnp.bfloat16
    u_hi = u2.astype(bf16)
    u_lo = (u2 - u_hi.astype(f32)).astype(bf16)
    w_hi = wr_ref[...].astype(bf16)
    w_lo = (wr_ref[...] - w_hi.astype(f32)).astype(bf16)
    logits = jnp.dot(jnp.concatenate([u_hi, u_lo, u_hi], axis=1), jnp.concatenate([w_hi, w_hi, w_lo], axis=0),
                     preferred_element_type=f32) + br_ref[...]
    lane = lax.broadcasted_iota(jnp.int32, logits.shape, 1)
    is_grp = lane < N_GROUPS
    gl = jnp.where(is_grp, logits, -jnp.inf)
    gmax = jnp.max(gl, axis=-1, keepdims=True)
    gidx = jnp.min(jnp.where(gl == gmax, lane, LANES), axis=-1, keepdims=True)
    p_top = 1.0 / jnp.sum(jnp.where(is_grp, jnp.exp(logits - gmax), 0.0), axis=-1, keepdims=True)
    lo = N_GROUPS + gidx * EXPERTS_PER_GROUP
    el = jnp.where((lane >= lo) & (lane < lo + EXPERTS_PER_GROUP), logits, -jnp.inf)
    v1 = jnp.max(el, axis=-1, keepdims=True)
    i1 = jnp.min(jnp.where(el == v1, lane, LANES), axis=-1, keepdims=True)
    el = jnp.where(lane == i1, -jnp.inf, el)
    v2 = jnp.max(el, axis=-1, keepdims=True)
    i2 = jnp.min(jnp.where(el == v2, lane, LANES), axis=-1, keepdims=True)
    e2 = jnp.exp(v2 - v1)
    w1 = p_top / (1.0 + e2)
    w2 = p_top * e2 / (1.0 + e2)
    ri_ref[...] = jnp.where(lane == 0, i1 - N_GROUPS, jnp.where(lane == 1, i2 - N_GROUPS, 0))
    rw_ref[...] = jnp.where(lane == 0, w1, jnp.where(lane == 1, w2, 0.0))


def _mix(hc, o_attn, z, x2, wco, bco, wao, wmx, gt1, g1, b1, sc2, sh2, w_r, b_r, seq, gc_tile, ga_tile, alpha):
    n, d = x2.shape
    ch = hc.shape[1]
    tm = MIX_TM
    per_batch = seq // tm

    def rows(width):
        return pl.BlockSpec((tm, width), lambda i: (i, 0))

    def const(shape):
        return pl.BlockSpec(shape, lambda i: (0,) * len(shape), pipeline_mode=pl.Buffered(1))

    def per_b():
        return pl.BlockSpec((1, 1, d), lambda i: (i // per_batch, 0, 0))

    return pl.pallas_call(
        functools.partial(_mix_kernel, alpha=alpha),
        grid=(n // tm,),
        in_specs=[rows(ch), rows(ch),
                  pl.BlockSpec((tm, ch), lambda i: (i, gc_tile)),
                  pl.BlockSpec((tm, ch), lambda i: (i, gc_tile + 1)),
                  pl.BlockSpec((tm, ch), lambda i: (i, ga_tile)),
                  pl.BlockSpec((tm, ch), lambda i: (i, ga_tile + 1)),
                  rows(d),
                  const((ch, d)), const((1, d)), const((ch, d)), const((d, d)),
                  per_b(), const((1, d)), const((1, d)), per_b(), per_b(),
                  const((d, LANES)), const((1, LANES))],
        out_specs=[rows(d), pl.BlockSpec((tm * (d // LANES), LANES), lambda i: (i, 0)), rows(LANES), rows(LANES)],
        out_shape=[jax.ShapeDtypeStruct((n, d), jnp.float32),
                   jax.ShapeDtypeStruct((n * (d // LANES), LANES), jnp.float32),
                   jax.ShapeDtypeStruct((n, LANES), jnp.int32),
                   jax.ShapeDtypeStruct((n, LANES), jnp.float32)],
        compiler_params=_cparams(("parallel",)),
        name="mix",
    )(hc, o_attn, z, z, z, z, x2, wco, bco, wao, wmx, gt1, g1, b1, sc2, sh2, w_r, b_r)


def _meta_kernel(eid_ref, dest_ref, tile_ref, cnt_ref, base_ref, *, tile_rows):
    ph = pl.program_id(0)
    r = pl.program_id(1)
    f32 = jnp.float32
    blk = eid_ref.shape[2]
    e = eid_ref[0]
    sub = lax.broadcasted_iota(jnp.int32, (N_EXPERTS, blk), 0)
    oh = (sub == e).astype(f32)
    blk_cnt = jnp.broadcast_to(jnp.sum(oh, axis=1, keepdims=True), (N_EXPERTS, LANES))

    @pl.when((ph == 0) & (r == 0))
    def _():
        cnt_ref[...] = jnp.zeros_like(cnt_ref)

    @pl.when(ph == 0)
    def _():
        cnt_ref[...] += blk_cnt

    @pl.when((ph == 1) & (r == 0))
    def _():
        padded = jnp.ceil(cnt_ref[...] * (1.0 / tile_rows)) * tile_rows
        er = lax.broadcasted_iota(jnp.int32, (N_EXPERTS, N_EXPERTS), 0)
        ec = lax.broadcasted_iota(jnp.int32, (N_EXPERTS, N_EXPERTS), 1)
        starts = jnp.dot((ec < er).astype(f32), padded, preferred_element_type=f32,
                         precision=lax.Precision.HIGHEST)
        base_ref[...] = starts
        ends = starts + padded
        lane = lax.broadcasted_iota(jnp.int32, (N_EXPERTS, LANES), 1)
        tile_start = (lane * tile_rows).astype(f32)
        owner = jnp.sum((ends <= tile_start).astype(f32), axis=0, keepdims=True)
        total_tiles = jnp.max(ends, axis=0, keepdims=True) * (1.0 / tile_rows)
        lane1 = lax.broadcasted_iota(jnp.int32, (1, LANES), 1)
        tile_ref[...] = jnp.where(lane1 == LANES - 1, total_tiles, owner).astype(jnp.int32)

    @pl.when(ph == 1)
    def _():
        ar = lax.broadcasted_iota(jnp.int32, (blk, blk), 0)
        ac = lax.broadcasted_iota(jnp.int32, (blk, blk), 1)
        before = (ar < ac).astype(jnp.bfloat16)
        cum = jnp.dot(oh.astype(jnp.bfloat16), before, preferred_element_type=f32)
        slot = jnp.sum(oh * (base_ref[:, 0:1] + cum), axis=0, keepdims=True)
        dest_ref[0] = slot.astype(jnp.int32)
        base_ref[...] += blk_cnt


def _meta(eid_flat, tile_rows):
    na = eid_flat.shape[0]
    nblk = na // META_BLK
    eid3 = eid_flat.reshape(nblk, 1, META_BLK)
    dest, tiles = pl.pallas_call(
        functools.partial(_meta_kernel, tile_rows=tile_rows),
        grid=(2, nblk),
        in_specs=[pl.BlockSpec((1, 1, META_BLK), lambda ph, r: (r, 0, 0))],
        out_specs=[pl.BlockSpec((1, 1, META_BLK), lambda ph, r: (ph * r, 0, 0)),
                   pl.BlockSpec((1, LANES), lambda ph, r: (0, 0))],
        out_shape=[jax.ShapeDtypeStruct((nblk, 1, META_BLK), jnp.int32),
                   jax.ShapeDtypeStruct((1, LANES), jnp.int32)],
        scratch_shapes=[pltpu.VMEM((N_EXPERTS, LANES), jnp.float32),
                        pltpu.VMEM((N_EXPERTS, LANES), jnp.float32)],
        compiler_params=_cparams(("arbitrary", "arbitrary")),
        name="meta",
    )(eid3)
    return dest.reshape(na), tiles.reshape(LANES)


def _invert_kernel(dest_ref, inv_ref, *, n_tok):
    def clear(s, c):
        inv_ref[s] = 0
        return c

    lax.fori_loop(0, inv_ref.shape[0], clear, 0, unroll=8)
    for k in range(TOPK_IN_GROUP):
        def put(tok, c, k=k):
            inv_ref[dest_ref[k * n_tok + tok]] = tok
            return c

        lax.fori_loop(0, n_tok, put, 0, unroll=8)


def _invert(dest, n_tok, n_inv):
    return pl.pallas_call(
        functools.partial(_invert_kernel, n_tok=n_tok),
        in_specs=[pl.BlockSpec(memory_space=pltpu.SMEM)],
        out_specs=pl.BlockSpec(memory_space=pltpu.SMEM),
        out_shape=jax.ShapeDtypeStruct((n_inv,), jnp.int32),
        name="invert",
    )(dest)


def _expert_kernel(tile_ref, inv_ref, u_ref, w1_ref, w3_ref, w2_ref, y_ref,
                   xbuf0, xbuf1, wf1, wf3, wf2, w1b, w3b, w2b, wslot_ref, xsem, wsem):
    t = pl.program_id(0)
    slab = u_ref.shape[1]
    tm = xbuf0.shape[0] // slab
    bf16 = jnp.bfloat16
    n_used = tile_ref[LANES - 1]
    e = tile_ref[t]
    changed = (t == 0) | (e != tile_ref[jnp.maximum(t - 1, 0)])
    active = t < n_used

    xbufs = (xbuf0, xbuf1)

    def row_copy(tok, buf, r):
        return pltpu.make_async_copy(u_ref.at[tok], xbufs[buf].at[pl.ds(r * slab, slab), :], xsem.at[buf])

    def gather(tile, buf):
        for r in range(tm):
            row_copy(inv_ref[tile * tm + r], buf, r).start()

    def gather_wait(buf):
        def one(r, c):
            row_copy(0, buf, r).wait()
            return c

        lax.fori_loop(0, tm, one, 0, unroll=8)

    def weight_copies(ex, ws):
        return (pltpu.make_async_copy(w1_ref.at[ex], wf1.at[ws], wsem.at[ws, 0]),
                pltpu.make_async_copy(w3_ref.at[ex], wf3.at[ws], wsem.at[ws, 1]),
                pltpu.make_async_copy(w2_ref.at[ex], wf2.at[ws], wsem.at[ws, 2]))

    @pl.when(t == 0)
    def _():
        gather(0, 0)
        for cp in weight_copies(e, 0):
            cp.start()
        wslot_ref[0] = 0

    @pl.when(active & changed)
    def _():
        ws = wslot_ref[0]
        for cp in weight_copies(e, ws):
            cp.wait()
        w1b[...] = wf1[ws].astype(bf16)
        w3b[...] = wf3[ws].astype(bf16)
        w2b[...] = wf2[ws].astype(bf16)

    def prefetch_next_expert():
        ws = wslot_ref[0]
        nxt = lax.while_loop(lambda t2: (t2 < n_used) & (tile_ref[t2] == e), lambda t2: t2 + 1, t + 1)

        @pl.when(nxt < n_used)
        def _():
            for cp in weight_copies(tile_ref[nxt], 1 - ws):
                cp.start()

        wslot_ref[0] = 1 - ws

    for buf in range(2):
        @pl.when(active & (t % 2 == buf))
        def _(buf=buf):
            gather_wait(buf)
            gather(t + 1, 1 - buf)
            xb = _load_slabs(xbufs[buf], tm).astype(bf16)
            h1 = jnp.dot(xb, w1b[...], preferred_element_type=jnp.float32)
            h3 = jnp.dot(xb, w3b[...], preferred_element_type=jnp.float32)
            hh = (h1 * _sigmoid(h1) * h3).astype(bf16)
            _store_slabs(y_ref, jnp.dot(hh, w2b[...], preferred_element_type=jnp.float32))

        @pl.when((t == n_used) & (t % 2 == buf))
        def _(buf=buf):
            gather_wait(buf)

    pl.when(active & changed)(prefetch_next_expert)

    @pl.when(jnp.logical_not(active))
    def _():
        y_ref[...] = jnp.zeros_like(y_ref)


def _experts(tiles, inv, u3, w1, w3, w2, n_tiles):
    slab = u3.shape[1]
    d = slab * LANES
    f = w1.shape[2]
    tm = EXP_TM
    any_spec = pl.BlockSpec(memory_space=pl.ANY)
    return pl.pallas_call(
        _expert_kernel,
        grid_spec=pltpu.PrefetchScalarGridSpec(
            num_scalar_prefetch=2,
            grid=(n_tiles,),
            in_specs=[any_spec, any_spec, any_spec, any_spec],
            out_specs=pl.BlockSpec((tm * slab, LANES), lambda t, tiles, inv: (t, 0)),
            scratch_shapes=[pltpu.VMEM((tm * slab, LANES), jnp.float32),
                            pltpu.VMEM((tm * slab, LANES), jnp.float32),
                            pltpu.VMEM((2, d, f), jnp.float32),
                            pltpu.VMEM((2, d, f), jnp.float32),
                            pltpu.VMEM((2, f, d), jnp.float32),
                            pltpu.VMEM((d, f), jnp.bfloat16),
                            pltpu.VMEM((d, f), jnp.bfloat16),
                            pltpu.VMEM((f, d), jnp.bfloat16),
                            pltpu.SMEM((1,), jnp.int32),
                            pltpu.SemaphoreType.DMA((2,)),
                            pltpu.SemaphoreType.DMA((2, 3))]),
        out_shape=jax.ShapeDtypeStruct((n_tiles * tm * slab, LANES), jnp.float32),
        compiler_params=_cparams(("arbitrary",)),
        name="experts",
    )(tiles, inv, u3, w1, w3, w2)


def _combine_kernel(dest_ref, ys_ref, x1_ref, rw_ref, gt_ref, g2_ref, b2_ref, o_ref, ybuf0, ybuf1, sem,
                    *, stride, alpha):
    i = pl.program_id(0)
    tm = x1_ref.shape[0]
    slab = ys_ref.shape[1]
    ybufs = (ybuf0, ybuf1)

    def row_copy(slot, buf, k, r):
        return pltpu.make_async_copy(ys_ref.at[slot], ybufs[buf].at[k, pl.ds(r * slab, slab), :], sem.at[buf])

    def gather(step, buf):
        for r in range(tm):
            for k in range(TOPK_IN_GROUP):
                row_copy(dest_ref[k * stride + step * tm + r], buf, k, r).start()

    def gather_wait(buf):
        def one(r, c):
            for k in range(TOPK_IN_GROUP):
                row_copy(0, buf, k, r).wait()
            return c

        lax.fori_loop(0, tm, one, 0, unroll=8)

    @pl.when(i == 0)
    def _():
        gather(0, 0)

    for buf in range(2):
        @pl.when(i % 2 == buf)
        def _(buf=buf):
            gather_wait(buf)
            gather(i + 1, 1 - buf)
            rw = rw_ref[...]
            f = (rw[:, 0:1] * _load_slabs(ybufs[buf].at[0], tm)
                 + rw[:, 1:2] * _load_slabs(ybufs[buf].at[1], tm))
            y = _layer_norm_rows(alpha * x1_ref[...] + (1.0 + gt_ref[0]) * f)
            o_ref[...] = y * g2_ref[...] + b2_ref[...]

        @pl.when((i == pl.num_programs(0) - 1) & (i % 2 == buf))
        def _(buf=buf):
            gather_wait(1 - buf)


def _combine(dest, ys, x1, rw, gt2, g2, b2, seq, alpha):
    n, d = x1.shape
    tm = ROW_TM
    per_batch = seq // tm
    dest_pad = jnp.concatenate([dest.reshape(TOPK_IN_GROUP, n), jnp.zeros((TOPK_IN_GROUP, tm), jnp.int32)],
                               axis=1).reshape(TOPK_IN_GROUP * (n + tm))
    ybuf = pltpu.VMEM((TOPK_IN_GROUP, tm * (d // LANES), LANES), jnp.float32)
    return pl.pallas_call(
        functools.partial(_combine_kernel, stride=n + tm, alpha=alpha),
        grid_spec=pltpu.PrefetchScalarGridSpec(
            num_scalar_prefetch=1,
            grid=(n // tm,),
            in_specs=[pl.BlockSpec(memory_space=pl.ANY),
                      pl.BlockSpec((tm, d), lambda i, dest: (i, 0)),
                      pl.BlockSpec((tm, LANES), lambda i, dest: (i, 0)),
                      pl.BlockSpec((1, 1, d), lambda i, dest: (i // per_batch, 0, 0)),
                      pl.BlockSpec((1, d), lambda i, dest: (0, 0)),
                      pl.BlockSpec((1, d), lambda i, dest: (0, 0))],
            out_specs=pl.BlockSpec((tm, d), lambda i, dest: (i, 0)),
            scratch_shapes=[ybuf, ybuf, pltpu.SemaphoreType.DMA((2,))]),
        out_shape=jax.ShapeDtypeStruct((n, d), jnp.float32),
        compiler_params=_cparams(("arbitrary",)),
        name="combine",
    )(dest_pad, ys, x1, rw, gt2, g2, b2)


def kernel(x, c, positions, w_cond, b_cond, w_in, b_glu, w_dw, b_dw, g_cn, b_cn, w_conv_out, b_conv_out, w_attn_out, w_mix_out, g_ln1, b_ln1, w_grp, b_grp, w_erouter, b_erouter, w1, w3, w2, g_ln2, b_ln2):
    nb, seq, d = x.shape
    depth = w_cond.shape[0]
    n = nb * seq
    ch = w_dw.shape[2]
    aw = N_HEADS * HEAD_DIM
    in_cols = w_in.shape[2]
    alpha = (2.0 * depth) ** 0.25
    bf16 = jnp.bfloat16
    assert 2 * ch + 3 * aw + 2 * d == in_cols and ch == PROJ_TN and aw == PROJ_TN and d == 2 * PROJ_TN
    assert seq % MOBA_BLOCK == 0 and seq // MOBA_BLOCK <= LANES and N_GROUPS + N_EXPERTS <= LANES
    t_a, t_b, t_q, t_k, t_v, t_gc, t_ga = 0, 1, 2, 3, 4, 5, 7
    heads_per_tile = PROJ_TN // HEAD_DIM
    nblk = seq // MOBA_BLOCK
    n_tiles = (n * TOPK_IN_GROUP) // EXP_TM + N_EXPERTS + 1
    assert (n * TOPK_IN_GROUP) % EXP_TM == 0 and n_tiles < LANES

    cos_t, sin_t = _rope_tables(positions)
    x2 = x.reshape(n, d)
    for l in range(depth):
        mod = _mod(c, w_cond[l], b_cond[l]).reshape(nb, 6, 1, d)
        sh1, sc1, gt1, sh2, sc2, gt2 = (mod[:, s] for s in range(6))

        bias = jnp.concatenate([b_glu[l], jnp.zeros((in_cols - 2 * ch,), jnp.float32)]).reshape(1, in_cols)
        z = _in_proj(x2, sc1, sh1, w_in[l].astype(bf16), bias, cos_t, sin_t, seq,
                     rope_tiles=(t_q, t_k + 1), sig_from=t_gc)
        km = _kmean(z, nb * nblk, t_k, aw).reshape(nb, nblk, aw)
        qx = _gate(z, km, nb, seq, t_q)
        o_attn = _attention(qx, z, nb, seq, t_k * heads_per_tile, t_v * heads_per_tile)
        hc = _conv(z, w_dw[l], b_dw[l], g_cn[l], b_cn[l], nb, seq, t_a, t_b)

        w_r = jnp.concatenate([w_grp[l], w_erouter[l],
                               jnp.zeros((d, LANES - N_GROUPS - N_EXPERTS), jnp.float32)], axis=1)
        b_r = jnp.concatenate([b_grp[l], b_erouter[l],
                               jnp.zeros((LANES - N_GROUPS - N_EXPERTS,), jnp.float32)]).reshape(1, LANES)
        x1, u2, ri, rw = _mix(hc, o_attn, z, x2, w_conv_out[l].astype(bf16), b_conv_out[l].reshape(1, d),
                              w_attn_out[l].astype(bf16), w_mix_out[l].astype(bf16), gt1,
                              g_ln1[l].reshape(1, d), b_ln1[l].reshape(1, d), sc2, sh2, w_r, b_r,
                              seq, t_gc, t_ga, alpha)

        eid = ri[:, :TOPK_IN_GROUP].T.reshape(TOPK_IN_GROUP * n)
        dest, tiles = _meta(eid, EXP_TM)
        inv = _invert(dest, n, (n_tiles + 1) * EXP_TM)
        f_in = w1.shape[-1]
        slab = d // LANES
        ys = _experts(tiles, inv, u2.reshape(n, slab, LANES), w1[l].reshape(N_EXPERTS, d, f_in), w3[l].reshape(N_EXPERTS, d, f_in),
                      w2[l].reshape(N_EXPERTS, f_in, d), n_tiles)
        x2 = _combine(dest, ys.reshape(n_tiles * EXP_TM, slab, LANES), x1, rw, gt2, g_ln2[l].reshape(1, d), b_ln2[l].reshape(1, d), seq, alpha)
    return x2.reshape(nb, seq, d)
```

```python
import functools

import jax
import jax.numpy as jnp
from jax import lax
from jax.experimental import pallas as pl
from jax.experimental.pallas import tpu as pltpu

CONV_WIDTH = 31
N_HEADS = 8
HEAD_DIM = 128
ROPE_DIM = HEAD_DIM // 4
ROPE_THETA = 500000.0
MOBA_BLOCK = 256
MOBA_TOPK = 3
N_GROUPS = 4
EXPERTS_PER_GROUP = 8
N_EXPERTS = N_GROUPS * EXPERTS_PER_GROUP
TOPK_IN_GROUP = 2
LN_EPS = 1e-5
NEG_INF = -1e30

LANES = 128
SUBLANES = 8
VMEM_LIMIT = 56 * 1024 * 1024

PROJ_TM = 1024
PROJ_TN = 1024
PROJ_CHUNK = 256
CONV_TT = 256
CONV_HALO = 32
ATTN_GROUP = 4
ATTN_HEADS = 2
ATTN_STRIP = 128
MIX_TM = 256
EXP_TM = 256
ROW_TM = 256
META_BLK = 512


def _cparams(sem, vmem=VMEM_LIMIT):
    return pltpu.CompilerParams(dimension_semantics=sem, vmem_limit_bytes=vmem)


def _layer_norm_rows(x):
    mu = jnp.mean(x, axis=-1, keepdims=True)
    xc = x - mu
    var = jnp.mean(xc * xc, axis=-1, keepdims=True)
    return xc * lax.rsqrt(var + LN_EPS)


def _sigmoid(x):
    return 1.0 / (1.0 + jnp.exp(-x))


def _store_slabs(ref, val):
    m, w = val.shape
    c = w // LANES
    for k in range(c):
        ref[pl.ds(k, m, stride=c), :] = val[:, k * LANES:(k + 1) * LANES]


def _load_slabs(ref, m):
    c = ref.shape[0] // m
    return jnp.concatenate([ref[pl.ds(k, m, stride=c), :] for k in range(c)], axis=1)


def _dot_nt(a, b):
    return lax.dot_general(a, b, (((1,), (1,)), ((), ())), preferred_element_type=jnp.float32)


def _mod_kernel(cb_ref, w_ref, b_ref, o_ref, s_ref):
    @pl.when(pl.program_id(0) == 0)
    def _():
        cb = cb_ref[...]
        s_ref[...] = cb * _sigmoid(cb)

    nb, d, _ = cb_ref.shape
    tn = w_ref.shape[1]
    for cc in range(tn // LANES):
        sl = slice(cc * LANES, (cc + 1) * LANES)
        wc = w_ref[:, sl]
        for b in range(nb):
            p = (wc * s_ref[b]).reshape(d // SUBLANES, SUBLANES, LANES)
            r = jnp.sum(jnp.sum(p, axis=0), axis=0, keepdims=True)
            o_ref[b:b + 1, sl] = r + b_ref[:, sl]


def _mod(c, w_cond, b_cond, tn=512):
    nb, d = c.shape
    n_out = w_cond.shape[1]
    cb = jnp.broadcast_to(c[:, :, None], (nb, d, LANES))
    return pl.pallas_call(
        _mod_kernel,
        grid=(n_out // tn,),
        in_specs=[pl.BlockSpec((nb, d, LANES), lambda j: (0, 0, 0)),
                  pl.BlockSpec((d, tn), lambda j: (0, j)),
                  pl.BlockSpec((1, tn), lambda j: (0, j))],
        out_specs=pl.BlockSpec((nb, tn), lambda j: (0, j)),
        out_shape=jax.ShapeDtypeStruct((nb, n_out), jnp.float32),
        scratch_shapes=[pltpu.VMEM((nb, d, LANES), jnp.float32)],
        compiler_params=_cparams(("arbitrary",)),
        name="mod",
    )(cb, w_cond, b_cond.reshape(1, n_out))


def _rope_tab_kernel(pos_ref, invf_ref, sign_ref, cos_ref, sin_ref):
    ang = pos_ref[...] * invf_ref[...]
    cos_ref[...] = jnp.cos(ang)
    sin_ref[...] = jnp.sin(ang) * sign_ref[...]


def _rope_tables(positions, tm=1024):
    n = positions.size
    half = ROPE_DIM // 2
    inv_freq = jnp.power(ROPE_THETA, -jnp.arange(half, dtype=jnp.float32) / half)
    invf = jnp.concatenate([inv_freq, inv_freq, jnp.zeros((LANES - ROPE_DIM,), jnp.float32)]).reshape(1, LANES)
    sign = jnp.concatenate([-jnp.ones((half,), jnp.float32),
                            jnp.ones((LANES - half,), jnp.float32)]).reshape(1, LANES)
    posb = jnp.broadcast_to(positions.astype(jnp.float32).reshape(n, 1), (n, LANES))
    vec = pl.BlockSpec((1, LANES), lambda i: (0, 0))
    tab = pl.BlockSpec((tm, LANES), lambda i: (i, 0))
    return pl.pallas_call(
        _rope_tab_kernel,
        grid=(n // tm,),
        in_specs=[tab, vec, vec],
        out_specs=[tab, tab],
        out_shape=[jax.ShapeDtypeStruct((n, LANES), jnp.float32)] * 2,
        compiler_params=_cparams(("parallel",)),
        name="rope_tab",
    )(posb, invf, sign)


def _in_proj_kernel(x_ref, sc_ref, sh_ref, w_ref, bias_ref, cos_ref, sin_ref, o_ref, u_ref,
                    *, rope_tiles, sig_from):
    j = pl.program_id(1)

    @pl.when(j == 0)
    def _():
        xn = _layer_norm_rows(x_ref[...])
        u_ref[...] = (xn * (1.0 + sc_ref[0]) + sh_ref[0]).astype(u_ref.dtype)

    is_rope = (j >= rope_tiles[0]) & (j < rope_tiles[1])

    def project(epilogue):
        for n in range(w_ref.shape[1] // PROJ_CHUNK):
            cs = slice(n * PROJ_CHUNK, (n + 1) * PROJ_CHUNK)
            acc = jnp.dot(u_ref[...], w_ref[:, cs], preferred_element_type=jnp.float32) + bias_ref[:, cs]
            o_ref[:, cs] = epilogue(acc).astype(o_ref.dtype)

    @pl.when(is_rope)
    def _():
        cosf = cos_ref[...]
        sinf = sin_ref[...]
        lane = lax.broadcasted_iota(jnp.int32, cosf.shape, 1)
        first = lane < (ROPE_DIM // 2)

        def rope(acc):
            heads = []
            for h in range(acc.shape[1] // HEAD_DIM):
                xh = acc[:, h * HEAD_DIM:(h + 1) * HEAD_DIM]
                partner = jnp.where(first, pltpu.roll(xh, HEAD_DIM - ROPE_DIM // 2, 1),
                                    pltpu.roll(xh, ROPE_DIM // 2, 1))
                heads.append(xh * cosf + partner * sinf)
            return jnp.concatenate(heads, axis=1)

        project(rope)

    @pl.when(j >= sig_from)
    def _():
        project(_sigmoid)

    @pl.when(jnp.logical_not(is_rope) & (j < sig_from))
    def _():
        project(lambda acc: acc)


def _in_proj(x2, sc1, sh1, w_bf, bias, cos_t, sin_t, seq, rope_tiles, sig_from):
    n, d = x2.shape
    cols = w_bf.shape[1]
    tm, tn = PROJ_TM, PROJ_TN
    per_batch = seq // tm
    kern = functools.partial(_in_proj_kernel, rope_tiles=rope_tiles, sig_from=sig_from)
    return pl.pallas_call(
        kern,
        grid=(n // tm, cols // tn),
        in_specs=[pl.BlockSpec((tm, d), lambda i, j: (i, 0)),
                  pl.BlockSpec((1, 1, d), lambda i, j: (i // per_batch, 0, 0)),
                  pl.BlockSpec((1, 1, d), lambda i, j: (i // per_batch, 0, 0)),
                  pl.BlockSpec((d, tn), lambda i, j: (0, j)),
                  pl.BlockSpec((1, tn), lambda i, j: (0, j)),
                  pl.BlockSpec((tm, LANES), lambda i, j: (i, 0)),
                  pl.BlockSpec((tm, LANES), lambda i, j: (i, 0))],
        out_specs=pl.BlockSpec((tm, tn), lambda i, j: (i, j)),
        out_shape=jax.ShapeDtypeStruct((n, cols), jnp.bfloat16),
        scratch_shapes=[pltpu.VMEM((tm, d), jnp.bfloat16)],
        compiler_params=_cparams(("parallel", "arbitrary")),
        name="in_proj",
    )(x2, sc1, sh1, w_bf, bias, cos_t, sin_t)


def _kmean_kernel(k_ref, o_ref):
    k = k_ref[...].astype(jnp.float32)
    o_ref[0] = (jnp.sum(k, axis=0, keepdims=True) * (1.0 / k.shape[0])).astype(o_ref.dtype)


def _kmean(z, n_blocks, k_tile, width):
    return pl.pallas_call(
        _kmean_kernel,
        grid=(n_blocks,),
        in_specs=[pl.BlockSpec((MOBA_BLOCK, width), lambda i: (i, k_tile))],
        out_specs=pl.BlockSpec((1, 1, width), lambda i: (i, 0, 0)),
        out_shape=jax.ShapeDtypeStruct((n_blocks, 1, width), jnp.bfloat16),
        compiler_params=_cparams(("parallel",)),
        name="kmean",
    )(z)


def _gate_kernel(q_ref, km_ref, qx_ref):
    i = pl.program_id(1)
    bs = q_ref.shape[0]
    nblk = km_ref.shape[1]
    sub = lax.broadcasted_iota(jnp.int32, (nblk, bs), 0)
    past = sub < i
    for h in range(N_HEADS):
        hs = slice(h * HEAD_DIM, (h + 1) * HEAD_DIM)
        q = q_ref[:, hs]
        g = jnp.where(past, _dot_nt(km_ref[0, :, hs], q), -jnp.inf)
        attend = sub == i
        for _ in range(MOBA_TOPK):
            m = jnp.max(g, axis=0, keepdims=True)
            idx = jnp.min(jnp.where(g == m, sub, nblk), axis=0, keepdims=True)
            hit = sub == idx
            attend = attend | (hit & past)
            g = jnp.where(hit, -jnp.inf, g)
        att = jnp.concatenate([jnp.where(attend, 1.0, 0.0), jnp.ones((LANES - nblk, bs), jnp.float32)], axis=0)
        bias = jnp.where(att.T > 0.5, 0.0, NEG_INF)
        qx_ref[:, 2 * h * HEAD_DIM:(2 * h + 1) * HEAD_DIM] = q
        qx_ref[:, (2 * h + 1) * HEAD_DIM:(2 * h + 2) * HEAD_DIM] = bias.astype(qx_ref.dtype)


def _gate(z, kmean, nb, seq, q_tile):
    n = z.shape[0]
    nq = seq // MOBA_BLOCK
    aw = N_HEADS * HEAD_DIM
    return pl.pallas_call(
        _gate_kernel,
        grid=(nb, nq),
        in_specs=[pl.BlockSpec((MOBA_BLOCK, aw), lambda b, i: (b * nq + i, q_tile)),
                  pl.BlockSpec((1, nq, aw), lambda b, i: (b, 0, 0))],
        out_specs=pl.BlockSpec((MOBA_BLOCK, 2 * aw), lambda b, i: (b * nq + i, 0)),
        out_shape=jax.ShapeDtypeStruct((n, 2 * aw), jnp.bfloat16),
        compiler_params=_cparams(("parallel", "arbitrary")),
        name="gate",
    )(z, kmean)


def _attn_kernel(qx_ref, k_ref, v_ref, o_ref, kx_ref, vx_ref, s_ref, p_ref):
    i = pl.program_id(2)
    bs = MOBA_BLOCK
    nblk = k_ref.shape[0] // bs
    wide = HEAD_DIM + LANES
    bf16 = jnp.bfloat16
    f32 = jnp.float32
    exp2_scale = (HEAD_DIM ** -0.5) * 1.4426950408889634
    grp = ATTN_GROUP * bs

    @pl.when(i == 0)
    def _():
        sub_k = lax.broadcasted_iota(jnp.int32, (LANES, bs), 0)
        for hh in range(ATTN_HEADS):
            hs = slice(hh * HEAD_DIM, (hh + 1) * HEAD_DIM)
            vx_ref[hh, :, :HEAD_DIM] = v_ref[:, hs]
            vx_ref[hh, :, HEAD_DIM:] = jnp.ones((nblk * bs, LANES), bf16)
            for jb in range(nblk):
                cs = slice(jb * bs, (jb + 1) * bs)
                kx_ref[hh, :HEAD_DIM, cs] = k_ref[cs, hs].astype(f32).T.astype(bf16)
                kx_ref[hh, HEAD_DIM:, cs] = jnp.where(sub_k == jb, 1.0, 0.0).astype(bf16)

    for c in range(nblk // ATTN_GROUP):
        @pl.when(i // ATTN_GROUP == c)
        def _(c=c):
            nk = (c + 1) * grp
            row = lax.broadcasted_iota(jnp.int32, (ATTN_STRIP, LANES), 0) + i * bs
            col = lax.broadcasted_iota(jnp.int32, (ATTN_STRIP, LANES), 1)
            for hh in range(ATTN_HEADS):
                s_ref[hh, :, :nk] = jnp.dot(qx_ref[:, hh * wide:(hh + 1) * wide], kx_ref[hh, :, 0:nk],
                                            preferred_element_type=f32)
            for hh in range(ATTN_HEADS):
                for r0 in range(0, bs, ATTN_STRIP):
                    rs = slice(r0, r0 + ATTN_STRIP)
                    mx = None
                    for t in range(nk // LANES):
                        ts = slice(t * LANES, (t + 1) * LANES)
                        st = s_ref[hh, rs, ts]
                        if t * LANES >= c * grp:
                            st = jnp.where(col + t * LANES <= row + r0, st, NEG_INF)
                            s_ref[hh, rs, ts] = st
                        mx = st if mx is None else jnp.maximum(mx, st)
                    mb = jnp.broadcast_to(jnp.max(mx, axis=-1, keepdims=True) * exp2_scale, (ATTN_STRIP, LANES))
                    for t in range(nk // LANES):
                        ts = slice(t * LANES, (t + 1) * LANES)
                        p_ref[hh, rs, ts] = jnp.exp2(s_ref[hh, rs, ts] * exp2_scale - mb).astype(bf16)
            for hh in range(ATTN_HEADS):
                acc = jnp.dot(p_ref[hh, :, :nk], vx_ref[hh, 0:nk, :], preferred_element_type=f32)
                o_ref[:, hh * HEAD_DIM:(hh + 1) * HEAD_DIM] = (
                    acc[:, :HEAD_DIM] / acc[:, HEAD_DIM:]).astype(o_ref.dtype)


def _attention(qx, z, nb, seq, k_col, v_col):
    n = z.shape[0]
    nq = seq // MOBA_BLOCK
    bs = MOBA_BLOCK
    wide = HEAD_DIM + LANES
    hps = ATTN_HEADS
    assert N_HEADS % hps == 0 and k_col % hps == 0 and v_col % hps == 0
    return pl.pallas_call(
        _attn_kernel,
        grid=(nb, N_HEADS // hps, nq),
        in_specs=[pl.BlockSpec((bs, hps * wide), lambda b, h, i: (b * nq + i, h)),
                  pl.BlockSpec((seq, hps * HEAD_DIM), lambda b, h, i: (b, k_col // hps + h)),
                  pl.BlockSpec((seq, hps * HEAD_DIM), lambda b, h, i: (b, v_col // hps + h))],
        out_specs=pl.BlockSpec((bs, hps * HEAD_DIM), lambda b, h, i: (b * nq + i, h)),
        out_shape=jax.ShapeDtypeStruct((n, N_HEADS * HEAD_DIM), jnp.bfloat16),
        scratch_shapes=[pltpu.VMEM((hps, wide, seq), jnp.bfloat16),
                        pltpu.VMEM((hps, seq, wide), jnp.bfloat16),
                        pltpu.VMEM((hps, bs, seq), jnp.float32),
                        pltpu.VMEM((hps, bs, seq), jnp.bfloat16)],
        compiler_params=_cparams(("parallel", "parallel", "arbitrary")),
        name="attn",
    )(qx, z, z)


def _conv_kernel(a_ref, b_ref, ha_ref, hb_ref, w_ref, bdw_ref, g_ref, bcn_ref, o_ref, hext_ref, grp_ref):
    i = pl.program_id(1)
    tt = a_ref.shape[0]
    halo = ha_ref.shape[0]
    ch = a_ref.shape[1]
    glu_h = ha_ref[...].astype(jnp.float32) * _sigmoid(hb_ref[...].astype(jnp.float32))
    hext_ref[0:halo, :] = jnp.where(i > 0, glu_h, 0.0)
    hext_ref[halo:halo + tt, :] = a_ref[...].astype(jnp.float32) * _sigmoid(b_ref[...].astype(jnp.float32))
    hext_ref[halo + tt:, :] = jnp.zeros((SUBLANES, ch), jnp.float32)

    first = halo - (CONV_WIDTH - 1)
    acc = jnp.zeros(o_ref.shape, jnp.float32) + bdw_ref[...]
    for s in range(SUBLANES):
        g = None
        for j in range(CONV_WIDTH):
            off = first + j
            if off % SUBLANES != s:
                continue
            term = w_ref[j:j + 1, :] * hext_ref[pl.ds(off - s, tt + SUBLANES), :]
            g = term if g is None else g + term
        if g is None:
            continue
        if s == 0:
            acc = acc + g[:tt]
        else:
            grp_ref[s] = g
            acc = acc + grp_ref[s, pl.ds(s, tt), :]
    y = _layer_norm_rows(acc) * g_ref[...] + bcn_ref[...]
    o_ref[...] = (y * _sigmoid(y)).astype(o_ref.dtype)


def _conv(z, w_dw, b_dw, g_cn, b_cn, nb, seq, a_tile, b_tile):
    n = z.shape[0]
    ch = w_dw.shape[1]
    tt, halo = CONV_TT, CONV_HALO
    nt = seq // tt
    r = tt // halo

    def cur(tile):
        return pl.BlockSpec((tt, ch), lambda b, i: (b * nt + i, tile))

    def prev(tile):
        return pl.BlockSpec((halo, ch), lambda b, i: (jnp.maximum((b * nt + i) * r - 1, 0), tile))

    vec = pl.BlockSpec((1, ch), lambda b, i: (0, 0))
    return pl.pallas_call(
        _conv_kernel,
        grid=(nb, nt),
        in_specs=[cur(a_tile), cur(b_tile), prev(a_tile), prev(b_tile),
                  pl.BlockSpec((CONV_WIDTH, ch), lambda b, i: (0, 0)), vec, vec, vec],
        out_specs=pl.BlockSpec((tt, ch), lambda b, i: (b * nt + i, 0)),
        out_shape=jax.ShapeDtypeStruct((n, ch), jnp.bfloat16),
        scratch_shapes=[pltpu.VMEM((halo + tt + SUBLANES, ch), jnp.float32),
                        pltpu.VMEM((SUBLANES, tt + SUBLANES, ch), jnp.float32)],
        compiler_params=_cparams(("parallel", "parallel")),
        name="conv",
    )(z, z, z, z, w_dw, b_dw.reshape(1, ch), g_cn.reshape(1, ch), b_cn.reshape(1, ch))


def _mix_kernel(hc_ref, o_ref, sgc0_ref, sgc1_ref, sga0_ref, sga1_ref, x_ref, wco_ref, bco_ref, wao_ref, wmx_ref,
                gt_ref, g1_ref, b1_ref, sc_ref, sh_ref, wr_ref, br_ref,
                x1_ref, u2_ref, ri_ref, rw_ref, *, alpha):
    f32 = jnp.float32
    y_conv = jnp.dot(hc_ref[...], wco_ref[...], preferred_element_type=f32) + bco_ref[...]
    y_attn = jnp.dot(o_ref[...], wao_ref[...], preferred_element_type=f32)
    sgc = jnp.concatenate([sgc0_ref[...], sgc1_ref[...]], axis=1).astype(f32)
    sga = jnp.concatenate([sga0_ref[...], sga1_ref[...]], axis=1).astype(f32)
    merged = sgc * y_conv + sga * y_attn
    t_out = jnp.dot(merged.astype(jnp.bfloat16), wmx_ref[...], preferred_element_type=f32)
    x1 = _layer_norm_rows(alpha * x_ref[...] + (1.0 + gt_ref[0]) * t_out) * g1_ref[...] + b1_ref[...]
    x1_ref[...] = x1
    u2 = _layer_norm_rows(x1) * (1.0 + sc_ref[0]) + sh_ref[0]
    _store_slabs(u2_ref, u2)

    bf16 = jnp.bfloat16
    u_hi = u2.astype(bf16)
    u_lo = (u2 - u_hi.astype(f32)).astype(bf16)
    w_hi = wr_ref[...].astype(bf16)
    w_lo = (wr_ref[...] - w_hi.astype(f32)).astype(bf16)
    logits = jnp.dot(jnp.concatenate([u_hi, u_lo, u_hi], axis=1), jnp.concatenate([w_hi, w_hi, w_lo], axis=0),
                     preferred_element_type=f32) + br_ref[...]
    lane = lax.broadcasted_iota(jnp.int32, logits.shape, 1)
    is_grp = lane < N_GROUPS
    gl = jnp.where(is_grp, logits, -jnp.inf)
    gmax = jnp.max(gl, axis=-1, keepdims=True)
    gidx = jnp.min(jnp.where(gl == gmax, lane, LANES), axis=-1, keepdims=True)
    p_top = 1.0 / jnp.sum(jnp.where(is_grp, jnp.exp(logits - gmax), 0.0), axis=-1, keepdims=True)
    lo = N_GROUPS + gidx * EXPERTS_PER_GROUP
    el = jnp.where((lane >= lo) & (lane < lo + EXPERTS_PER_GROUP), logits, -jnp.inf)
    v1 = jnp.max(el, axis=-1, keepdims=True)
    i1 = jnp.min(jnp.where(el == v1, lane, LANES), axis=-1, keepdims=True)
    el = jnp.where(lane == i1, -jnp.inf, el)
    v2 = jnp.max(el, axis=-1, keepdims=True)
    i2 = jnp.min(jnp.where(el == v2, lane, LANES), axis=-1, keepdims=True)
    e2 = jnp.exp(v2 - v1)
    w1 = p_top / (1.0 + e2)
    w2 = p_top * e2 / (1.0 + e2)
    ri_ref[...] = jnp.where(lane == 0, i1 - N_GROUPS, jnp.where(lane == 1, i2 - N_GROUPS, 0))
    rw_ref[...] = jnp.where(lane == 0, w1, jnp.where(lane == 1, w2, 0.0))


def _mix(hc, o_attn, z, x2, wco, bco, wao, wmx, gt1, g1, b1, sc2, sh2, w_r, b_r, seq, gc_tile, ga_tile, alpha):
    n, d = x2.shape
    ch = hc.shape[1]
    tm = MIX_TM
    per_batch = seq // tm

    def rows(width):
        return pl.BlockSpec((tm, width), lambda i: (i, 0))

    def const(shape):
        return pl.BlockSpec(shape, lambda i: (0,) * len(shape), pipeline_mode=pl.Buffered(1))

    def per_b():
        return pl.BlockSpec((1, 1, d), lambda i: (i // per_batch, 0, 0))

    return pl.pallas_call(
        functools.partial(_mix_kernel, alpha=alpha),
        grid=(n // tm,),
        in_specs=[rows(ch), rows(ch),
                  pl.BlockSpec((tm, ch), lambda i: (i, gc_tile)),
                  pl.BlockSpec((tm, ch), lambda i: (i, gc_tile + 1)),
                  pl.BlockSpec((tm, ch), lambda i: (i, ga_tile)),
                  pl.BlockSpec((tm, ch), lambda i: (i, ga_tile + 1)),
                  rows(d),
                  const((ch, d)), const((1, d)), const((ch, d)), const((d, d)),
                  per_b(), const((1, d)), const((1, d)), per_b(), per_b(),
                  const((d, LANES)), const((1, LANES))],
        out_specs=[rows(d), pl.BlockSpec((tm * (d // LANES), LANES), lambda i: (i, 0)), rows(LANES), rows(LANES)],
        out_shape=[jax.ShapeDtypeStruct((n, d), jnp.float32),
                   jax.ShapeDtypeStruct((n * (d // LANES), LANES), jnp.float32),
                   jax.ShapeDtypeStruct((n, LANES), jnp.int32),
                   jax.ShapeDtypeStruct((n, LANES), jnp.float32)],
        compiler_params=_cparams(("parallel",)),
        name="mix",
    )(hc, o_attn, z, z, z, z, x2, wco, bco, wao, wmx, gt1, g1, b1, sc2, sh2, w_r, b_r)


def _meta_kernel(eid_ref, dest_ref, tile_ref, cnt_ref, base_ref, *, tile_rows):
    ph = pl.program_id(0)
    r = pl.program_id(1)
    f32 = jnp.float32
    blk = eid_ref.shape[2]
    e = eid_ref[0]
    sub = lax.broadcasted_iota(jnp.int32, (N_EXPERTS, blk), 0)
    oh = (sub == e).astype(f32)
    blk_cnt = jnp.broadcast_to(jnp.sum(oh, axis=1, keepdims=True), (N_EXPERTS, LANES))

    @pl.when((ph == 0) & (r == 0))
    def _():
        cnt_ref[...] = jnp.zeros_like(cnt_ref)

    @pl.when(ph == 0)
    def _():
        cnt_ref[...] += blk_cnt

    @pl.when((ph == 1) & (r == 0))
    def _():
        padded = jnp.ceil(cnt_ref[...] * (1.0 / tile_rows)) * tile_rows
        er = lax.broadcasted_iota(jnp.int32, (N_EXPERTS, N_EXPERTS), 0)
        ec = lax.broadcasted_iota(jnp.int32, (N_EXPERTS, N_EXPERTS), 1)
        starts = jnp.dot((ec < er).astype(f32), padded, preferred_element_type=f32,
                         precision=lax.Precision.HIGHEST)
        base_ref[...] = starts
        ends = starts + padded
        lane = lax.broadcasted_iota(jnp.int32, (N_EXPERTS, LANES), 1)
        tile_start = (lane * tile_rows).astype(f32)
        owner = jnp.sum((ends <= tile_start).astype(f32), axis=0, keepdims=True)
        total_tiles = jnp.max(ends, axis=0, keepdims=True) * (1.0 / tile_rows)
        lane1 = lax.broadcasted_iota(jnp.int32, (1, LANES), 1)
        tile_ref[0:1, :] = jnp.where(lane1 == LANES - 1, total_tiles, owner).astype(jnp.int32)
        sub_e = lax.broadcasted_iota(jnp.int32, (N_EXPERTS, LANES), 0)
        diag = sub_e == lane
        tile_ref[1:2, :] = jnp.sum(jnp.where(diag, starts + cnt_ref[...], 0.0), axis=0,
                                   keepdims=True).astype(jnp.int32)
        tile_ref[2:3, :] = jnp.sum(jnp.where(diag, ends, 0.0), axis=0, keepdims=True).astype(jnp.int32)

    @pl.when(ph == 1)
    def _():
        ar = lax.broadcasted_iota(jnp.int32, (blk, blk), 0)
        ac = lax.broadcasted_iota(jnp.int32, (blk, blk), 1)
        before = (ar < ac).astype(jnp.bfloat16)
        cum = jnp.dot(oh.astype(jnp.bfloat16), before, preferred_element_type=f32)
        slot = jnp.sum(oh * (base_ref[:, 0:1] + cum), axis=0, keepdims=True)
        dest_ref[0] = slot.astype(jnp.int32)
        base_ref[...] += blk_cnt


def _meta(eid_flat, tile_rows):
    na = eid_flat.shape[0]
    nblk = na // META_BLK
    eid3 = eid_flat.reshape(nblk, 1, META_BLK)
    dest, tiles = pl.pallas_call(
        functools.partial(_meta_kernel, tile_rows=tile_rows),
        grid=(2, nblk),
        in_specs=[pl.BlockSpec((1, 1, META_BLK), lambda ph, r: (r, 0, 0))],
        out_specs=[pl.BlockSpec((1, 1, META_BLK), lambda ph, r: (ph * r, 0, 0)),
                   pl.BlockSpec((3, LANES), lambda ph, r: (0, 0))],
        out_shape=[jax.ShapeDtypeStruct((nblk, 1, META_BLK), jnp.int32),
                   jax.ShapeDtypeStruct((3, LANES), jnp.int32)],
        scratch_shapes=[pltpu.VMEM((N_EXPERTS, LANES), jnp.float32),
                        pltpu.VMEM((N_EXPERTS, LANES), jnp.float32)],
        compiler_params=_cparams(("arbitrary", "arbitrary")),
        name="meta",
    )(eid3)
    return dest.reshape(na), tiles.reshape(3 * LANES)


def _dispatch_kernel(dest_ref, seg_ref, u_ref, xs_ref, zero_ref, sem, zsem, *, n_tok, tile_rows):
    i = pl.program_id(0)
    slab = xs_ref.shape[1]
    tm = u_ref.shape[0] // slab
    base = i * tm

    @pl.when(i == 0)
    def _():
        zero_ref[...] = jnp.zeros_like(zero_ref)

        def fills(e, act):
            lo = seg_ref[LANES + e]
            pad = seg_ref[2 * LANES + e] - lo
            pos = lo
            for bit in reversed(range(tile_rows.bit_length() - 1)):
                rows = 1 << bit
                take = (pad >> bit) & 1

                @pl.when(take == 1)
                def _(pos=pos, rows=rows):
                    act(pltpu.make_async_copy(zero_ref.at[pl.ds(0, rows)], xs_ref.at[pl.ds(pos, rows)], zsem))

                pos = pos + take * rows

        def start_fills(e, c):
            fills(e, lambda cp: cp.start())
            return c

        def wait_fills(e, c):
            fills(e, lambda cp: cp.wait())
            return c

        run = zero_ref.shape[0]
        used = seg_ref[2 * LANES + N_EXPERTS - 1]
        n_runs = (xs_ref.shape[0] - used) // run

        def tail_fill(j):
            return pltpu.make_async_copy(zero_ref, xs_ref.at[pl.ds(used + j * run, run)], zsem)

        def start_tail(j, c):
            tail_fill(j).start()
            return c

        def wait_tail(j, c):
            tail_fill(j).wait()
            return c

        lax.fori_loop(0, N_EXPERTS, start_fills, 0)
        lax.fori_loop(0, n_runs, start_tail, 0)
        lax.fori_loop(0, N_EXPERTS, wait_fills, 0)
        lax.fori_loop(0, n_runs, wait_tail, 0)

    def row_copy(r, k):
        d = dest_ref[k * n_tok + base + r]
        return pltpu.make_async_copy(u_ref.at[pl.ds(r * slab, slab), :], xs_ref.at[d], sem)

    for r in range(tm):
        for k in range(TOPK_IN_GROUP):
            row_copy(r, k).start()

    def drain(r, c):
        for k in range(TOPK_IN_GROUP):
            row_copy(r, k).wait()
        return c

    lax.fori_loop(0, tm, drain, 0, unroll=8)


def _dispatch(dest, tiles, u2s, n_tok, n_slots, tile_rows):
    slab = u2s.shape[0] // n_tok
    tm = ROW_TM
    assert tile_rows & (tile_rows - 1) == 0
    return pl.pallas_call(
        functools.partial(_dispatch_kernel, n_tok=n_tok, tile_rows=tile_rows),
        grid_spec=pltpu.PrefetchScalarGridSpec(
            num_scalar_prefetch=2,
            grid=(n_tok // tm,),
            in_specs=[pl.BlockSpec((tm * slab, LANES), lambda i, dest, seg: (i, 0))],
            out_specs=pl.BlockSpec(memory_space=pl.ANY),
            scratch_shapes=[pltpu.VMEM((tile_rows // 2, slab, LANES), jnp.float32),
                            pltpu.SemaphoreType.DMA(()),
                            pltpu.SemaphoreType.DMA(())]),
        out_shape=jax.ShapeDtypeStruct((n_slots, slab, LANES), jnp.float32),
        compiler_params=_cparams(("arbitrary",)),
        name="dispatch",
    )(dest, tiles, u2s)


def _expert_kernel(tile_ref, x_ref, w1_ref, w3_ref, w2_ref, y_ref, wf1, wf3, wf2, w1b, w3b, w2b, wslot_ref, wsem):
    t = pl.program_id(0)
    tm = EXP_TM
    bf16 = jnp.bfloat16
    n_used = tile_ref[LANES - 1]
    e = tile_ref[t]
    changed = (t == 0) | (e != tile_ref[jnp.maximum(t - 1, 0)])
    active = t < n_used

    def weight_copies(ex, ws):
        return (pltpu.make_async_copy(w1_ref.at[ex], wf1.at[ws], wsem.at[ws, 0]),
                pltpu.make_async_copy(w3_ref.at[ex], wf3.at[ws], wsem.at[ws, 1]),
                pltpu.make_async_copy(w2_ref.at[ex], wf2.at[ws], wsem.at[ws, 2]))

    @pl.when(t == 0)
    def _():
        for cp in weight_copies(e, 0):
            cp.start()
        wslot_ref[0] = 0

    @pl.when(active & changed)
    def _():
        ws = wslot_ref[0]
        for cp in weight_copies(e, ws):
            cp.wait()
        nxt = lax.while_loop(lambda t2: (t2 < n_used) & (tile_ref[t2] == e), lambda t2: t2 + 1, t + 1)

        @pl.when(nxt < n_used)
        def _():
            for cp in weight_copies(tile_ref[nxt], 1 - ws):
                cp.start()

        w1b[...] = wf1[ws].astype(bf16)
        w3b[...] = wf3[ws].astype(bf16)
        w2b[...] = wf2[ws].astype(bf16)
        wslot_ref[0] = 1 - ws

    @pl.when(active)
    def _():
        xb = _load_slabs(x_ref, tm).astype(bf16)
        h1 = jnp.dot(xb, w1b[...], preferred_element_type=jnp.float32)
        h3 = jnp.dot(xb, w3b[...], preferred_element_type=jnp.float32)
        hh = (h1 * _sigmoid(h1) * h3).astype(bf16)
        _store_slabs(y_ref, jnp.dot(hh, w2b[...], preferred_element_type=jnp.float32))

    @pl.when(jnp.logical_not(active))
    def _():
        y_ref[...] = jnp.zeros_like(y_ref)


def _experts(tiles, xs2, w1, w3, w2, n_tiles):
    d, f = w1.shape[1], w1.shape[2]
    tm = EXP_TM
    slab = d // LANES
    any_spec = pl.BlockSpec(memory_space=pl.ANY)

    def x_idx(t, tiles):
        return (jnp.maximum(jnp.minimum(t, tiles[LANES - 1] - 1), 0), 0)

    return pl.pallas_call(
        _expert_kernel,
        grid_spec=pltpu.PrefetchScalarGridSpec(
            num_scalar_prefetch=1,
            grid=(n_tiles,),
            in_specs=[pl.BlockSpec((tm * slab, LANES), x_idx), any_spec, any_spec, any_spec],
            out_specs=pl.BlockSpec((tm * slab, LANES), lambda t, tiles: (t, 0)),
            scratch_shapes=[pltpu.VMEM((2, d, f), jnp.float32),
                            pltpu.VMEM((2, d, f), jnp.float32),
                            pltpu.VMEM((2, f, d), jnp.float32),
                            pltpu.VMEM((d, f), jnp.bfloat16),
                            pltpu.VMEM((d, f), jnp.bfloat16),
                            pltpu.VMEM((f, d), jnp.bfloat16),
                            pltpu.SMEM((1,), jnp.int32),
                            pltpu.SemaphoreType.DMA((2, 3))]),
        out_shape=jax.ShapeDtypeStruct((n_tiles * tm * slab, LANES), jnp.float32),
        compiler_params=_cparams(("arbitrary",)),
        name="experts",
    )(tiles, xs2, w1, w3, w2)


def _combine_kernel(dest_ref, ys_ref, x1_ref, rw_ref, gt_ref, g2_ref, b2_ref, o_ref, ybuf0, ybuf1, sem,
                    *, stride, alpha):
    i = pl.program_id(0)
    tm = x1_ref.shape[0]
    slab = ys_ref.shape[1]
    ybufs = (ybuf0, ybuf1)

    def row_copy(slot, buf, k, r):
        return pltpu.make_async_copy(ys_ref.at[slot], ybufs[buf].at[k, pl.ds(r * slab, slab), :], sem.at[buf])

    def gather(step, buf):
        for r in range(tm):
            for k in range(TOPK_IN_GROUP):
                row_copy(dest_ref[k * stride + step * tm + r], buf, k, r).start()

    def gather_wait(buf):
        def one(r, c):
            for k in range(TOPK_IN_GROUP):
                row_copy(0, buf, k, r).wait()
            return c

        lax.fori_loop(0, tm, one, 0, unroll=8)

    @pl.when(i == 0)
    def _():
        gather(0, 0)

    for buf in range(2):
        @pl.when(i % 2 == buf)
        def _(buf=buf):
            gather_wait(buf)
            gather(i + 1, 1 - buf)
            rw = rw_ref[...]
            f = (rw[:, 0:1] * _load_slabs(ybufs[buf].at[0], tm)
                 + rw[:, 1:2] * _load_slabs(ybufs[buf].at[1], tm))
            y = _layer_norm_rows(alpha * x1_ref[...] + (1.0 + gt_ref[0]) * f)
            o_ref[...] = y * g2_ref[...] + b2_ref[...]

        @pl.when((i == pl.num_programs(0) - 1) & (i % 2 == buf))
        def _(buf=buf):
            gather_wait(1 - buf)


def _combine(dest, ys, x1, rw, gt2, g2, b2, seq, alpha):
    n, d = x1.shape
    tm = ROW_TM
    per_batch = seq // tm
    dest_pad = jnp.concatenate([dest.reshape(TOPK_IN_GROUP, n), jnp.zeros((TOPK_IN_GROUP, tm), jnp.int32)],
                               axis=1).reshape(TOPK_IN_GROUP * (n + tm))
    ybuf = pltpu.VMEM((TOPK_IN_GROUP, tm * (d // LANES), LANES), jnp.float32)
    return pl.pallas_call(
        functools.partial(_combine_kernel, stride=n + tm, alpha=alpha),
        grid_spec=pltpu.PrefetchScalarGridSpec(
            num_scalar_prefetch=1,
            grid=(n // tm,),
            in_specs=[pl.BlockSpec(memory_space=pl.ANY),
                      pl.BlockSpec((tm, d), lambda i, dest: (i, 0)),
                      pl.BlockSpec((tm, LANES), lambda i, dest: (i, 0)),
                      pl.BlockSpec((1, 1, d), lambda i, dest: (i // per_batch, 0, 0)),
                      pl.BlockSpec((1, d), lambda i, dest: (0, 0)),
                      pl.BlockSpec((1, d), lambda i, dest: (0, 0))],
            out_specs=pl.BlockSpec((tm, d), lambda i, dest: (i, 0)),
            scratch_shapes=[ybuf, ybuf, pltpu.SemaphoreType.DMA((2,))]),
        out_shape=jax.ShapeDtypeStruct((n, d), jnp.float32),
        compiler_params=_cparams(("arbitrary",)),
        name="combine",
    )(dest_pad, ys, x1, rw, gt2, g2, b2)


def kernel(x, c, positions, w_cond, b_cond, w_in, b_glu, w_dw, b_dw, g_cn, b_cn, w_conv_out, b_conv_out, w_attn_out, w_mix_out, g_ln1, b_ln1, w_grp, b_grp, w_erouter, b_erouter, w1, w3, w2, g_ln2, b_ln2):
    nb, seq, d = x.shape
    depth = w_cond.shape[0]
    n = nb * seq
    ch = w_dw.shape[2]
    aw = N_HEADS * HEAD_DIM
    in_cols = w_in.shape[2]
    alpha = (2.0 * depth) ** 0.25
    bf16 = jnp.bfloat16
    assert 2 * ch + 3 * aw + 2 * d == in_cols and ch == PROJ_TN and aw == PROJ_TN and d == 2 * PROJ_TN
    assert seq % MOBA_BLOCK == 0 and seq // MOBA_BLOCK <= LANES and N_GROUPS + N_EXPERTS <= LANES
    t_a, t_b, t_q, t_k, t_v, t_gc, t_ga = 0, 1, 2, 3, 4, 5, 7
    heads_per_tile = PROJ_TN // HEAD_DIM
    nblk = seq // MOBA_BLOCK
    n_tiles = (n * TOPK_IN_GROUP) // EXP_TM + N_EXPERTS
    assert (n * TOPK_IN_GROUP) % EXP_TM == 0 and n_tiles < LANES

    cos_t, sin_t = _rope_tables(positions)
    x2 = x.reshape(n, d)
    for l in range(depth):
        mod = _mod(c, w_cond[l], b_cond[l]).reshape(nb, 6, 1, d)
        sh1, sc1, gt1, sh2, sc2, gt2 = (mod[:, s] for s in range(6))

        bias = jnp.concatenate([b_glu[l], jnp.zeros((in_cols - 2 * ch,), jnp.float32)]).reshape(1, in_cols)
        z = _in_proj(x2, sc1, sh1, w_in[l].astype(bf16), bias, cos_t, sin_t, seq,
                     rope_tiles=(t_q, t_k + 1), sig_from=t_gc)
        km = _kmean(z, nb * nblk, t_k, aw).reshape(nb, nblk, aw)
        qx = _gate(z, km, nb, seq, t_q)
        o_attn = _attention(qx, z, nb, seq, t_k * heads_per_tile, t_v * heads_per_tile)
        hc = _conv(z, w_dw[l], b_dw[l], g_cn[l], b_cn[l], nb, seq, t_a, t_b)

        w_r = jnp.concatenate([w_grp[l], w_erouter[l],
                               jnp.zeros((d, LANES - N_GROUPS - N_EXPERTS), jnp.float32)], axis=1)
        b_r = jnp.concatenate([b_grp[l], b_erouter[l],
                               jnp.zeros((LANES - N_GROUPS - N_EXPERTS,), jnp.float32)]).reshape(1, LANES)
        x1, u2, ri, rw = _mix(hc, o_attn, z, x2, w_conv_out[l].astype(bf16), b_conv_out[l].reshape(1, d),
                              w_attn_out[l].astype(bf16), w_mix_out[l].astype(bf16), gt1,
                              g_ln1[l].reshape(1, d), b_ln1[l].reshape(1, d), sc2, sh2, w_r, b_r,
                              seq, t_gc, t_ga, alpha)

        eid = ri[:, :TOPK_IN_GROUP].T.reshape(TOPK_IN_GROUP * n)
        dest, tiles = _meta(eid, EXP_TM)
        f_in = w1.shape[-1]
        slab = d // LANES
        xs = _dispatch(dest, tiles, u2, n, n_tiles * EXP_TM, EXP_TM)
        ys = _experts(tiles, xs.reshape(n_tiles * EXP_TM * slab, LANES), w1[l].reshape(N_EXPERTS, d, f_in),
                      w3[l].reshape(N_EXPERTS, d, f_in), w2[l].reshape(N_EXPERTS, f_in, d), n_tiles)
        x2 = _combine(dest, ys.reshape(n_tiles * EXP_TM, slab, LANES), x1, rw, gt2, g_ln2[l].reshape(1, d), b_ln2[l].reshape(1, d), seq, alpha)
    return x2.reshape(nb, seq, d)
```

```python
import functools

import jax
import jax.numpy as jnp
from jax import lax
from jax.experimental import pallas as pl
from jax.experimental.pallas import tpu as pltpu

CONV_WIDTH = 31
N_HEADS = 8
HEAD_DIM = 128
ROPE_DIM = HEAD_DIM // 4
ROPE_THETA = 500000.0
MOBA_BLOCK = 256
MOBA_TOPK = 3
N_GROUPS = 4
EXPERTS_PER_GROUP = 8
N_EXPERTS = N_GROUPS * EXPERTS_PER_GROUP
TOPK_IN_GROUP = 2
LN_EPS = 1e-5
NEG_INF = -1e30

LANES = 128
SUBLANES = 8
VMEM_LIMIT = 56 * 1024 * 1024
ATTN_VMEM_LIMIT = 60 * 1024 * 1024

PROJ_TM = 1024
PROJ_TN = 1024
PROJ_CHUNK = 256
CONV_TT = 256
CONV_HALO = 32
ATTN_GROUP = 4
ATTN_HEADS = 4
ATTN_STRIP = 128
MIX_TM = 256
EXP_TM = 256
ROW_TM = 256
META_BLK = 1024


def _cparams(sem, vmem=VMEM_LIMIT):
    return pltpu.CompilerParams(dimension_semantics=sem, vmem_limit_bytes=vmem)


def _layer_norm_rows(x):
    mu = jnp.mean(x, axis=-1, keepdims=True)
    xc = x - mu
    var = jnp.mean(xc * xc, axis=-1, keepdims=True)
    return xc * lax.rsqrt(var + LN_EPS)


def _sigmoid(x):
    return 1.0 / (1.0 + jnp.exp(-x))


def _store_slabs(ref, val):
    m, w = val.shape
    c = w // LANES
    for k in range(c):
        ref[pl.ds(k, m, stride=c), :] = val[:, k * LANES:(k + 1) * LANES]


def _load_slabs(ref, m):
    c = ref.shape[0] // m
    return jnp.concatenate([ref[pl.ds(k, m, stride=c), :] for k in range(c)], axis=1)


def _dot_nt(a, b):
    return lax.dot_general(a, b, (((1,), (1,)), ((), ())), preferred_element_type=jnp.float32)


def _mod_kernel(cb_ref, w_ref, b_ref, o_ref, s_ref):
    @pl.when(pl.program_id(0) == 0)
    def _():
        cb = cb_ref[...]
        s_ref[...] = cb * _sigmoid(cb)

    nb, d, _ = cb_ref.shape
    tn = w_ref.shape[1]
    for cc in range(tn // LANES):
        sl = slice(cc * LANES, (cc + 1) * LANES)
        wc = w_ref[:, sl]
        for b in range(nb):
            p = (wc * s_ref[b]).reshape(d // SUBLANES, SUBLANES, LANES)
            r = jnp.sum(jnp.sum(p, axis=0), axis=0, keepdims=True)
            o_ref[b:b + 1, sl] = r + b_ref[:, sl]


def _mod(c, w_cond, b_cond, tn=1024):
    nb, d = c.shape
    n_out = w_cond.shape[1]
    cb = jnp.broadcast_to(c[:, :, None], (nb, d, LANES))
    return pl.pallas_call(
        _mod_kernel,
        grid=(n_out // tn,),
        in_specs=[pl.BlockSpec((nb, d, LANES), lambda j: (0, 0, 0)),
                  pl.BlockSpec((d, tn), lambda j: (0, j)),
                  pl.BlockSpec((1, tn), lambda j: (0, j))],
        out_specs=pl.BlockSpec((nb, tn), lambda j: (0, j)),
        out_shape=jax.ShapeDtypeStruct((nb, n_out), jnp.float32),
        scratch_shapes=[pltpu.VMEM((nb, d, LANES), jnp.float32)],
        compiler_params=_cparams(("arbitrary",)),
        name="mod",
    )(cb, w_cond, b_cond.reshape(1, n_out))


def _rope_tab_kernel(pos_ref, invf_ref, sign_ref, cos_ref, sin_ref):
    ang = pos_ref[...] * invf_ref[...]
    cos_ref[...] = jnp.cos(ang)
    sin_ref[...] = jnp.sin(ang) * sign_ref[...]


def _rope_tables(positions, tm=1024):
    n = positions.size
    half = ROPE_DIM // 2
    inv_freq = jnp.power(ROPE_THETA, -jnp.arange(half, dtype=jnp.float32) / half)
    invf = jnp.concatenate([inv_freq, inv_freq, jnp.zeros((LANES - ROPE_DIM,), jnp.float32)]).reshape(1, LANES)
    sign = jnp.concatenate([-jnp.ones((half,), jnp.float32),
                            jnp.ones((LANES - half,), jnp.float32)]).reshape(1, LANES)
    posb = jnp.broadcast_to(positions.astype(jnp.float32).reshape(n, 1), (n, LANES))
    vec = pl.BlockSpec((1, LANES), lambda i: (0, 0))
    tab = pl.BlockSpec((tm, LANES), lambda i: (i, 0))
    return pl.pallas_call(
        _rope_tab_kernel,
        grid=(n // tm,),
        in_specs=[tab, vec, vec],
        out_specs=[tab, tab],
        out_shape=[jax.ShapeDtypeStruct((n, LANES), jnp.float32)] * 2,
        compiler_params=_cparams(("parallel",)),
        name="rope_tab",
    )(posb, invf, sign)


def _in_proj_kernel(x_ref, sc_ref, sh_ref, w_ref, bias_ref, cos_ref, sin_ref, o_ref, km_ref, u_ref,
                    *, rope_tiles, sig_from):
    j = pl.program_id(1)

    @pl.when(j == 0)
    def _():
        xn = _layer_norm_rows(x_ref[...])
        u_ref[...] = (xn * (1.0 + sc_ref[0]) + sh_ref[0]).astype(u_ref.dtype)

    is_rope = (j >= rope_tiles[0]) & (j < rope_tiles[1])

    def project(epilogue, block_means=False):
        for n in range(w_ref.shape[1] // PROJ_CHUNK):
            cs = slice(n * PROJ_CHUNK, (n + 1) * PROJ_CHUNK)
            acc = jnp.dot(u_ref[...], w_ref[:, cs], preferred_element_type=jnp.float32) + bias_ref[:, cs]
            out = epilogue(acc).astype(o_ref.dtype)
            o_ref[:, cs] = out
            if block_means:
                stored = out.astype(jnp.float32)
                for blk in range(km_ref.shape[0]):
                    rows = stored[blk * MOBA_BLOCK:(blk + 1) * MOBA_BLOCK]
                    km_ref[blk, :, cs] = (jnp.sum(rows, axis=0, keepdims=True)
                                          * (1.0 / MOBA_BLOCK)).astype(km_ref.dtype)

    @pl.when(is_rope)
    def _():
        cosf = cos_ref[...]
        sinf = sin_ref[...]
        lane = lax.broadcasted_iota(jnp.int32, cosf.shape, 1)
        first = lane < (ROPE_DIM // 2)

        def rope(acc):
            heads = []
            for h in range(acc.shape[1] // HEAD_DIM):
                xh = acc[:, h * HEAD_DIM:(h + 1) * HEAD_DIM]
                partner = jnp.where(first, pltpu.roll(xh, HEAD_DIM - ROPE_DIM // 2, 1),
                                    pltpu.roll(xh, ROPE_DIM // 2, 1))
                heads.append(xh * cosf + partner * sinf)
            return jnp.concatenate(heads, axis=1)

        project(rope, block_means=True)

    @pl.when(j >= sig_from)
    def _():
        project(_sigmoid)

    @pl.when(jnp.logical_not(is_rope) & (j < sig_from))
    def _():
        project(lambda acc: acc)


def _in_proj(x2, sc1, sh1, w_bf, bias, cos_t, sin_t, seq, rope_tiles, sig_from):
    n, d = x2.shape
    cols = w_bf.shape[1]
    tm, tn = PROJ_TM, PROJ_TN
    per_batch = seq // tm
    kern = functools.partial(_in_proj_kernel, rope_tiles=rope_tiles, sig_from=sig_from)
    return pl.pallas_call(
        kern,
        grid=(n // tm, cols // tn),
        in_specs=[pl.BlockSpec((tm, d), lambda i, j: (i, 0)),
                  pl.BlockSpec((1, 1, d), lambda i, j: (i // per_batch, 0, 0)),
                  pl.BlockSpec((1, 1, d), lambda i, j: (i // per_batch, 0, 0)),
                  pl.BlockSpec((d, tn), lambda i, j: (0, j)),
                  pl.BlockSpec((1, tn), lambda i, j: (0, j)),
                  pl.BlockSpec((tm, LANES), lambda i, j: (i, 0)),
                  pl.BlockSpec((tm, LANES), lambda i, j: (i, 0))],
        out_specs=[pl.BlockSpec((tm, tn), lambda i, j: (i, j)),
                   pl.BlockSpec((tm // MOBA_BLOCK, 1, tn), lambda i, j: (i, 0, 0))],
        out_shape=[jax.ShapeDtypeStruct((n, cols), jnp.bfloat16),
                   jax.ShapeDtypeStruct((n // MOBA_BLOCK, 1, tn), jnp.bfloat16)],
        scratch_shapes=[pltpu.VMEM((tm, d), jnp.bfloat16)],
        compiler_params=_cparams(("parallel", "arbitrary")),
        name="in_proj",
    )(x2, sc1, sh1, w_bf, bias, cos_t, sin_t)


def _gate_kernel(q_ref, km_ref, qx_ref):
    i = pl.program_id(1)
    bs = q_ref.shape[0]
    nblk = km_ref.shape[1]
    sub = lax.broadcasted_iota(jnp.int32, (nblk, bs), 0)
    past = sub < i
    for h in range(N_HEADS):
        hs = slice(h * HEAD_DIM, (h + 1) * HEAD_DIM)
        q = q_ref[:, hs]
        g = jnp.where(past, _dot_nt(km_ref[0, :, hs], q), -jnp.inf)
        attend = sub == i
        for _ in range(MOBA_TOPK):
            m = jnp.max(g, axis=0, keepdims=True)
            idx = jnp.min(jnp.where(g == m, sub, nblk), axis=0, keepdims=True)
            hit = sub == idx
            attend = attend | (hit & past)
            g = jnp.where(hit, -jnp.inf, g)
        att = jnp.concatenate([jnp.where(attend, 1.0, 0.0), jnp.ones((LANES - nblk, bs), jnp.float32)], axis=0)
        bias = jnp.where(att.T > 0.5, 0.0, NEG_INF)
        qx_ref[:, 2 * h * HEAD_DIM:(2 * h + 1) * HEAD_DIM] = q
        qx_ref[:, (2 * h + 1) * HEAD_DIM:(2 * h + 2) * HEAD_DIM] = bias.astype(qx_ref.dtype)


def _gate(z, kmean, nb, seq, q_tile):
    n = z.shape[0]
    nq = seq // MOBA_BLOCK
    aw = N_HEADS * HEAD_DIM
    return pl.pallas_call(
        _gate_kernel,
        grid=(nb, nq),
        in_specs=[pl.BlockSpec((MOBA_BLOCK, aw), lambda b, i: (b * nq + i, q_tile)),
                  pl.BlockSpec((1, nq, aw), lambda b, i: (b, 0, 0))],
        out_specs=pl.BlockSpec((MOBA_BLOCK, 2 * aw), lambda b, i: (b * nq + i, 0)),
        out_shape=jax.ShapeDtypeStruct((n, 2 * aw), jnp.bfloat16),
        compiler_params=_cparams(("parallel", "arbitrary")),
        name="gate",
    )(z, kmean)


def _attn_kernel(qx_ref, k_ref, v_ref, o_ref, kx_ref, vx_ref, s_ref, p_ref):
    i = pl.program_id(2)
    bs = MOBA_BLOCK
    nblk = k_ref.shape[0] // bs
    wide = HEAD_DIM + LANES
    bf16 = jnp.bfloat16
    f32 = jnp.float32
    exp2_scale = (HEAD_DIM ** -0.5) * 1.4426950408889634
    grp = ATTN_GROUP * bs

    @pl.when(i == 0)
    def _():
        sub_k = lax.broadcasted_iota(jnp.int32, (LANES, bs), 0)
        for hh in range(ATTN_HEADS):
            hs = slice(hh * HEAD_DIM, (hh + 1) * HEAD_DIM)
            vx_ref[hh, :, :HEAD_DIM] = v_ref[:, hs]
            vx_ref[hh, :, HEAD_DIM:] = jnp.ones((nblk * bs, LANES), bf16)
            for jb in range(nblk):
                cs = slice(jb * bs, (jb + 1) * bs)
                kx_ref[hh, :HEAD_DIM, cs] = k_ref[cs, hs].astype(f32).T.astype(bf16)
                kx_ref[hh, HEAD_DIM:, cs] = jnp.where(sub_k == jb, 1.0, 0.0).astype(bf16)

    for c in range(nblk // ATTN_GROUP):
        @pl.when(i // ATTN_GROUP == c)
        def _(c=c):
            nk = (c + 1) * grp
            row = lax.broadcasted_iota(jnp.int32, (ATTN_STRIP, LANES), 0) + i * bs
            col = lax.broadcasted_iota(jnp.int32, (ATTN_STRIP, LANES), 1)
            for hh in range(ATTN_HEADS):
                s_ref[hh, :, :nk] = jnp.dot(qx_ref[:, hh * wide:(hh + 1) * wide], kx_ref[hh, :, 0:nk],
                                            preferred_element_type=f32)
            for hh in range(ATTN_HEADS):
                for r0 in range(0, bs, ATTN_STRIP):
                    rs = slice(r0, r0 + ATTN_STRIP)
                    mx = None
                    for t in range(nk // LANES):
                        ts = slice(t * LANES, (t + 1) * LANES)
                        st = s_ref[hh, rs, ts]
                        if t * LANES >= c * grp:
                            st = jnp.where(col + t * LANES <= row + r0, st, NEG_INF)
                            s_ref[hh, rs, ts] = st
                        mx = st if mx is None else jnp.maximum(mx, st)
                    mb = jnp.broadcast_to(jnp.max(mx, axis=-1, keepdims=True) * exp2_scale, (ATTN_STRIP, LANES))
                    for t in range(nk // LANES):
                        ts = slice(t * LANES, (t + 1) * LANES)
                        p_ref[hh, rs, ts] = jnp.exp2(s_ref[hh, rs, ts] * exp2_scale - mb).astype(bf16)
            for hh in range(ATTN_HEADS):
                acc = jnp.dot(p_ref[hh, :, :nk], vx_ref[hh, 0:nk, :], preferred_element_type=f32)
                o_ref[:, hh * HEAD_DIM:(hh + 1) * HEAD_DIM] = (
                    acc[:, :HEAD_DIM] / acc[:, HEAD_DIM:]).astype(o_ref.dtype)


def _attention(qx, z, nb, seq, k_col, v_col):
    n = z.shape[0]
    nq = seq // MOBA_BLOCK
    bs = MOBA_BLOCK
    wide = HEAD_DIM + LANES
    hps = ATTN_HEADS
    assert N_HEADS % hps == 0 and k_col % hps == 0 and v_col % hps == 0
    return pl.pallas_call(
        _attn_kernel,
        grid=(nb, N_HEADS // hps, nq),
        in_specs=[pl.BlockSpec((bs, hps * wide), lambda b, h, i: (b * nq + i, h)),
                  pl.BlockSpec((seq, hps * HEAD_DIM), lambda b, h, i: (b, k_col // hps + h),
                               pipeline_mode=pl.Buffered(1)),
                  pl.BlockSpec((seq, hps * HEAD_DIM), lambda b, h, i: (b, v_col // hps + h),
                               pipeline_mode=pl.Buffered(1))],
        out_specs=pl.BlockSpec((bs, hps * HEAD_DIM), lambda b, h, i: (b * nq + i, h)),
        out_shape=jax.ShapeDtypeStruct((n, N_HEADS * HEAD_DIM), jnp.bfloat16),
        scratch_shapes=[pltpu.VMEM((hps, wide, seq), jnp.bfloat16),
                        pltpu.VMEM((hps, seq, wide), jnp.bfloat16),
                        pltpu.VMEM((hps, bs, seq), jnp.float32),
                        pltpu.VMEM((hps, bs, seq), jnp.bfloat16)],
        compiler_params=_cparams(("parallel", "parallel", "arbitrary"), vmem=ATTN_VMEM_LIMIT),
        name="attn",
    )(qx, z, z)


def _conv_kernel(a_ref, b_ref, ha_ref, hb_ref, w_ref, bdw_ref, g_ref, bcn_ref, o_ref, hext_ref, grp_ref):
    i = pl.program_id(1)
    tt = a_ref.shape[0]
    halo = ha_ref.shape[0]
    ch = a_ref.shape[1]
    glu_h = ha_ref[...].astype(jnp.float32) * _sigmoid(hb_ref[...].astype(jnp.float32))
    hext_ref[0:halo, :] = jnp.where(i > 0, glu_h, 0.0)
    hext_ref[halo:halo + tt, :] = a_ref[...].astype(jnp.float32) * _sigmoid(b_ref[...].astype(jnp.float32))
    hext_ref[halo + tt:, :] = jnp.zeros((SUBLANES, ch), jnp.float32)

    first = halo - (CONV_WIDTH - 1)
    acc = jnp.zeros(o_ref.shape, jnp.float32) + bdw_ref[...]
    for s in range(SUBLANES):
        g = None
        for j in range(CONV_WIDTH):
            off = first + j
            if off % SUBLANES != s:
                continue
            term = w_ref[j:j + 1, :] * hext_ref[pl.ds(off - s, tt + SUBLANES), :]
            g = term if g is None else g + term
        if g is None:
            continue
        if s == 0:
            acc = acc + g[:tt]
        else:
            grp_ref[s] = g
            acc = acc + grp_ref[s, pl.ds(s, tt), :]
    y = _layer_norm_rows(acc) * g_ref[...] + bcn_ref[...]
    o_ref[...] = (y * _sigmoid(y)).astype(o_ref.dtype)


def _conv(z, w_dw, b_dw, g_cn, b_cn, nb, seq, a_tile, b_tile):
    n = z.shape[0]
    ch = w_dw.shape[1]
    tt, halo = CONV_TT, CONV_HALO
    nt = seq // tt
    r = tt // halo

    def cur(tile):
        return pl.BlockSpec((tt, ch), lambda b, i: (b * nt + i, tile))

    def prev(tile):
        return pl.BlockSpec((halo, ch), lambda b, i: (jnp.maximum((b * nt + i) * r - 1, 0), tile))

    vec = pl.BlockSpec((1, ch), lambda b, i: (0, 0))
    return pl.pallas_call(
        _conv_kernel,
        grid=(nb, nt),
        in_specs=[cur(a_tile), cur(b_tile), prev(a_tile), prev(b_tile),
                  pl.BlockSpec((CONV_WIDTH, ch), lambda b, i: (0, 0)), vec, vec, vec],
        out_specs=pl.BlockSpec((tt, ch), lambda b, i: (b * nt + i, 0)),
        out_shape=jax.ShapeDtypeStruct((n, ch), jnp.bfloat16),
        scratch_shapes=[pltpu.VMEM((halo + tt + SUBLANES, ch), jnp.float32),
                        pltpu.VMEM((SUBLANES, tt + SUBLANES, ch), jnp.float32)],
        compiler_params=_cparams(("parallel", "parallel")),
        name="conv",
    )(z, z, z, z, w_dw, b_dw.reshape(1, ch), g_cn.reshape(1, ch), b_cn.reshape(1, ch))


def _mix_kernel(hc_ref, o_ref, sgc0_ref, sgc1_ref, sga0_ref, sga1_ref, x_ref, wco_ref, bco_ref, wao_ref, wmx_ref,
                gt_ref, g1_ref, b1_ref, sc_ref, sh_ref, wr_ref, br_ref,
                x1_ref, u2_ref, ri_ref, rw_ref, *, alpha):
    f32 = jnp.float32
    y_conv = jnp.dot(hc_ref[...], wco_ref[...], preferred_element_type=f32) + bco_ref[...]
    y_attn = jnp.dot(o_ref[...], wao_ref[...], preferred_element_type=f32)
    sgc = jnp.concatenate([sgc0_ref[...], sgc1_ref[...]], axis=1).astype(f32)
    sga = jnp.concatenate([sga0_ref[...], sga1_ref[...]], axis=1).astype(f32)
    merged = sgc * y_conv + sga * y_attn
    t_out = jnp.dot(merged.astype(jnp.bfloat16), wmx_ref[...], preferred_element_type=f32)
    x1 = _layer_norm_rows(alpha * x_ref[...] + (1.0 + gt_ref[0]) * t_out) * g1_ref[...] + b1_ref[...]
    x1_ref[...] = x1
    u2 = _layer_norm_rows(x1) * (1.0 + sc_ref[0]) + sh_ref[0]
    _store_slabs(u2_ref, u2)

    bf16 = jnp.bfloat16
    u_hi = u2.astype(bf16)
    u_lo = (u2 - u_hi.astype(f32)).astype(bf16)
    w_hi = wr_ref[...].astype(bf16)
    w_lo = (wr_ref[...] - w_hi.astype(f32)).astype(bf16)
    logits = jnp.dot(jnp.concatenate([u_hi, u_lo, u_hi], axis=1), jnp.concatenate([w_hi, w_hi, w_lo], axis=0),
                     preferred_element_type=f32) + br_ref[...]
    lane = lax.broadcasted_iota(jnp.int32, logits.shape, 1)
    is_grp = lane < N_GROUPS
    gl = jnp.where(is_grp, logits, -jnp.inf)
    gmax = jnp.max(gl, axis=-1, keepdims=True)
    gidx = jnp.min(jnp.where(gl == gmax, lane, LANES), axis=-1, keepdims=True)
    p_top = 1.0 / jnp.sum(jnp.where(is_grp, jnp.exp(logits - gmax), 0.0), axis=-1, keepdims=True)
    lo = N_GROUPS + gidx * EXPERTS_PER_GROUP
    el = jnp.where((lane >= lo) & (lane < lo + EXPERTS_PER_GROUP), logits, -jnp.inf)
    v1 = jnp.max(el, axis=-1, keepdims=True)
    i1 = jnp.min(jnp.where(el == v1, lane, LANES), axis=-1, keepdims=True)
    el = jnp.where(lane == i1, -jnp.inf, el)
    v2 = jnp.max(el, axis=-1, keepdims=True)
    i2 = jnp.min(jnp.where(el == v2, lane, LANES), axis=-1, keepdims=True)
    e2 = jnp.exp(v2 - v1)
    w1 = p_top / (1.0 + e2)
    w2 = p_top * e2 / (1.0 + e2)
    ri_ref[...] = jnp.where(lane == 0, i1 - N_GROUPS, jnp.where(lane == 1, i2 - N_GROUPS, 0))
    rw_ref[...] = jnp.where(lane == 0, w1, jnp.where(lane == 1, w2, 0.0))


def _mix(hc, o_attn, z, x2, wco, bco, wao, wmx, gt1, g1, b1, sc2, sh2, w_r, b_r, seq, gc_tile, ga_tile, alpha):
    n, d = x2.shape
    ch = hc.shape[1]
    tm = MIX_TM
    per_batch = seq // tm

    def rows(width):
        return pl.BlockSpec((tm, width), lambda i: (i, 0))

    def const(shape):
        return pl.BlockSpec(shape, lambda i: (0,) * len(shape), pipeline_mode=pl.Buffered(1))

    def per_b():
        return pl.BlockSpec((1, 1, d), lambda i: (i // per_batch, 0, 0))

    return pl.pallas_call(
        functools.partial(_mix_kernel, alpha=alpha),
        grid=(n // tm,),
        in_specs=[rows(ch), rows(ch),
                  pl.BlockSpec((tm, ch), lambda i: (i, gc_tile)),
                  pl.BlockSpec((tm, ch), lambda i: (i, gc_tile + 1)),
                  pl.BlockSpec((tm, ch), lambda i: (i, ga_tile)),
                  pl.BlockSpec((tm, ch), lambda i: (i, ga_tile + 1)),
                  rows(d),
                  const((ch, d)), const((1, d)), const((ch, d)), const((d, d)),
                  per_b(), const((1, d)), const((1, d)), per_b(), per_b(),
                  const((d, LANES)), const((1, LANES))],
        out_specs=[rows(d), pl.BlockSpec((tm * (d // LANES), LANES), lambda i: (i, 0)), rows(LANES), rows(LANES)],
        out_shape=[jax.ShapeDtypeStruct((n, d), jnp.float32),
                   jax.ShapeDtypeStruct((n * (d // LANES), LANES), jnp.float32),
                   jax.ShapeDtypeStruct((n, LANES), jnp.int32),
                   jax.ShapeDtypeStruct((n, LANES), jnp.float32)],
        compiler_params=_cparams(("parallel",)),
        name="mix",
    )(hc, o_attn, z, z, z, z, x2, wco, bco, wao, wmx, gt1, g1, b1, sc2, sh2, w_r, b_r)


def _meta_kernel(eid_ref, dest_ref, tile_ref, cnt_ref, base_ref, *, tile_rows):
    ph = pl.program_id(0)
    r = pl.program_id(1)
    f32 = jnp.float32
    blk = eid_ref.shape[2]
    e = eid_ref[0]
    sub = lax.broadcasted_iota(jnp.int32, (N_EXPERTS, blk), 0)
    oh = (sub == e).astype(f32)
    blk_cnt = jnp.broadcast_to(jnp.sum(oh, axis=1, keepdims=True), (N_EXPERTS, LANES))

    @pl.when((ph == 0) & (r == 0))
    def _():
        cnt_ref[...] = jnp.zeros_like(cnt_ref)

    @pl.when(ph == 0)
    def _():
        cnt_ref[...] += blk_cnt

    @pl.when((ph == 1) & (r == 0))
    def _():
        padded = jnp.ceil(cnt_ref[...] * (1.0 / tile_rows)) * tile_rows
        er = lax.broadcasted_iota(jnp.int32, (N_EXPERTS, N_EXPERTS), 0)
        ec = lax.broadcasted_iota(jnp.int32, (N_EXPERTS, N_EXPERTS), 1)
        starts = jnp.dot((ec < er).astype(f32), padded, preferred_element_type=f32,
                         precision=lax.Precision.HIGHEST)
        base_ref[...] = starts
        ends = starts + padded
        lane = lax.broadcasted_iota(jnp.int32, (N_EXPERTS, LANES), 1)
        tile_start = (lane * tile_rows).astype(f32)
        owner = jnp.sum((ends <= tile_start).astype(f32), axis=0, keepdims=True)
        total_tiles = jnp.max(ends, axis=0, keepdims=True) * (1.0 / tile_rows)
        lane1 = lax.broadcasted_iota(jnp.int32, (1, LANES), 1)
        tile_ref[0:1, :] = jnp.where(lane1 == LANES - 1, total_tiles, owner).astype(jnp.int32)
        sub_e = lax.broadcasted_iota(jnp.int32, (N_EXPERTS, LANES), 0)
        diag = sub_e == lane
        tile_ref[1:2, :] = jnp.sum(jnp.where(diag, starts + cnt_ref[...], 0.0), axis=0,
                                   keepdims=True).astype(jnp.int32)
        tile_ref[2:3, :] = jnp.sum(jnp.where(diag, ends, 0.0), axis=0, keepdims=True).astype(jnp.int32)

    @pl.when(ph == 1)
    def _():
        ar = lax.broadcasted_iota(jnp.int32, (blk, blk), 0)
        ac = lax.broadcasted_iota(jnp.int32, (blk, blk), 1)
        before = (ar < ac).astype(jnp.bfloat16)
        cum = jnp.dot(oh.astype(jnp.bfloat16), before, preferred_element_type=f32)
        slot = jnp.sum(oh * (base_ref[:, 0:1] + cum), axis=0, keepdims=True)
        dest_ref[0] = slot.astype(jnp.int32)
        base_ref[...] += blk_cnt


def _meta(eid_flat, tile_rows):
    na = eid_flat.shape[0]
    nblk = na // META_BLK
    eid3 = eid_flat.reshape(nblk, 1, META_BLK)
    dest, tiles = pl.pallas_call(
        functools.partial(_meta_kernel, tile_rows=tile_rows),
        grid=(2, nblk),
        in_specs=[pl.BlockSpec((1, 1, META_BLK), lambda ph, r: (r, 0, 0))],
        out_specs=[pl.BlockSpec((1, 1, META_BLK), lambda ph, r: (ph * r, 0, 0)),
                   pl.BlockSpec((3, LANES), lambda ph, r: (0, 0))],
        out_shape=[jax.ShapeDtypeStruct((nblk, 1, META_BLK), jnp.int32),
                   jax.ShapeDtypeStruct((3, LANES), jnp.int32)],
        scratch_shapes=[pltpu.VMEM((N_EXPERTS, LANES), jnp.float32),
                        pltpu.VMEM((N_EXPERTS, LANES), jnp.float32)],
        compiler_params=_cparams(("arbitrary", "arbitrary")),
        name="meta",
    )(eid3)
    return dest.reshape(na), tiles.reshape(3 * LANES)


def _dispatch_kernel(dest_ref, seg_ref, u_ref, xs_ref, zero_ref, sem, zsem, *, n_tok, tile_rows):
    i = pl.program_id(0)
    slab = xs_ref.shape[1]
    tm = u_ref.shape[0] // slab
    base = i * tm

    @pl.when(i == 0)
    def _():
        zero_ref[...] = jnp.zeros_like(zero_ref)

        def fills(e, act):
            lo = seg_ref[LANES + e]
            pad = seg_ref[2 * LANES + e] - lo
            pos = lo
            for bit in reversed(range(tile_rows.bit_length() - 1)):
                rows = 1 << bit
                take = (pad >> bit) & 1

                @pl.when(take == 1)
                def _(pos=pos, rows=rows):
                    act(pltpu.make_async_copy(zero_ref.at[pl.ds(0, rows)], xs_ref.at[pl.ds(pos, rows)], zsem))

                pos = pos + take * rows

        def start_fills(e, c):
            fills(e, lambda cp: cp.start())
            return c

        def wait_fills(e, c):
            fills(e, lambda cp: cp.wait())
            return c

        run = zero_ref.shape[0]
        used = seg_ref[2 * LANES + N_EXPERTS - 1]
        n_runs = (xs_ref.shape[0] - used) // run

        def tail_fill(j):
            return pltpu.make_async_copy(zero_ref, xs_ref.at[pl.ds(used + j * run, run)], zsem)

        def start_tail(j, c):
            tail_fill(j).start()
            return c

        def wait_tail(j, c):
            tail_fill(j).wait()
            return c

        lax.fori_loop(0, N_EXPERTS, start_fills, 0)
        lax.fori_loop(0, n_runs, start_tail, 0)
        lax.fori_loop(0, N_EXPERTS, wait_fills, 0)
        lax.fori_loop(0, n_runs, wait_tail, 0)

    def row_copy(r, k):
        d = dest_ref[k * n_tok + base + r]
        return pltpu.make_async_copy(u_ref.at[pl.ds(r * slab, slab), :], xs_ref.at[d], sem)

    for r in range(tm):
        for k in range(TOPK_IN_GROUP):
            row_copy(r, k).start()

    def drain(r, c):
        for k in range(TOPK_IN_GROUP):
            row_copy(r, k).wait()
        return c

    lax.fori_loop(0, tm, drain, 0, unroll=8)


def _dispatch(dest, tiles, u2s, n_tok, n_slots, tile_rows):
    slab = u2s.shape[0] // n_tok
    tm = ROW_TM
    assert tile_rows & (tile_rows - 1) == 0
    return pl.pallas_call(
        functools.partial(_dispatch_kernel, n_tok=n_tok, tile_rows=tile_rows),
        grid_spec=pltpu.PrefetchScalarGridSpec(
            num_scalar_prefetch=2,
            grid=(n_tok // tm,),
            in_specs=[pl.BlockSpec((tm * slab, LANES), lambda i, dest, seg: (i, 0))],
            out_specs=pl.BlockSpec(memory_space=pl.ANY),
            scratch_shapes=[pltpu.VMEM((tile_rows // 2, slab, LANES), jnp.float32),
                            pltpu.SemaphoreType.DMA(()),
                            pltpu.SemaphoreType.DMA(())]),
        out_shape=jax.ShapeDtypeStruct((n_slots, slab, LANES), jnp.float32),
        compiler_params=_cparams(("arbitrary",)),
        name="dispatch",
    )(dest, tiles, u2s)


def _expert_kernel(tile_ref, x_ref, w1_ref, w3_ref, w2_ref, y_ref, wf1, wf3, wf2, w1b, w3b, w2b, wslot_ref, wsem):
    t = pl.program_id(0)
    tm = EXP_TM
    bf16 = jnp.bfloat16
    n_used = tile_ref[LANES - 1]
    e = tile_ref[t]
    changed = (t == 0) | (e != tile_ref[jnp.maximum(t - 1, 0)])
    active = t < n_used

    def weight_copies(ex, ws):
        return (pltpu.make_async_copy(w1_ref.at[ex], wf1.at[ws], wsem.at[ws, 0]),
                pltpu.make_async_copy(w3_ref.at[ex], wf3.at[ws], wsem.at[ws, 1]),
                pltpu.make_async_copy(w2_ref.at[ex], wf2.at[ws], wsem.at[ws, 2]))

    @pl.when(t == 0)
    def _():
        for cp in weight_copies(e, 0):
            cp.start()
        wslot_ref[0] = 0

    @pl.when(active & changed)
    def _():
        ws = wslot_ref[0]
        for cp in weight_copies(e, ws):
            cp.wait()
        nxt = lax.while_loop(lambda t2: (t2 < n_used) & (tile_ref[t2] == e), lambda t2: t2 + 1, t + 1)

        @pl.when(nxt < n_used)
        def _():
            for cp in weight_copies(tile_ref[nxt], 1 - ws):
                cp.start()

        w1b[...] = wf1[ws].astype(bf16)
        w3b[...] = wf3[ws].astype(bf16)
        w2b[...] = wf2[ws].astype(bf16)
        wslot_ref[0] = 1 - ws

    @pl.when(active)
    def _():
        xb = _load_slabs(x_ref, tm).astype(bf16)
        h1 = jnp.dot(xb, w1b[...], preferred_element_type=jnp.float32)
        h3 = jnp.dot(xb, w3b[...], preferred_element_type=jnp.float32)
        hh = (h1 * _sigmoid(h1) * h3).astype(bf16)
        _store_slabs(y_ref, jnp.dot(hh, w2b[...], preferred_element_type=jnp.float32))

    @pl.when(jnp.logical_not(active))
    def _():
        y_ref[...] = jnp.zeros_like(y_ref)


def _experts(tiles, xs2, w1, w3, w2, n_tiles):
    d, f = w1.shape[1], w1.shape[2]
    tm = EXP_TM
    slab = d // LANES
    any_spec = pl.BlockSpec(memory_space=pl.ANY)

    def x_idx(t, tiles):
        return (jnp.maximum(jnp.minimum(t, tiles[LANES - 1] - 1), 0), 0)

    return pl.pallas_call(
        _expert_kernel,
        grid_spec=pltpu.PrefetchScalarGridSpec(
            num_scalar_prefetch=1,
            grid=(n_tiles,),
            in_specs=[pl.BlockSpec((tm * slab, LANES), x_idx), any_spec, any_spec, any_spec],
            out_specs=pl.BlockSpec((tm * slab, LANES), lambda t, tiles: (t, 0)),
            scratch_shapes=[pltpu.VMEM((2, d, f), jnp.float32),
                            pltpu.VMEM((2, d, f), jnp.float32),
                            pltpu.VMEM((2, f, d), jnp.float32),
                            pltpu.VMEM((d, f), jnp.bfloat16),
                            pltpu.VMEM((d, f), jnp.bfloat16),
                            pltpu.VMEM((f, d), jnp.bfloat16),
                            pltpu.SMEM((1,), jnp.int32),
                            pltpu.SemaphoreType.DMA((2, 3))]),
        out_shape=jax.ShapeDtypeStruct((n_tiles * tm * slab, LANES), jnp.float32),
        compiler_params=_cparams(("arbitrary",)),
        name="experts",
    )(tiles, xs2, w1, w3, w2)


def _combine_kernel(dest_ref, ys_ref, x1_ref, rw_ref, gt_ref, g2_ref, b2_ref, o_ref, ybuf0, ybuf1, sem,
                    *, stride, alpha):
    i = pl.program_id(0)
    tm = x1_ref.shape[0]
    slab = ys_ref.shape[1]
    ybufs = (ybuf0, ybuf1)

    def row_copy(slot, buf, k, r):
        return pltpu.make_async_copy(ys_ref.at[slot], ybufs[buf].at[k, pl.ds(r * slab, slab), :], sem.at[buf])

    def gather(step, buf):
        for r in range(tm):
            for k in range(TOPK_IN_GROUP):
                row_copy(dest_ref[k * stride + step * tm + r], buf, k, r).start()

    def gather_wait(buf):
        def one(r, c):
            for k in range(TOPK_IN_GROUP):
                row_copy(0, buf, k, r).wait()
            return c

        lax.fori_loop(0, tm, one, 0, unroll=8)

    @pl.when(i == 0)
    def _():
        gather(0, 0)

    for buf in range(2):
        @pl.when(i % 2 == buf)
        def _(buf=buf):
            gather_wait(buf)
            gather(i + 1, 1 - buf)
            rw = rw_ref[...]
            f = (rw[:, 0:1] * _load_slabs(ybufs[buf].at[0], tm)
                 + rw[:, 1:2] * _load_slabs(ybufs[buf].at[1], tm))
            y = _layer_norm_rows(alpha * x1_ref[...] + (1.0 + gt_ref[0]) * f)
            o_ref[...] = y * g2_ref[...] + b2_ref[...]

        @pl.when((i == pl.num_programs(0) - 1) & (i % 2 == buf))
        def _(buf=buf):
            gather_wait(1 - buf)


def _combine(dest, ys, x1, rw, gt2, g2, b2, seq, alpha):
    n, d = x1.shape
    tm = ROW_TM
    per_batch = seq // tm
    dest_pad = jnp.concatenate([dest.reshape(TOPK_IN_GROUP, n), jnp.zeros((TOPK_IN_GROUP, tm), jnp.int32)],
                               axis=1).reshape(TOPK_IN_GROUP * (n + tm))
    ybuf = pltpu.VMEM((TOPK_IN_GROUP, tm * (d // LANES), LANES), jnp.float32)
    return pl.pallas_call(
        functools.partial(_combine_kernel, stride=n + tm, alpha=alpha),
        grid_spec=pltpu.PrefetchScalarGridSpec(
            num_scalar_prefetch=1,
            grid=(n // tm,),
            in_specs=[pl.BlockSpec(memory_space=pl.ANY),
                      pl.BlockSpec((tm, d), lambda i, dest: (i, 0)),
                      pl.BlockSpec((tm, LANES), lambda i, dest: (i, 0)),
                      pl.BlockSpec((1, 1, d), lambda i, dest: (i // per_batch, 0, 0)),
                      pl.BlockSpec((1, d), lambda i, dest: (0, 0)),
                      pl.BlockSpec((1, d), lambda i, dest: (0, 0))],
            out_specs=pl.BlockSpec((tm, d), lambda i, dest: (i, 0)),
            scratch_shapes=[ybuf, ybuf, pltpu.SemaphoreType.DMA((2,))]),
        out_shape=jax.ShapeDtypeStruct((n, d), jnp.float32),
        compiler_params=_cparams(("arbitrary",)),
        name="combine",
    )(dest_pad, ys, x1, rw, gt2, g2, b2)


def kernel(x, c, positions, w_cond, b_cond, w_in, b_glu, w_dw, b_dw, g_cn, b_cn, w_conv_out, b_conv_out, w_attn_out, w_mix_out, g_ln1, b_ln1, w_grp, b_grp, w_erouter, b_erouter, w1, w3, w2, g_ln2, b_ln2):
    nb, seq, d = x.shape
    depth = w_cond.shape[0]
    n = nb * seq
    ch = w_dw.shape[2]
    aw = N_HEADS * HEAD_DIM
    in_cols = w_in.shape[2]
    alpha = (2.0 * depth) ** 0.25
    bf16 = jnp.bfloat16
    assert 2 * ch + 3 * aw + 2 * d == in_cols and ch == PROJ_TN and aw == PROJ_TN and d == 2 * PROJ_TN
    assert seq % MOBA_BLOCK == 0 and seq // MOBA_BLOCK <= LANES and N_GROUPS + N_EXPERTS <= LANES
    t_a, t_b, t_q, t_k, t_v, t_gc, t_ga = 0, 1, 2, 3, 4, 5, 7
    heads_per_tile = PROJ_TN // HEAD_DIM
    nblk = seq // MOBA_BLOCK
    n_tiles = (n * TOPK_IN_GROUP) // EXP_TM + N_EXPERTS
    assert (n * TOPK_IN_GROUP) % EXP_TM == 0 and n_tiles < LANES

    cos_t, sin_t = _rope_tables(positions)
    x2 = x.reshape(n, d)
    for l in range(depth):
        mod = _mod(c, w_cond[l], b_cond[l]).reshape(nb, 6, 1, d)
        sh1, sc1, gt1, sh2, sc2, gt2 = (mod[:, s] for s in range(6))

        bias = jnp.concatenate([b_glu[l], jnp.zeros((in_cols - 2 * ch,), jnp.float32)]).reshape(1, in_cols)
        z, km = _in_proj(x2, sc1, sh1, w_in[l].astype(bf16), bias, cos_t, sin_t, seq,
                         rope_tiles=(t_q, t_k + 1), sig_from=t_gc)
        km = km.reshape(nb, nblk, aw)
        qx = _gate(z, km, nb, seq, t_q)
        o_attn = _attention(qx, z, nb, seq, t_k * heads_per_tile, t_v * heads_per_tile)
        hc = _conv(z, w_dw[l], b_dw[l], g_cn[l], b_cn[l], nb, seq, t_a, t_b)

        w_r = jnp.concatenate([w_grp[l], w_erouter[l],
                               jnp.zeros((d, LANES - N_GROUPS - N_EXPERTS), jnp.float32)], axis=1)
        b_r = jnp.concatenate([b_grp[l], b_erouter[l],
                               jnp.zeros((LANES - N_GROUPS - N_EXPERTS,), jnp.float32)]).reshape(1, LANES)
        x1, u2, ri, rw = _mix(hc, o_attn, z, x2, w_conv_out[l].astype(bf16), b_conv_out[l].reshape(1, d),
                              w_attn_out[l].astype(bf16), w_mix_out[l].astype(bf16), gt1,
                              g_ln1[l].reshape(1, d), b_ln1[l].reshape(1, d), sc2, sh2, w_r, b_r,
                              seq, t_gc, t_ga, alpha)

        eid = ri[:, :TOPK_IN_GROUP].T.reshape(TOPK_IN_GROUP * n)
        dest, tiles = _meta(eid, EXP_TM)
        f_in = w1.shape[-1]
        slab = d // LANES
        xs = _dispatch(dest, tiles, u2, n, n_tiles * EXP_TM, EXP_TM)
        ys = _experts(tiles, xs.reshape(n_tiles * EXP_TM * slab, LANES), w1[l].reshape(N_EXPERTS, d, f_in),
                      w3[l].reshape(N_EXPERTS, d, f_in), w2[l].reshape(N_EXPERTS, f_in, d), n_tiles)
        x2 = _combine(dest, ys.reshape(n_tiles * EXP_TM, slab, LANES), x1, rw, gt2, g_ln2[l].reshape(1, d), b_ln2[l].reshape(1, d), seq, alpha)
    return x2.reshape(nb, seq, d)
```

```python
import functools

import jax
import jax.numpy as jnp
from jax import lax
from jax.experimental import pallas as pl
from jax.experimental.pallas import tpu as pltpu

CONV_WIDTH = 31
N_HEADS = 8
HEAD_DIM = 128
ROPE_DIM = HEAD_DIM // 4
ROPE_THETA = 500000.0
MOBA_BLOCK = 256
MOBA_TOPK = 3
N_GROUPS = 4
EXPERTS_PER_GROUP = 8
N_EXPERTS = N_GROUPS * EXPERTS_PER_GROUP
TOPK_IN_GROUP = 2
LN_EPS = 1e-5
NEG_INF = -1e30

LANES = 128
SUBLANES = 8
DMA_PRIORITIES = 2
VMEM_LIMIT = 56 * 1024 * 1024
ATTN_VMEM_LIMIT = 60 * 1024 * 1024

PROJ_TM = 1024
PROJ_TN = 1024
PROJ_CHUNK = 256
CONV_TT = 256
CONV_HALO = 32
ATTN_GROUP = 4
ATTN_HEADS = 4
ATTN_STRIP = 128
MIX_TM = 256
EXP_TM = 256
ROW_TM = 256
META_BLK = 1024


def _cparams(sem, vmem=VMEM_LIMIT, flags=None):
    return pltpu.CompilerParams(dimension_semantics=sem, vmem_limit_bytes=vmem, flags=flags)


def _layer_norm_rows(x):
    mu = jnp.mean(x, axis=-1, keepdims=True)
    xc = x - mu
    var = jnp.mean(xc * xc, axis=-1, keepdims=True)
    return xc * lax.rsqrt(var + LN_EPS)


def _sigmoid(x):
    return 1.0 / (1.0 + jnp.exp(-x))


def _store_slabs(ref, val):
    m, w = val.shape
    c = w // LANES
    for k in range(c):
        ref[pl.ds(k, m, stride=c), :] = val[:, k * LANES:(k + 1) * LANES]


def _load_slabs(ref, m):
    c = ref.shape[0] // m
    return jnp.concatenate([ref[pl.ds(k, m, stride=c), :] for k in range(c)], axis=1)


def _dot_nt(a, b):
    return lax.dot_general(a, b, (((1,), (1,)), ((), ())), preferred_element_type=jnp.float32)


def _mod_kernel(cb_ref, w_ref, b_ref, o_ref, s_ref):
    @pl.when(pl.program_id(0) == 0)
    def _():
        cb = cb_ref[...]
        s_ref[...] = cb * _sigmoid(cb)

    nb, d, _ = cb_ref.shape
    tn = w_ref.shape[1]
    for cc in range(tn // LANES):
        sl = slice(cc * LANES, (cc + 1) * LANES)
        wc = w_ref[:, sl]
        for b in range(nb):
            p = (wc * s_ref[b]).reshape(d // SUBLANES, SUBLANES, LANES)
            r = jnp.sum(jnp.sum(p, axis=0), axis=0, keepdims=True)
            o_ref[b:b + 1, sl] = r + b_ref[:, sl]


def _mod(c, w_cond, b_cond, tn=1024):
    nb, d = c.shape
    n_out = w_cond.shape[1]
    cb = jnp.broadcast_to(c[:, :, None], (nb, d, LANES))
    return pl.pallas_call(
        _mod_kernel,
        grid=(n_out // tn,),
        in_specs=[pl.BlockSpec((nb, d, LANES), lambda j: (0, 0, 0)),
                  pl.BlockSpec((d, tn), lambda j: (0, j)),
                  pl.BlockSpec((1, tn), lambda j: (0, j))],
        out_specs=pl.BlockSpec((nb, tn), lambda j: (0, j)),
        out_shape=jax.ShapeDtypeStruct((nb, n_out), jnp.float32),
        scratch_shapes=[pltpu.VMEM((nb, d, LANES), jnp.float32)],
        compiler_params=_cparams(("arbitrary",)),
        name="mod",
    )(cb, w_cond, b_cond.reshape(1, n_out))


def _rope_tab_kernel(pos_ref, invf_ref, sign_ref, cos_ref, sin_ref):
    ang = pos_ref[...] * invf_ref[...]
    cos_ref[...] = jnp.cos(ang)
    sin_ref[...] = jnp.sin(ang) * sign_ref[...]


def _rope_tables(positions, tm=1024):
    n = positions.size
    half = ROPE_DIM // 2
    inv_freq = jnp.power(ROPE_THETA, -jnp.arange(half, dtype=jnp.float32) / half)
    invf = jnp.concatenate([inv_freq, inv_freq, jnp.zeros((LANES - ROPE_DIM,), jnp.float32)]).reshape(1, LANES)
    sign = jnp.concatenate([-jnp.ones((half,), jnp.float32),
                            jnp.ones((LANES - half,), jnp.float32)]).reshape(1, LANES)
    posb = jnp.broadcast_to(positions.astype(jnp.float32).reshape(n, 1), (n, LANES))
    vec = pl.BlockSpec((1, LANES), lambda i: (0, 0))
    tab = pl.BlockSpec((tm, LANES), lambda i: (i, 0))
    return pl.pallas_call(
        _rope_tab_kernel,
        grid=(n // tm,),
        in_specs=[tab, vec, vec],
        out_specs=[tab, tab],
        out_shape=[jax.ShapeDtypeStruct((n, LANES), jnp.float32)] * 2,
        compiler_params=_cparams(("parallel",)),
        name="rope_tab",
    )(posb, invf, sign)


def _in_proj_kernel(x_ref, sc_ref, sh_ref, w_ref, bias_ref, cos_ref, sin_ref, o_ref, km_ref, u_ref,
                    *, rope_tiles, sig_from):
    j = pl.program_id(1)

    @pl.when(j == 0)
    def _():
        xn = _layer_norm_rows(x_ref[...])
        u_ref[...] = (xn * (1.0 + sc_ref[0]) + sh_ref[0]).astype(u_ref.dtype)

    is_rope = (j >= rope_tiles[0]) & (j < rope_tiles[1])

    def project(epilogue, block_means=False):
        for n in range(w_ref.shape[1] // PROJ_CHUNK):
            cs = slice(n * PROJ_CHUNK, (n + 1) * PROJ_CHUNK)
            acc = jnp.dot(u_ref[...], w_ref[:, cs].astype(u_ref.dtype),
                          preferred_element_type=jnp.float32) + bias_ref[:, cs]
            out = epilogue(acc).astype(o_ref.dtype)
            o_ref[:, cs] = out
            if block_means:
                stored = out.astype(jnp.float32)
                for blk in range(km_ref.shape[0]):
                    rows = stored[blk * MOBA_BLOCK:(blk + 1) * MOBA_BLOCK]
                    km_ref[blk, :, cs] = (jnp.sum(rows, axis=0, keepdims=True)
                                          * (1.0 / MOBA_BLOCK)).astype(km_ref.dtype)

    @pl.when(is_rope)
    def _():
        cosf = cos_ref[...]
        sinf = sin_ref[...]
        lane = lax.broadcasted_iota(jnp.int32, cosf.shape, 1)
        first = lane < (ROPE_DIM // 2)

        def rope(acc):
            heads = []
            for h in range(acc.shape[1] // HEAD_DIM):
                xh = acc[:, h * HEAD_DIM:(h + 1) * HEAD_DIM]
                partner = jnp.where(first, pltpu.roll(xh, HEAD_DIM - ROPE_DIM // 2, 1),
                                    pltpu.roll(xh, ROPE_DIM // 2, 1))
                heads.append(xh * cosf + partner * sinf)
            return jnp.concatenate(heads, axis=1)

        project(rope, block_means=True)

    @pl.when(j >= sig_from)
    def _():
        project(_sigmoid)

    @pl.when(jnp.logical_not(is_rope) & (j < sig_from))
    def _():
        project(lambda acc: acc)


def _in_proj(x2, sc1, sh1, w_in, bias, cos_t, sin_t, seq, rope_tiles, sig_from):
    n, d = x2.shape
    cols = w_in.shape[1]
    tm, tn = PROJ_TM, PROJ_TN
    per_batch = seq // tm
    kern = functools.partial(_in_proj_kernel, rope_tiles=rope_tiles, sig_from=sig_from)
    return pl.pallas_call(
        kern,
        grid=(n // tm, cols // tn),
        in_specs=[pl.BlockSpec((tm, d), lambda i, j: (i, 0)),
                  pl.BlockSpec((1, 1, d), lambda i, j: (i // per_batch, 0, 0)),
                  pl.BlockSpec((1, 1, d), lambda i, j: (i // per_batch, 0, 0)),
                  pl.BlockSpec((d, tn), lambda i, j: (0, j)),
                  pl.BlockSpec((1, tn), lambda i, j: (0, j)),
                  pl.BlockSpec((tm, LANES), lambda i, j: (i, 0)),
                  pl.BlockSpec((tm, LANES), lambda i, j: (i, 0))],
        out_specs=[pl.BlockSpec((tm, tn), lambda i, j: (i, j)),
                   pl.BlockSpec((tm // MOBA_BLOCK, 1, tn), lambda i, j: (i, 0, 0))],
        out_shape=[jax.ShapeDtypeStruct((n, cols), jnp.bfloat16),
                   jax.ShapeDtypeStruct((n // MOBA_BLOCK, 1, tn), jnp.bfloat16)],
        scratch_shapes=[pltpu.VMEM((tm, d), jnp.bfloat16)],
        compiler_params=_cparams(("parallel", "arbitrary")),
        name="in_proj",
    )(x2, sc1, sh1, w_in, bias, cos_t, sin_t)


def _gate_kernel(q_ref, km_ref, qx_ref):
    i = pl.program_id(1)
    bs = q_ref.shape[0]
    nblk = km_ref.shape[1]
    sub = lax.broadcasted_iota(jnp.int32, (nblk, bs), 0)
    past = sub < i
    for h in range(N_HEADS):
        hs = slice(h * HEAD_DIM, (h + 1) * HEAD_DIM)
        q = q_ref[:, hs]
        g = jnp.where(past, _dot_nt(km_ref[0, :, hs], q), -jnp.inf)
        attend = sub == i
        for _ in range(MOBA_TOPK):
            m = jnp.max(g, axis=0, keepdims=True)
            idx = jnp.min(jnp.where(g == m, sub, nblk), axis=0, keepdims=True)
            hit = sub == idx
            attend = attend | (hit & past)
            g = jnp.where(hit, -jnp.inf, g)
        att = jnp.concatenate([jnp.where(attend, 1.0, 0.0), jnp.ones((LANES - nblk, bs), jnp.float32)], axis=0)
        bias = jnp.where(att.T > 0.5, 0.0, NEG_INF)
        qx_ref[:, 2 * h * HEAD_DIM:(2 * h + 1) * HEAD_DIM] = q
        qx_ref[:, (2 * h + 1) * HEAD_DIM:(2 * h + 2) * HEAD_DIM] = bias.astype(qx_ref.dtype)


def _gate(z, kmean, nb, seq, q_tile):
    n = z.shape[0]
    nq = seq // MOBA_BLOCK
    aw = N_HEADS * HEAD_DIM
    return pl.pallas_call(
        _gate_kernel,
        grid=(nb, nq),
        in_specs=[pl.BlockSpec((MOBA_BLOCK, aw), lambda b, i: (b * nq + i, q_tile)),
                  pl.BlockSpec((1, nq, aw), lambda b, i: (b, 0, 0))],
        out_specs=pl.BlockSpec((MOBA_BLOCK, 2 * aw), lambda b, i: (b * nq + i, 0)),
        out_shape=jax.ShapeDtypeStruct((n, 2 * aw), jnp.bfloat16),
        compiler_params=_cparams(("parallel", "arbitrary")),
        name="gate",
    )(z, kmean)


def _attn_kernel(qx_ref, k_ref, v_ref, o_ref, kx_ref, vx_ref, s_ref, p_ref):
    i = pl.program_id(2)
    bs = MOBA_BLOCK
    nblk = k_ref.shape[0] // bs
    wide = HEAD_DIM + LANES
    bf16 = jnp.bfloat16
    f32 = jnp.float32
    exp2_scale = (HEAD_DIM ** -0.5) * 1.4426950408889634
    grp = ATTN_GROUP * bs

    @pl.when(i == 0)
    def _():
        sub_k = lax.broadcasted_iota(jnp.int32, (LANES, bs), 0)
        for hh in range(ATTN_HEADS):
            hs = slice(hh * HEAD_DIM, (hh + 1) * HEAD_DIM)
            vx_ref[hh, :, :HEAD_DIM] = v_ref[:, hs]
            vx_ref[hh, :, HEAD_DIM:] = jnp.ones((nblk * bs, LANES), bf16)
            for jb in range(nblk):
                cs = slice(jb * bs, (jb + 1) * bs)
                kx_ref[hh, :HEAD_DIM, cs] = k_ref[cs, hs].astype(f32).T.astype(bf16)
                kx_ref[hh, HEAD_DIM:, cs] = jnp.where(sub_k == jb, 1.0, 0.0).astype(bf16)

    for c in range(nblk // ATTN_GROUP):
        @pl.when(i // ATTN_GROUP == c)
        def _(c=c):
            nk = (c + 1) * grp
            row = lax.broadcasted_iota(jnp.int32, (ATTN_STRIP, LANES), 0) + i * bs
            col = lax.broadcasted_iota(jnp.int32, (ATTN_STRIP, LANES), 1)
            for hh in range(ATTN_HEADS):
                s_ref[hh, :, :nk] = jnp.dot(qx_ref[:, hh * wide:(hh + 1) * wide], kx_ref[hh, :, 0:nk],
                                            preferred_element_type=f32)
            for hh in range(ATTN_HEADS):
                for r0 in range(0, bs, ATTN_STRIP):
                    rs = slice(r0, r0 + ATTN_STRIP)
                    mx = None
                    for t in range(nk // LANES):
                        ts = slice(t * LANES, (t + 1) * LANES)
                        st = s_ref[hh, rs, ts]
                        if t * LANES >= c * grp:
                            st = jnp.where(col + t * LANES <= row + r0, st, NEG_INF)
                            s_ref[hh, rs, ts] = st
                        mx = st if mx is None else jnp.maximum(mx, st)
                    mb = jnp.broadcast_to(jnp.max(mx, axis=-1, keepdims=True) * exp2_scale, (ATTN_STRIP, LANES))
                    for t in range(nk // LANES):
                        ts = slice(t * LANES, (t + 1) * LANES)
                        p_ref[hh, rs, ts] = jnp.exp2(s_ref[hh, rs, ts] * exp2_scale - mb).astype(bf16)
            for hh in range(ATTN_HEADS):
                acc = jnp.dot(p_ref[hh, :, :nk], vx_ref[hh, 0:nk, :], preferred_element_type=f32)
                o_ref[:, hh * HEAD_DIM:(hh + 1) * HEAD_DIM] = (
                    acc[:, :HEAD_DIM] / acc[:, HEAD_DIM:]).astype(o_ref.dtype)


def _attention(qx, z, nb, seq, k_col, v_col):
    n = z.shape[0]
    nq = seq // MOBA_BLOCK
    bs = MOBA_BLOCK
    wide = HEAD_DIM + LANES
    hps = ATTN_HEADS
    assert N_HEADS % hps == 0 and k_col % hps == 0 and v_col % hps == 0
    return pl.pallas_call(
        _attn_kernel,
        grid=(nb, N_HEADS // hps, nq),
        in_specs=[pl.BlockSpec((bs, hps * wide), lambda b, h, i: (b * nq + i, h)),
                  pl.BlockSpec((seq, hps * HEAD_DIM), lambda b, h, i: (b, k_col // hps + h),
                               pipeline_mode=pl.Buffered(1)),
                  pl.BlockSpec((seq, hps * HEAD_DIM), lambda b, h, i: (b, v_col // hps + h),
                               pipeline_mode=pl.Buffered(1))],
        out_specs=pl.BlockSpec((bs, hps * HEAD_DIM), lambda b, h, i: (b * nq + i, h)),
        out_shape=jax.ShapeDtypeStruct((n, N_HEADS * HEAD_DIM), jnp.bfloat16),
        scratch_shapes=[pltpu.VMEM((hps, wide, seq), jnp.bfloat16),
                        pltpu.VMEM((hps, seq, wide), jnp.bfloat16),
                        pltpu.VMEM((hps, bs, seq), jnp.float32),
                        pltpu.VMEM((hps, bs, seq), jnp.bfloat16)],
        compiler_params=_cparams(("parallel", "parallel", "arbitrary"), vmem=ATTN_VMEM_LIMIT),
        name="attn",
    )(qx, z, z)


def _conv_kernel(a_ref, b_ref, ha_ref, hb_ref, w_ref, bdw_ref, g_ref, bcn_ref, o_ref, hext_ref, grp_ref):
    i = pl.program_id(1)
    tt = a_ref.shape[0]
    halo = ha_ref.shape[0]
    ch = a_ref.shape[1]
    glu_h = ha_ref[...].astype(jnp.float32) * _sigmoid(hb_ref[...].astype(jnp.float32))
    hext_ref[0:halo, :] = jnp.where(i > 0, glu_h, 0.0)
    hext_ref[halo:halo + tt, :] = a_ref[...].astype(jnp.float32) * _sigmoid(b_ref[...].astype(jnp.float32))
    hext_ref[halo + tt:, :] = jnp.zeros((SUBLANES, ch), jnp.float32)

    first = halo - (CONV_WIDTH - 1)
    acc = jnp.zeros(o_ref.shape, jnp.float32) + bdw_ref[...]
    for s in range(SUBLANES):
        g = None
        for j in range(CONV_WIDTH):
            off = first + j
            if off % SUBLANES != s:
                continue
            term = w_ref[j:j + 1, :] * hext_ref[pl.ds(off - s, tt + SUBLANES), :]
            g = term if g is None else g + term
        if g is None:
            continue
        if s == 0:
            acc = acc + g[:tt]
        else:
            grp_ref[s] = g
            acc = acc + grp_ref[s, pl.ds(s, tt), :]
    y = _layer_norm_rows(acc) * g_ref[...] + bcn_ref[...]
    o_ref[...] = (y * _sigmoid(y)).astype(o_ref.dtype)


def _conv(z, w_dw, b_dw, g_cn, b_cn, nb, seq, a_tile, b_tile):
    n = z.shape[0]
    ch = w_dw.shape[1]
    tt, halo = CONV_TT, CONV_HALO
    nt = seq // tt
    r = tt // halo

    def cur(tile):
        return pl.BlockSpec((tt, ch), lambda b, i: (b * nt + i, tile))

    def prev(tile):
        return pl.BlockSpec((halo, ch), lambda b, i: (jnp.maximum((b * nt + i) * r - 1, 0), tile))

    vec = pl.BlockSpec((1, ch), lambda b, i: (0, 0))
    return pl.pallas_call(
        _conv_kernel,
        grid=(nb, nt),
        in_specs=[cur(a_tile), cur(b_tile), prev(a_tile), prev(b_tile),
                  pl.BlockSpec((CONV_WIDTH, ch), lambda b, i: (0, 0)), vec, vec, vec],
        out_specs=pl.BlockSpec((tt, ch), lambda b, i: (b * nt + i, 0)),
        out_shape=jax.ShapeDtypeStruct((n, ch), jnp.bfloat16),
        scratch_shapes=[pltpu.VMEM((halo + tt + SUBLANES, ch), jnp.float32),
                        pltpu.VMEM((SUBLANES, tt + SUBLANES, ch), jnp.float32)],
        compiler_params=_cparams(("parallel", "parallel")),
        name="conv",
    )(z, z, z, z, w_dw, b_dw.reshape(1, ch), g_cn.reshape(1, ch), b_cn.reshape(1, ch))


def _mix_kernel(hc_ref, o_ref, sgc0_ref, sgc1_ref, sga0_ref, sga1_ref, x_ref, wco_ref, bco_ref, wao_ref, wmx_ref,
                gt_ref, g1_ref, b1_ref, sc_ref, sh_ref, wr_ref, br_ref,
                x1_ref, u2_ref, ri_ref, rw_ref, *, alpha):
    f32 = jnp.float32
    y_conv = jnp.dot(hc_ref[...], wco_ref[...], preferred_element_type=f32) + bco_ref[...]
    y_attn = jnp.dot(o_ref[...], wao_ref[...], preferred_element_type=f32)
    sgc = jnp.concatenate([sgc0_ref[...], sgc1_ref[...]], axis=1).astype(f32)
    sga = jnp.concatenate([sga0_ref[...], sga1_ref[...]], axis=1).astype(f32)
    merged = sgc * y_conv + sga * y_attn
    t_out = jnp.dot(merged.astype(jnp.bfloat16), wmx_ref[...], preferred_element_type=f32)
    x1 = _layer_norm_rows(alpha * x_ref[...] + (1.0 + gt_ref[0]) * t_out) * g1_ref[...] + b1_ref[...]
    x1_ref[...] = x1
    u2 = _layer_norm_rows(x1) * (1.0 + sc_ref[0]) + sh_ref[0]
    _store_slabs(u2_ref, u2)

    bf16 = jnp.bfloat16
    u_hi = u2.astype(bf16)
    u_lo = (u2 - u_hi.astype(f32)).astype(bf16)
    w_hi = wr_ref[...].astype(bf16)
    w_lo = (wr_ref[...] - w_hi.astype(f32)).astype(bf16)
    hi_terms = jnp.dot(u_hi, jnp.concatenate([w_hi, w_lo], axis=1), preferred_element_type=f32)
    logits = (hi_terms[:, :LANES] + hi_terms[:, LANES:]
              + jnp.dot(u_lo, w_hi, preferred_element_type=f32) + br_ref[...])
    lane = lax.broadcasted_iota(jnp.int32, logits.shape, 1)
    is_grp = lane < N_GROUPS
    gl = jnp.where(is_grp, logits, -jnp.inf)
    gmax = jnp.max(gl, axis=-1, keepdims=True)
    gidx = jnp.min(jnp.where(gl == gmax, lane, LANES), axis=-1, keepdims=True)
    p_top = 1.0 / jnp.sum(jnp.where(is_grp, jnp.exp(logits - gmax), 0.0), axis=-1, keepdims=True)
    lo = N_GROUPS + gidx * EXPERTS_PER_GROUP
    el = jnp.where((lane >= lo) & (lane < lo + EXPERTS_PER_GROUP), logits, -jnp.inf)
    v1 = jnp.max(el, axis=-1, keepdims=True)
    i1 = jnp.min(jnp.where(el == v1, lane, LANES), axis=-1, keepdims=True)
    el = jnp.where(lane == i1, -jnp.inf, el)
    v2 = jnp.max(el, axis=-1, keepdims=True)
    i2 = jnp.min(jnp.where(el == v2, lane, LANES), axis=-1, keepdims=True)
    e2 = jnp.exp(v2 - v1)
    w1 = p_top / (1.0 + e2)
    w2 = p_top * e2 / (1.0 + e2)
    ri_ref[...] = jnp.where(lane == 0, i1 - N_GROUPS, jnp.where(lane == 1, i2 - N_GROUPS, 0))
    rw_ref[...] = jnp.where(lane == 0, w1, jnp.where(lane == 1, w2, 0.0))


def _mix(hc, o_attn, z, x2, wco, bco, wao, wmx, gt1, g1, b1, sc2, sh2, w_r, b_r, seq, gc_tile, ga_tile, alpha):
    n, d = x2.shape
    ch = hc.shape[1]
    tm = MIX_TM
    per_batch = seq // tm

    def rows(width):
        return pl.BlockSpec((tm, width), lambda i: (i, 0))

    def const(shape):
        return pl.BlockSpec(shape, lambda i: (0,) * len(shape), pipeline_mode=pl.Buffered(1))

    def per_b():
        return pl.BlockSpec((1, 1, d), lambda i: (i // per_batch, 0, 0))

    return pl.pallas_call(
        functools.partial(_mix_kernel, alpha=alpha),
        grid=(n // tm,),
        in_specs=[rows(ch), rows(ch),
                  pl.BlockSpec((tm, ch), lambda i: (i, gc_tile)),
                  pl.BlockSpec((tm, ch), lambda i: (i, gc_tile + 1)),
                  pl.BlockSpec((tm, ch), lambda i: (i, ga_tile)),
                  pl.BlockSpec((tm, ch), lambda i: (i, ga_tile + 1)),
                  rows(d),
                  const((ch, d)), const((1, d)), const((ch, d)), const((d, d)),
                  per_b(), const((1, d)), const((1, d)), per_b(), per_b(),
                  const((d, LANES)), const((1, LANES))],
        out_specs=[rows(d), pl.BlockSpec((tm * (d // LANES), LANES), lambda i: (i, 0)), rows(LANES), rows(LANES)],
        out_shape=[jax.ShapeDtypeStruct((n, d), jnp.float32),
                   jax.ShapeDtypeStruct((n * (d // LANES), LANES), jnp.float32),
                   jax.ShapeDtypeStruct((n, LANES), jnp.int32),
                   jax.ShapeDtypeStruct((n, LANES), jnp.float32)],
        compiler_params=_cparams(("parallel",)),
        name="mix",
    )(hc, o_attn, z, z, z, z, x2, wco, bco, wao, wmx, gt1, g1, b1, sc2, sh2, w_r, b_r)


def _meta_kernel(eid_ref, dest_ref, tile_ref, cnt_ref, base_ref, *, tile_rows):
    ph = pl.program_id(0)
    r = pl.program_id(1)
    f32 = jnp.float32
    blk = eid_ref.shape[2]
    e = eid_ref[0]
    sub = lax.broadcasted_iota(jnp.int32, (N_EXPERTS, blk), 0)
    oh = (sub == e).astype(f32)
    blk_cnt = jnp.broadcast_to(jnp.sum(oh, axis=1, keepdims=True), (N_EXPERTS, LANES))

    @pl.when((ph == 0) & (r == 0))
    def _():
        cnt_ref[...] = jnp.zeros_like(cnt_ref)

    @pl.when(ph == 0)
    def _():
        cnt_ref[...] += blk_cnt

    @pl.when((ph == 1) & (r == 0))
    def _():
        padded = jnp.ceil(cnt_ref[...] * (1.0 / tile_rows)) * tile_rows
        er = lax.broadcasted_iota(jnp.int32, (N_EXPERTS, N_EXPERTS), 0)
        ec = lax.broadcasted_iota(jnp.int32, (N_EXPERTS, N_EXPERTS), 1)
        starts = jnp.dot((ec < er).astype(f32), padded, preferred_element_type=f32,
                         precision=lax.Precision.HIGHEST)
        base_ref[...] = starts
        ends = starts + padded
        lane = lax.broadcasted_iota(jnp.int32, (N_EXPERTS, LANES), 1)
        tile_start = (lane * tile_rows).astype(f32)
        owner = jnp.sum((ends <= tile_start).astype(f32), axis=0, keepdims=True)
        total_tiles = jnp.max(ends, axis=0, keepdims=True) * (1.0 / tile_rows)
        lane1 = lax.broadcasted_iota(jnp.int32, (1, LANES), 1)
        tile_ref[0:1, :] = jnp.where(lane1 == LANES - 1, total_tiles, owner).astype(jnp.int32)
        sub_e = lax.broadcasted_iota(jnp.int32, (N_EXPERTS, LANES), 0)
        diag = sub_e == lane
        tile_ref[1:2, :] = jnp.sum(jnp.where(diag, starts + cnt_ref[...], 0.0), axis=0,
                                   keepdims=True).astype(jnp.int32)
        tile_ref[2:3, :] = jnp.sum(jnp.where(diag, ends, 0.0), axis=0, keepdims=True).astype(jnp.int32)

    @pl.when(ph == 1)
    def _():
        ar = lax.broadcasted_iota(jnp.int32, (blk, blk), 0)
        ac = lax.broadcasted_iota(jnp.int32, (blk, blk), 1)
        before = (ar < ac).astype(jnp.bfloat16)
        cum = jnp.dot(oh.astype(jnp.bfloat16), before, preferred_element_type=f32)
        slot = jnp.sum(oh * (base_ref[:, 0:1] + cum), axis=0, keepdims=True)
        dest_ref[0] = slot.astype(jnp.int32)
        base_ref[...] += blk_cnt


def _meta(eid_flat, tile_rows):
    na = eid_flat.shape[0]
    nblk = na // META_BLK
    eid3 = eid_flat.reshape(nblk, 1, META_BLK)
    dest, tiles = pl.pallas_call(
        functools.partial(_meta_kernel, tile_rows=tile_rows),
        grid=(2, nblk),
        in_specs=[pl.BlockSpec((1, 1, META_BLK), lambda ph, r: (r, 0, 0))],
        out_specs=[pl.BlockSpec((1, 1, META_BLK), lambda ph, r: (ph * r, 0, 0)),
                   pl.BlockSpec((3, LANES), lambda ph, r: (0, 0))],
        out_shape=[jax.ShapeDtypeStruct((nblk, 1, META_BLK), jnp.int32),
                   jax.ShapeDtypeStruct((3, LANES), jnp.int32)],
        scratch_shapes=[pltpu.VMEM((N_EXPERTS, LANES), jnp.float32),
                        pltpu.VMEM((N_EXPERTS, LANES), jnp.float32)],
        compiler_params=_cparams(("arbitrary", "arbitrary")),
        name="meta",
    )(eid3)
    return dest.reshape(na), tiles.reshape(3 * LANES)


def _dispatch_kernel(dest_ref, seg_ref, u_ref, xs_ref, zero_ref, sem, zsem, *, n_tok, tile_rows):
    i = pl.program_id(0)
    slab = xs_ref.shape[1]
    tm = u_ref.shape[0] // slab
    base = i * tm

    @pl.when(i == 0)
    def _():
        zero_ref[...] = jnp.zeros_like(zero_ref)

        def fills(e, act):
            lo = seg_ref[LANES + e]
            pad = seg_ref[2 * LANES + e] - lo
            pos = lo
            for bit in reversed(range(tile_rows.bit_length() - 1)):
                rows = 1 << bit
                take = (pad >> bit) & 1

                @pl.when(take == 1)
                def _(pos=pos, rows=rows):
                    act(pltpu.make_async_copy(zero_ref.at[pl.ds(0, rows)], xs_ref.at[pl.ds(pos, rows)], zsem))

                pos = pos + take * rows

        def start_fills(e, c):
            fills(e, lambda cp: cp.start())
            return c

        def wait_fills(e, c):
            fills(e, lambda cp: cp.wait())
            return c

        run = zero_ref.shape[0]
        used = seg_ref[2 * LANES + N_EXPERTS - 1]
        n_runs = (xs_ref.shape[0] - used) // run

        def tail_fill(j):
            return pltpu.make_async_copy(zero_ref, xs_ref.at[pl.ds(used + j * run, run)], zsem)

        def start_tail(j, c):
            tail_fill(j).start()
            return c

        def wait_tail(j, c):
            tail_fill(j).wait()
            return c

        lax.fori_loop(0, N_EXPERTS, start_fills, 0)
        lax.fori_loop(0, n_runs, start_tail, 0)
        lax.fori_loop(0, N_EXPERTS, wait_fills, 0)
        lax.fori_loop(0, n_runs, wait_tail, 0)

    def row_copy(r, k):
        d = dest_ref[k * n_tok + base + r]
        return pltpu.make_async_copy(u_ref.at[pl.ds(r * slab, slab), :], xs_ref.at[d], sem)

    for r in range(tm):
        for k in range(TOPK_IN_GROUP):
            row_copy(r, k).start(priority=(r * TOPK_IN_GROUP + k) % DMA_PRIORITIES)

    def drain(r, c):
        for k in range(TOPK_IN_GROUP):
            row_copy(r, k).wait()
        return c

    lax.fori_loop(0, tm, drain, 0, unroll=8)


def _dispatch(dest, tiles, u2s, n_tok, n_slots, tile_rows):
    slab = u2s.shape[0] // n_tok
    tm = ROW_TM
    assert tile_rows & (tile_rows - 1) == 0
    return pl.pallas_call(
        functools.partial(_dispatch_kernel, n_tok=n_tok, tile_rows=tile_rows),
        grid_spec=pltpu.PrefetchScalarGridSpec(
            num_scalar_prefetch=2,
            grid=(n_tok // tm,),
            in_specs=[pl.BlockSpec((tm * slab, LANES), lambda i, dest, seg: (i, 0))],
            out_specs=pl.BlockSpec(memory_space=pl.ANY),
            scratch_shapes=[pltpu.VMEM((tile_rows // 2, slab, LANES), jnp.float32),
                            pltpu.SemaphoreType.DMA(()),
                            pltpu.SemaphoreType.DMA(())]),
        out_shape=jax.ShapeDtypeStruct((n_slots, slab, LANES), jnp.float32),
        compiler_params=_cparams(("arbitrary",)),
        name="dispatch",
    )(dest, tiles, u2s)


def _expert_kernel(tile_ref, x_ref, w1_ref, w3_ref, w2_ref, y_ref, wf1, wf3, wf2, w1b, w3b, w2b, wslot_ref, wsem):
    t = pl.program_id(0)
    tm = EXP_TM
    bf16 = jnp.bfloat16
    n_used = tile_ref[LANES - 1]
    e = tile_ref[t]
    changed = (t == 0) | (e != tile_ref[jnp.maximum(t - 1, 0)])
    active = t < n_used

    def weight_copies(ex, ws):
        return (pltpu.make_async_copy(w1_ref.at[ex], wf1.at[ws], wsem.at[ws, 0]),
                pltpu.make_async_copy(w3_ref.at[ex], wf3.at[ws], wsem.at[ws, 1]),
                pltpu.make_async_copy(w2_ref.at[ex], wf2.at[ws], wsem.at[ws, 2]))

    @pl.when(t == 0)
    def _():
        for cp in weight_copies(e, 0):
            cp.start()
        wslot_ref[0] = 0

    @pl.when(active & changed)
    def _():
        ws = wslot_ref[0]
        for cp in weight_copies(e, ws):
            cp.wait()
        nxt = lax.while_loop(lambda t2: (t2 < n_used) & (tile_ref[t2] == e), lambda t2: t2 + 1, t + 1)

        @pl.when(nxt < n_used)
        def _():
            for cp in weight_copies(tile_ref[nxt], 1 - ws):
                cp.start(priority=DMA_PRIORITIES - 1)

        w1b[...] = wf1[ws].astype(bf16)
        w3b[...] = wf3[ws].astype(bf16)
        w2b[...] = wf2[ws].astype(bf16)
        wslot_ref[0] = 1 - ws

    @pl.when(active)
    def _():
        xb = _load_slabs(x_ref, tm).astype(bf16)
        h1 = jnp.dot(xb, w1b[...], preferred_element_type=jnp.float32)
        h3 = jnp.dot(xb, w3b[...], preferred_element_type=jnp.float32)
        hh = (h1 * _sigmoid(h1) * h3).astype(bf16)
        _store_slabs(y_ref, jnp.dot(hh, w2b[...], preferred_element_type=jnp.float32))

    @pl.when(jnp.logical_not(active))
    def _():
        y_ref[...] = jnp.zeros_like(y_ref)


def _experts(tiles, xs2, w1, w3, w2, n_tiles):
    d, f = w1.shape[1], w1.shape[2]
    tm = EXP_TM
    slab = d // LANES
    any_spec = pl.BlockSpec(memory_space=pl.ANY)

    def x_idx(t, tiles):
        return (jnp.maximum(jnp.minimum(t, tiles[LANES - 1] - 1), 0), 0)

    return pl.pallas_call(
        _expert_kernel,
        grid_spec=pltpu.PrefetchScalarGridSpec(
            num_scalar_prefetch=1,
            grid=(n_tiles,),
            in_specs=[pl.BlockSpec((tm * slab, LANES), x_idx), any_spec, any_spec, any_spec],
            out_specs=pl.BlockSpec((tm * slab, LANES), lambda t, tiles: (t, 0)),
            scratch_shapes=[pltpu.VMEM((2, d, f), jnp.float32),
                            pltpu.VMEM((2, d, f), jnp.float32),
                            pltpu.VMEM((2, f, d), jnp.float32),
                            pltpu.VMEM((d, f), jnp.bfloat16),
                            pltpu.VMEM((d, f), jnp.bfloat16),
                            pltpu.VMEM((f, d), jnp.bfloat16),
                            pltpu.SMEM((1,), jnp.int32),
                            pltpu.SemaphoreType.DMA((2, 3))]),
        out_shape=jax.ShapeDtypeStruct((n_tiles * tm * slab, LANES), jnp.float32),
        compiler_params=_cparams(("arbitrary",)),
        name="experts",
    )(tiles, xs2, w1, w3, w2)


def _combine_kernel(dest_ref, ys_ref, x1_ref, rw_ref, gt_ref, g2_ref, b2_ref, o_ref, ybuf0, ybuf1, sem,
                    *, stride, alpha):
    i = pl.program_id(0)
    tm = x1_ref.shape[0]
    slab = ys_ref.shape[1]
    ybufs = (ybuf0, ybuf1)

    def row_copy(slot, buf, k, r):
        return pltpu.make_async_copy(ys_ref.at[slot], ybufs[buf].at[k, pl.ds(r * slab, slab), :], sem.at[buf])

    def gather(step, buf):
        for r in range(tm):
            for k in range(TOPK_IN_GROUP):
                row_copy(dest_ref[k * stride + step * tm + r], buf, k, r).start(
                    priority=(r * TOPK_IN_GROUP + k) % DMA_PRIORITIES)

    def gather_wait(buf):
        def one(r, c):
            for k in range(TOPK_IN_GROUP):
                row_copy(0, buf, k, r).wait()
            return c

        lax.fori_loop(0, tm, one, 0, unroll=8)

    @pl.when(i == 0)
    def _():
        gather(0, 0)

    for buf in range(2):
        @pl.when(i % 2 == buf)
        def _(buf=buf):
            gather_wait(buf)
            gather(i + 1, 1 - buf)
            rw = rw_ref[...]
            f = (rw[:, 0:1] * _load_slabs(ybufs[buf].at[0], tm)
                 + rw[:, 1:2] * _load_slabs(ybufs[buf].at[1], tm))
            y = _layer_norm_rows(alpha * x1_ref[...] + (1.0 + gt_ref[0]) * f)
            o_ref[...] = y * g2_ref[...] + b2_ref[...]

        @pl.when((i == pl.num_programs(0) - 1) & (i % 2 == buf))
        def _(buf=buf):
            gather_wait(1 - buf)


def _combine(dest, ys, x1, rw, gt2, g2, b2, seq, alpha):
    n, d = x1.shape
    tm = ROW_TM
    per_batch = seq // tm
    dest_pad = jnp.concatenate([dest.reshape(TOPK_IN_GROUP, n), jnp.zeros((TOPK_IN_GROUP, tm), jnp.int32)],
                               axis=1).reshape(TOPK_IN_GROUP * (n + tm))
    ybuf = pltpu.VMEM((TOPK_IN_GROUP, tm * (d // LANES), LANES), jnp.float32)
    return pl.pallas_call(
        functools.partial(_combine_kernel, stride=n + tm, alpha=alpha),
        grid_spec=pltpu.PrefetchScalarGridSpec(
            num_scalar_prefetch=1,
            grid=(n // tm,),
            in_specs=[pl.BlockSpec(memory_space=pl.ANY),
                      pl.BlockSpec((tm, d), lambda i, dest: (i, 0)),
                      pl.BlockSpec((tm, LANES), lambda i, dest: (i, 0)),
                      pl.BlockSpec((1, 1, d), lambda i, dest: (i // per_batch, 0, 0)),
                      pl.BlockSpec((1, d), lambda i, dest: (0, 0)),
                      pl.BlockSpec((1, d), lambda i, dest: (0, 0))],
            out_specs=pl.BlockSpec((tm, d), lambda i, dest: (i, 0)),
            scratch_shapes=[ybuf, ybuf, pltpu.SemaphoreType.DMA((2,))]),
        out_shape=jax.ShapeDtypeStruct((n, d), jnp.float32),
        compiler_params=_cparams(("arbitrary",)),
        name="combine",
    )(dest_pad, ys, x1, rw, gt2, g2, b2)


def kernel(x, c, positions, w_cond, b_cond, w_in, b_glu, w_dw, b_dw, g_cn, b_cn, w_conv_out, b_conv_out, w_attn_out, w_mix_out, g_ln1, b_ln1, w_grp, b_grp, w_erouter, b_erouter, w1, w3, w2, g_ln2, b_ln2):
    nb, seq, d = x.shape
    depth = w_cond.shape[0]
    n = nb * seq
    ch = w_dw.shape[2]
    aw = N_HEADS * HEAD_DIM
    in_cols = w_in.shape[2]
    alpha = (2.0 * depth) ** 0.25
    bf16 = jnp.bfloat16
    assert 2 * ch + 3 * aw + 2 * d == in_cols and ch == PROJ_TN and aw == PROJ_TN and d == 2 * PROJ_TN
    assert seq % MOBA_BLOCK == 0 and seq // MOBA_BLOCK <= LANES and N_GROUPS + N_EXPERTS <= LANES
    t_a, t_b, t_q, t_k, t_v, t_gc, t_ga = 0, 1, 2, 3, 4, 5, 7
    heads_per_tile = PROJ_TN // HEAD_DIM
    nblk = seq // MOBA_BLOCK
    n_tiles = (n * TOPK_IN_GROUP) // EXP_TM + N_EXPERTS
    assert (n * TOPK_IN_GROUP) % EXP_TM == 0 and n_tiles < LANES

    cos_t, sin_t = _rope_tables(positions)
    x2 = x.reshape(n, d)
    for l in range(depth):
        mod = _mod(c, w_cond[l], b_cond[l]).reshape(nb, 6, 1, d)
        sh1, sc1, gt1, sh2, sc2, gt2 = (mod[:, s] for s in range(6))

        bias = jnp.concatenate([b_glu[l], jnp.zeros((in_cols - 2 * ch,), jnp.float32)]).reshape(1, in_cols)
        z, km = _in_proj(x2, sc1, sh1, w_in[l], bias, cos_t, sin_t, seq,
                         rope_tiles=(t_q, t_k + 1), sig_from=t_gc)
        km = km.reshape(nb, nblk, aw)
        qx = _gate(z, km, nb, seq, t_q)
        o_attn = _attention(qx, z, nb, seq, t_k * heads_per_tile, t_v * heads_per_tile)
        hc = _conv(z, w_dw[l], b_dw[l], g_cn[l], b_cn[l], nb, seq, t_a, t_b)

        w_r = jnp.concatenate([w_grp[l], w_erouter[l],
                               jnp.zeros((d, LANES - N_GROUPS - N_EXPERTS), jnp.float32)], axis=1)
        b_r = jnp.concatenate([b_grp[l], b_erouter[l],
                               jnp.zeros((LANES - N_GROUPS - N_EXPERTS,), jnp.float32)]).reshape(1, LANES)
        x1, u2, ri, rw = _mix(hc, o_attn, z, x2, w_conv_out[l].astype(bf16), b_conv_out[l].reshape(1, d),
                              w_attn_out[l].astype(bf16), w_mix_out[l].astype(bf16), gt1,
                              g_ln1[l].reshape(1, d), b_ln1[l].reshape(1, d), sc2, sh2, w_r, b_r,
                              seq, t_gc, t_ga, alpha)

        eid = ri[:, :TOPK_IN_GROUP].T.reshape(TOPK_IN_GROUP * n)
        dest, tiles = _meta(eid, EXP_TM)
        f_in = w1.shape[-1]
        slab = d // LANES
        xs = _dispatch(dest, tiles, u2, n, n_tiles * EXP_TM, EXP_TM)
        ys = _experts(tiles, xs.reshape(n_tiles * EXP_TM * slab, LANES), w1[l].reshape(N_EXPERTS, d, f_in),
                      w3[l].reshape(N_EXPERTS, d, f_in), w2[l].reshape(N_EXPERTS, f_in, d), n_tiles)
        x2 = _combine(dest, ys.reshape(n_tiles * EXP_TM, slab, LANES), x1, rw, gt2, g_ln2[l].reshape(1, d), b_ln2[l].reshape(1, d), seq, alpha)
    return x2.reshape(nb, seq, d)
```

```python
import functools

import jax
import jax.numpy as jnp
from jax import lax
from jax.experimental import pallas as pl
from jax.experimental.pallas import tpu as pltpu

CONV_WIDTH = 31
N_HEADS = 8
HEAD_DIM = 128
ROPE_DIM = HEAD_DIM // 4
ROPE_THETA = 500000.0
MOBA_BLOCK = 256
MOBA_TOPK = 3
N_GROUPS = 4
EXPERTS_PER_GROUP = 8
N_EXPERTS = N_GROUPS * EXPERTS_PER_GROUP
TOPK_IN_GROUP = 2
LN_EPS = 1e-5
NEG_INF = -1e30

LANES = 128
SUBLANES = 8
DMA_PRIORITIES = 2
VMEM_LIMIT = 56 * 1024 * 1024
ATTN_VMEM_LIMIT = 60 * 1024 * 1024

PROJ_TM = 1024
PROJ_TN = 1024
PROJ_CHUNK = 256
CONV_TT = 256
CONV_HALO = 32
ATTN_GROUP = 4
ATTN_HEADS = 4
ATTN_STRIP = 128
MIX_TM = 256
EXP_TM = 256
ROW_TM = 256
META_BLK = 1024


def _cparams(sem, vmem=VMEM_LIMIT, flags=None):
    return pltpu.CompilerParams(dimension_semantics=sem, vmem_limit_bytes=vmem, flags=flags)


def _layer_norm_rows(x):
    mu = jnp.mean(x, axis=-1, keepdims=True)
    xc = x - mu
    var = jnp.mean(xc * xc, axis=-1, keepdims=True)
    return xc * lax.rsqrt(var + LN_EPS)


def _sigmoid(x):
    return 1.0 / (1.0 + jnp.exp(-x))


def _store_slabs(ref, val):
    m, w = val.shape
    c = w // LANES
    for k in range(c):
        ref[pl.ds(k, m, stride=c), :] = val[:, k * LANES:(k + 1) * LANES]


def _load_slabs(ref, m):
    c = ref.shape[0] // m
    return jnp.concatenate([ref[pl.ds(k, m, stride=c), :] for k in range(c)], axis=1)


def _dot_nt(a, b):
    return lax.dot_general(a, b, (((1,), (1,)), ((), ())), preferred_element_type=jnp.float32)


def _mod_kernel(cb_ref, w_ref, b_ref, o_ref, s_ref):
    @pl.when(pl.program_id(0) == 0)
    def _():
        cb = cb_ref[...]
        s_ref[...] = cb * _sigmoid(cb)

    nb, d, _ = cb_ref.shape
    tn = w_ref.shape[1]
    for cc in range(tn // LANES):
        sl = slice(cc * LANES, (cc + 1) * LANES)
        wc = w_ref[:, sl]
        for b in range(nb):
            p = (wc * s_ref[b]).reshape(d // SUBLANES, SUBLANES, LANES)
            r = jnp.sum(jnp.sum(p, axis=0), axis=0, keepdims=True)
            o_ref[b:b + 1, sl] = r + b_ref[:, sl]


def _mod(c, w_cond, b_cond, tn=1024):
    nb, d = c.shape
    n_out = w_cond.shape[1]
    cb = jnp.broadcast_to(c[:, :, None], (nb, d, LANES))
    return pl.pallas_call(
        _mod_kernel,
        grid=(n_out // tn,),
        in_specs=[pl.BlockSpec((nb, d, LANES), lambda j: (0, 0, 0)),
                  pl.BlockSpec((d, tn), lambda j: (0, j)),
                  pl.BlockSpec((1, tn), lambda j: (0, j))],
        out_specs=pl.BlockSpec((nb, tn), lambda j: (0, j)),
        out_shape=jax.ShapeDtypeStruct((nb, n_out), jnp.float32),
        scratch_shapes=[pltpu.VMEM((nb, d, LANES), jnp.float32)],
        compiler_params=_cparams(("arbitrary",)),
        name="mod",
    )(cb, w_cond, b_cond.reshape(1, n_out))


def _rope_tab_kernel(pos_ref, invf_ref, sign_ref, cos_ref, sin_ref):
    pack = LANES // ROPE_DIM
    ang = pos_ref[...] * invf_ref[...]
    cos_p = jnp.cos(ang)
    sin_p = jnp.sin(ang) * sign_ref[...]
    rows = ang.shape[0]
    rotated = lax.broadcasted_iota(jnp.int32, ang.shape, 1) < ROPE_DIM
    for t in range(pack):
        shift = (LANES - t * ROPE_DIM) % LANES
        c = pltpu.roll(cos_p, shift, 1) if shift else cos_p
        s = pltpu.roll(sin_p, shift, 1) if shift else sin_p
        cos_ref[pl.ds(t, rows, stride=pack), :] = jnp.where(rotated, c, 1.0)
        sin_ref[pl.ds(t, rows, stride=pack), :] = jnp.where(rotated, s, 0.0)


def _rope_tables(positions, tm=1024):
    n = positions.size
    half = ROPE_DIM // 2
    pack = LANES // ROPE_DIM
    inv_freq = jnp.power(ROPE_THETA, -jnp.arange(half, dtype=jnp.float32) / half)
    invf = jnp.tile(jnp.concatenate([inv_freq, inv_freq]), pack).reshape(1, LANES)
    sign = jnp.tile(jnp.concatenate([-jnp.ones((half,), jnp.float32), jnp.ones((half,), jnp.float32)]),
                    pack).reshape(1, LANES)
    posb = jnp.repeat(positions.astype(jnp.float32).reshape(n // pack, pack), ROPE_DIM, axis=1)
    vec = pl.BlockSpec((1, LANES), lambda i: (0, 0))
    tab = pl.BlockSpec((tm, LANES), lambda i: (i, 0))
    return pl.pallas_call(
        _rope_tab_kernel,
        grid=(n // tm,),
        in_specs=[pl.BlockSpec((tm // pack, LANES), lambda i: (i, 0)), vec, vec],
        out_specs=[tab, tab],
        out_shape=[jax.ShapeDtypeStruct((n, LANES), jnp.float32)] * 2,
        compiler_params=_cparams(("parallel",)),
        name="rope_tab",
    )(posb, invf, sign)


def _in_proj_kernel(x_ref, sc_ref, sh_ref, w_ref, bias_ref, cos_ref, sin_ref, o_ref, km_ref, u_ref,
                    *, rope_tiles, sig_from):
    j = pl.program_id(1)

    @pl.when(j == 0)
    def _():
        xn = _layer_norm_rows(x_ref[...])
        u_ref[...] = (xn * (1.0 + sc_ref[0]) + sh_ref[0]).astype(u_ref.dtype)

    is_rope = (j >= rope_tiles[0]) & (j < rope_tiles[1])

    def project(epilogue, block_means=False):
        for n in range(w_ref.shape[1] // PROJ_CHUNK):
            cs = slice(n * PROJ_CHUNK, (n + 1) * PROJ_CHUNK)
            acc = jnp.dot(u_ref[...], w_ref[:, cs].astype(u_ref.dtype),
                          preferred_element_type=jnp.float32) + bias_ref[:, cs]
            out = epilogue(acc).astype(o_ref.dtype)
            o_ref[:, cs] = out
            if block_means:
                stored = out.astype(jnp.float32)
                for blk in range(km_ref.shape[0]):
                    rows = stored[blk * MOBA_BLOCK:(blk + 1) * MOBA_BLOCK]
                    km_ref[blk, :, cs] = (jnp.sum(rows, axis=0, keepdims=True)
                                          * (1.0 / MOBA_BLOCK)).astype(km_ref.dtype)

    @pl.when(is_rope)
    def _():
        cosf = cos_ref[...]
        sinf = sin_ref[...]
        lane = lax.broadcasted_iota(jnp.int32, cosf.shape, 1)
        first = lane < (ROPE_DIM // 2)

        def rope(acc):
            heads = []
            for h in range(acc.shape[1] // HEAD_DIM):
                xh = acc[:, h * HEAD_DIM:(h + 1) * HEAD_DIM]
                partner = jnp.where(first, pltpu.roll(xh, HEAD_DIM - ROPE_DIM // 2, 1),
                                    pltpu.roll(xh, ROPE_DIM // 2, 1))
                heads.append(xh * cosf + partner * sinf)
            return jnp.concatenate(heads, axis=1)

        project(rope, block_means=True)

    @pl.when(j >= sig_from)
    def _():
        project(_sigmoid)

    @pl.when(jnp.logical_not(is_rope) & (j < sig_from))
    def _():
        project(lambda acc: acc)


def _in_proj(x2, sc1, sh1, w_in, bias, cos_t, sin_t, seq, rope_tiles, sig_from):
    n, d = x2.shape
    cols = w_in.shape[1]
    tm, tn = PROJ_TM, PROJ_TN
    per_batch = seq // tm
    kern = functools.partial(_in_proj_kernel, rope_tiles=rope_tiles, sig_from=sig_from)
    return pl.pallas_call(
        kern,
        grid=(n // tm, cols // tn),
        in_specs=[pl.BlockSpec((tm, d), lambda i, j: (i, 0)),
                  pl.BlockSpec((1, 1, d), lambda i, j: (i // per_batch, 0, 0)),
                  pl.BlockSpec((1, 1, d), lambda i, j: (i // per_batch, 0, 0)),
                  pl.BlockSpec((d, tn), lambda i, j: (0, j)),
                  pl.BlockSpec((1, tn), lambda i, j: (0, j)),
                  pl.BlockSpec((tm, LANES), lambda i, j: (i, 0)),
                  pl.BlockSpec((tm, LANES), lambda i, j: (i, 0))],
        out_specs=[pl.BlockSpec((tm, tn), lambda i, j: (i, j)),
                   pl.BlockSpec((tm // MOBA_BLOCK, 1, tn), lambda i, j: (i, 0, 0))],
        out_shape=[jax.ShapeDtypeStruct((n, cols), jnp.bfloat16),
                   jax.ShapeDtypeStruct((n // MOBA_BLOCK, 1, tn), jnp.bfloat16)],
        scratch_shapes=[pltpu.VMEM((tm, d), jnp.bfloat16)],
        compiler_params=_cparams(("parallel", "arbitrary")),
        name="in_proj",
    )(x2, sc1, sh1, w_in, bias, cos_t, sin_t)


def _gate_kernel(q_ref, km_ref, qx_ref):
    i = pl.program_id(1)
    bs = q_ref.shape[0]
    nblk = km_ref.shape[1]
    sub = lax.broadcasted_iota(jnp.int32, (nblk, bs), 0)
    past = sub < i
    for h in range(N_HEADS):
        hs = slice(h * HEAD_DIM, (h + 1) * HEAD_DIM)
        q = q_ref[:, hs]
        g = jnp.where(past, _dot_nt(km_ref[0, :, hs], q), -jnp.inf)
        attend = sub == i
        for _ in range(MOBA_TOPK):
            m = jnp.max(g, axis=0, keepdims=True)
            idx = jnp.min(jnp.where(g == m, sub, nblk), axis=0, keepdims=True)
            hit = sub == idx
            attend = attend | (hit & past)
            g = jnp.where(hit, -jnp.inf, g)
        att = jnp.concatenate([jnp.where(attend, 1.0, 0.0), jnp.ones((LANES - nblk, bs), jnp.float32)], axis=0)
        bias = jnp.where(att.T > 0.5, 0.0, NEG_INF)
        qx_ref[:, 2 * h * HEAD_DIM:(2 * h + 1) * HEAD_DIM] = q
        qx_ref[:, (2 * h + 1) * HEAD_DIM:(2 * h + 2) * HEAD_DIM] = bias.astype(qx_ref.dtype)


def _gate(z, kmean, nb, seq, q_tile):
    n = z.shape[0]
    nq = seq // MOBA_BLOCK
    aw = N_HEADS * HEAD_DIM
    return pl.pallas_call(
        _gate_kernel,
        grid=(nb, nq),
        in_specs=[pl.BlockSpec((MOBA_BLOCK, aw), lambda b, i: (b * nq + i, q_tile)),
                  pl.BlockSpec((1, nq, aw), lambda b, i: (b, 0, 0))],
        out_specs=pl.BlockSpec((MOBA_BLOCK, 2 * aw), lambda b, i: (b * nq + i, 0)),
        out_shape=jax.ShapeDtypeStruct((n, 2 * aw), jnp.bfloat16),
        compiler_params=_cparams(("parallel", "arbitrary")),
        name="gate",
    )(z, kmean)


def _attn_kernel(qx_ref, k_ref, v_ref, o_ref, kx_ref, vx_ref, s_ref, p_ref):
    i = pl.program_id(2)
    bs = MOBA_BLOCK
    nblk = k_ref.shape[0] // bs
    wide = HEAD_DIM + LANES
    bf16 = jnp.bfloat16
    f32 = jnp.float32
    exp2_scale = (HEAD_DIM ** -0.5) * 1.4426950408889634
    grp = ATTN_GROUP * bs

    @pl.when(i == 0)
    def _():
        sub_k = lax.broadcasted_iota(jnp.int32, (LANES, bs), 0)
        for hh in range(ATTN_HEADS):
            hs = slice(hh * HEAD_DIM, (hh + 1) * HEAD_DIM)
            vx_ref[hh, :, :HEAD_DIM] = v_ref[:, hs]
            vx_ref[hh, :, HEAD_DIM:] = jnp.ones((nblk * bs, LANES), bf16)
            for jb in range(nblk):
                cs = slice(jb * bs, (jb + 1) * bs)
                kx_ref[hh, :HEAD_DIM, cs] = k_ref[cs, hs].astype(f32).T.astype(bf16)
                kx_ref[hh, HEAD_DIM:, cs] = jnp.where(sub_k == jb, 1.0, 0.0).astype(bf16)

    for c in range(nblk // ATTN_GROUP):
        @pl.when(i // ATTN_GROUP == c)
        def _(c=c):
            nk = (c + 1) * grp
            row = lax.broadcasted_iota(jnp.int32, (ATTN_STRIP, LANES), 0) + i * bs
            col = lax.broadcasted_iota(jnp.int32, (ATTN_STRIP, LANES), 1)
            for hh in range(ATTN_HEADS):
                s_ref[hh, :, :nk] = jnp.dot(qx_ref[:, hh * wide:(hh + 1) * wide], kx_ref[hh, :, 0:nk],
                                            preferred_element_type=f32)
            for hh in range(ATTN_HEADS):
                for r0 in range(0, bs, ATTN_STRIP):
                    rs = slice(r0, r0 + ATTN_STRIP)
                    mx = None
                    for t in range(nk // LANES):
                        ts = slice(t * LANES, (t + 1) * LANES)
                        st = s_ref[hh, rs, ts]
                        if t * LANES >= c * grp:
                            st = jnp.where(col + t * LANES <= row + r0, st, NEG_INF)
                            s_ref[hh, rs, ts] = st
                        mx = st if mx is None else jnp.maximum(mx, st)
                    mb = jnp.broadcast_to(jnp.max(mx, axis=-1, keepdims=True) * exp2_scale, (ATTN_STRIP, LANES))
                    for t in range(nk // LANES):
                        ts = slice(t * LANES, (t + 1) * LANES)
                        p_ref[hh, rs, ts] = jnp.exp2(s_ref[hh, rs, ts] * exp2_scale - mb).astype(bf16)
            for hh in range(ATTN_HEADS):
                acc = jnp.dot(p_ref[hh, :, :nk], vx_ref[hh, 0:nk, :], preferred_element_type=f32)
                o_ref[:, hh * HEAD_DIM:(hh + 1) * HEAD_DIM] = (
                    acc[:, :HEAD_DIM] / acc[:, HEAD_DIM:]).astype(o_ref.dtype)


def _attention(qx, z, nb, seq, k_col, v_col):
    n = z.shape[0]
    nq = seq // MOBA_BLOCK
    bs = MOBA_BLOCK
    wide = HEAD_DIM + LANES
    hps = ATTN_HEADS
    assert N_HEADS % hps == 0 and k_col % hps == 0 and v_col % hps == 0
    return pl.pallas_call(
        _attn_kernel,
        grid=(nb, N_HEADS // hps, nq),
        in_specs=[pl.BlockSpec((bs, hps * wide), lambda b, h, i: (b * nq + i, h)),
                  pl.BlockSpec((seq, hps * HEAD_DIM), lambda b, h, i: (b, k_col // hps + h),
                               pipeline_mode=pl.Buffered(1)),
                  pl.BlockSpec((seq, hps * HEAD_DIM), lambda b, h, i: (b, v_col // hps + h),
                               pipeline_mode=pl.Buffered(1))],
        out_specs=pl.BlockSpec((bs, hps * HEAD_DIM), lambda b, h, i: (b * nq + i, h)),
        out_shape=jax.ShapeDtypeStruct((n, N_HEADS * HEAD_DIM), jnp.bfloat16),
        scratch_shapes=[pltpu.VMEM((hps, wide, seq), jnp.bfloat16),
                        pltpu.VMEM((hps, seq, wide), jnp.bfloat16),
                        pltpu.VMEM((hps, bs, seq), jnp.float32),
                        pltpu.VMEM((hps, bs, seq), jnp.bfloat16)],
        compiler_params=_cparams(("parallel", "parallel", "arbitrary"), vmem=ATTN_VMEM_LIMIT),
        name="attn",
    )(qx, z, z)


def _conv_kernel(a_ref, b_ref, ha_ref, hb_ref, w_ref, bdw_ref, g_ref, bcn_ref, o_ref, hext_ref, grp_ref):
    i = pl.program_id(1)
    tt = a_ref.shape[0]
    halo = ha_ref.shape[0]
    ch = a_ref.shape[1]
    glu_h = ha_ref[...].astype(jnp.float32) * _sigmoid(hb_ref[...].astype(jnp.float32))
    hext_ref[0:halo, :] = jnp.where(i > 0, glu_h, 0.0)
    hext_ref[halo:halo + tt, :] = a_ref[...].astype(jnp.float32) * _sigmoid(b_ref[...].astype(jnp.float32))
    hext_ref[halo + tt:, :] = jnp.zeros((SUBLANES, ch), jnp.float32)

    first = halo - (CONV_WIDTH - 1)
    acc = jnp.zeros(o_ref.shape, jnp.float32) + bdw_ref[...]
    for s in range(SUBLANES):
        g = None
        for j in range(CONV_WIDTH):
            off = first + j
            if off % SUBLANES != s:
                continue
            term = w_ref[j:j + 1, :] * hext_ref[pl.ds(off - s, tt + SUBLANES), :]
            g = term if g is None else g + term
        if g is None:
            continue
        if s == 0:
            acc = acc + g[:tt]
        else:
            grp_ref[s] = g
            acc = acc + grp_ref[s, pl.ds(s, tt), :]
    y = _layer_norm_rows(acc) * g_ref[...] + bcn_ref[...]
    o_ref[...] = (y * _sigmoid(y)).astype(o_ref.dtype)


def _conv(z, w_dw, b_dw, g_cn, b_cn, nb, seq, a_tile, b_tile):
    n = z.shape[0]
    ch = w_dw.shape[1]
    tt, halo = CONV_TT, CONV_HALO
    nt = seq // tt
    r = tt // halo

    def cur(tile):
        return pl.BlockSpec((tt, ch), lambda b, i: (b * nt + i, tile))

    def prev(tile):
        return pl.BlockSpec((halo, ch), lambda b, i: (jnp.maximum((b * nt + i) * r - 1, 0), tile))

    vec = pl.BlockSpec((1, ch), lambda b, i: (0, 0))
    return pl.pallas_call(
        _conv_kernel,
        grid=(nb, nt),
        in_specs=[cur(a_tile), cur(b_tile), prev(a_tile), prev(b_tile),
                  pl.BlockSpec((CONV_WIDTH, ch), lambda b, i: (0, 0)), vec, vec, vec],
        out_specs=pl.BlockSpec((tt, ch), lambda b, i: (b * nt + i, 0)),
        out_shape=jax.ShapeDtypeStruct((n, ch), jnp.bfloat16),
        scratch_shapes=[pltpu.VMEM((halo + tt + SUBLANES, ch), jnp.float32),
                        pltpu.VMEM((SUBLANES, tt + SUBLANES, ch), jnp.float32)],
        compiler_params=_cparams(("parallel", "parallel")),
        name="conv",
    )(z, z, z, z, w_dw, b_dw.reshape(1, ch), g_cn.reshape(1, ch), b_cn.reshape(1, ch))


def _mix_kernel(hc_ref, o_ref, sgc0_ref, sgc1_ref, sga0_ref, sga1_ref, x_ref, wco_ref, bco_ref, wao_ref, wmx_ref,
                gt_ref, g1_ref, b1_ref, sc_ref, sh_ref, wr_ref, br_ref,
                x1_ref, u2_ref, ri_ref, rw_ref, *, alpha):
    f32 = jnp.float32
    y_conv = jnp.dot(hc_ref[...], wco_ref[...], preferred_element_type=f32) + bco_ref[...]
    y_attn = jnp.dot(o_ref[...], wao_ref[...], preferred_element_type=f32)
    sgc = jnp.concatenate([sgc0_ref[...], sgc1_ref[...]], axis=1).astype(f32)
    sga = jnp.concatenate([sga0_ref[...], sga1_ref[...]], axis=1).astype(f32)
    merged = sgc * y_conv + sga * y_attn
    t_out = jnp.dot(merged.astype(jnp.bfloat16), wmx_ref[...], preferred_element_type=f32)
    x1 = _layer_norm_rows(alpha * x_ref[...] + (1.0 + gt_ref[0]) * t_out) * g1_ref[...] + b1_ref[...]
    x1_ref[...] = x1
    u2 = _layer_norm_rows(x1) * (1.0 + sc_ref[0]) + sh_ref[0]
    _store_slabs(u2_ref, u2)

    bf16 = jnp.bfloat16
    u_hi = u2.astype(bf16)
    u_lo = (u2 - u_hi.astype(f32)).astype(bf16)
    w_hi = wr_ref[...].astype(bf16)
    w_lo = (wr_ref[...] - w_hi.astype(f32)).astype(bf16)
    hi_terms = jnp.dot(u_hi, jnp.concatenate([w_hi, w_lo], axis=1), preferred_element_type=f32)
    logits = (hi_terms[:, :LANES] + hi_terms[:, LANES:]
              + jnp.dot(u_lo, w_hi, preferred_element_type=f32) + br_ref[...])
    lane = lax.broadcasted_iota(jnp.int32, logits.shape, 1)
    is_grp = lane < N_GROUPS
    gl = jnp.where(is_grp, logits, -jnp.inf)
    gmax = jnp.max(gl, axis=-1, keepdims=True)
    gidx = jnp.min(jnp.where(gl == gmax, lane, LANES), axis=-1, keepdims=True)
    p_top = 1.0 / jnp.sum(jnp.where(is_grp, jnp.exp(logits - gmax), 0.0), axis=-1, keepdims=True)
    lo = N_GROUPS + gidx * EXPERTS_PER_GROUP
    el = jnp.where((lane >= lo) & (lane < lo + EXPERTS_PER_GROUP), logits, -jnp.inf)
    v1 = jnp.max(el, axis=-1, keepdims=True)
    i1 = jnp.min(jnp.where(el == v1, lane, LANES), axis=-1, keepdims=True)
    el = jnp.where(lane == i1, -jnp.inf, el)
    v2 = jnp.max(el, axis=-1, keepdims=True)
    i2 = jnp.min(jnp.where(el == v2, lane, LANES), axis=-1, keepdims=True)
    e2 = jnp.exp(v2 - v1)
    w1 = p_top / (1.0 + e2)
    w2 = p_top * e2 / (1.0 + e2)
    ri_ref[...] = jnp.where(lane == 0, i1 - N_GROUPS, jnp.where(lane == 1, i2 - N_GROUPS, 0))
    rw_ref[...] = jnp.where(lane == 0, w1, jnp.where(lane == 1, w2, 0.0))


def _mix(hc, o_attn, z, x2, wco, bco, wao, wmx, gt1, g1, b1, sc2, sh2, w_r, b_r, seq, gc_tile, ga_tile, alpha):
    n, d = x2.shape
    ch = hc.shape[1]
    tm = MIX_TM
    per_batch = seq // tm

    def rows(width):
        return pl.BlockSpec((tm, width), lambda i: (i, 0))

    def const(shape):
        return pl.BlockSpec(shape, lambda i: (0,) * len(shape), pipeline_mode=pl.Buffered(1))

    def per_b():
        return pl.BlockSpec((1, 1, d), lambda i: (i // per_batch, 0, 0))

    return pl.pallas_call(
        functools.partial(_mix_kernel, alpha=alpha),
        grid=(n // tm,),
        in_specs=[rows(ch), rows(ch),
                  pl.BlockSpec((tm, ch), lambda i: (i, gc_tile)),
                  pl.BlockSpec((tm, ch), lambda i: (i, gc_tile + 1)),
                  pl.BlockSpec((tm, ch), lambda i: (i, ga_tile)),
                  pl.BlockSpec((tm, ch), lambda i: (i, ga_tile + 1)),
                  rows(d),
                  const((ch, d)), const((1, d)), const((ch, d)), const((d, d)),
                  per_b(), const((1, d)), const((1, d)), per_b(), per_b(),
                  const((d, LANES)), const((1, LANES))],
        out_specs=[rows(d), pl.BlockSpec((tm * (d // LANES), LANES), lambda i: (i, 0)), rows(LANES), rows(LANES)],
        out_shape=[jax.ShapeDtypeStruct((n, d), jnp.float32),
                   jax.ShapeDtypeStruct((n * (d // LANES), LANES), jnp.float32),
                   jax.ShapeDtypeStruct((n, LANES), jnp.int32),
                   jax.ShapeDtypeStruct((n, LANES), jnp.float32)],
        compiler_params=_cparams(("parallel",)),
        name="mix",
    )(hc, o_attn, z, z, z, z, x2, wco, bco, wao, wmx, gt1, g1, b1, sc2, sh2, w_r, b_r)


def _meta_kernel(eid_ref, dest_ref, tile_ref, cnt_ref, base_ref, *, tile_rows):
    ph = pl.program_id(0)
    r = pl.program_id(1)
    f32 = jnp.float32
    blk = eid_ref.shape[2]
    e = eid_ref[0]
    sub = lax.broadcasted_iota(jnp.int32, (N_EXPERTS, blk), 0)
    oh = (sub == e).astype(f32)
    blk_cnt = jnp.broadcast_to(jnp.sum(oh, axis=1, keepdims=True), (N_EXPERTS, LANES))

    @pl.when((ph == 0) & (r == 0))
    def _():
        cnt_ref[...] = jnp.zeros_like(cnt_ref)

    @pl.when(ph == 0)
    def _():
        cnt_ref[...] += blk_cnt

    @pl.when((ph == 1) & (r == 0))
    def _():
        padded = jnp.ceil(cnt_ref[...] * (1.0 / tile_rows)) * tile_rows
        er = lax.broadcasted_iota(jnp.int32, (N_EXPERTS, N_EXPERTS), 0)
        ec = lax.broadcasted_iota(jnp.int32, (N_EXPERTS, N_EXPERTS), 1)
        starts = jnp.dot((ec < er).astype(f32), padded, preferred_element_type=f32,
                         precision=lax.Precision.HIGHEST)
        base_ref[...] = starts
        ends = starts + padded
        lane = lax.broadcasted_iota(jnp.int32, (N_EXPERTS, LANES), 1)
        tile_start = (lane * tile_rows).astype(f32)
        owner = jnp.sum((ends <= tile_start).astype(f32), axis=0, keepdims=True)
        total_tiles = jnp.max(ends, axis=0, keepdims=True) * (1.0 / tile_rows)
        lane1 = lax.broadcasted_iota(jnp.int32, (1, LANES), 1)
        tile_ref[0:1, :] = jnp.where(lane1 == LANES - 1, total_tiles, owner).astype(jnp.int32)
        sub_e = lax.broadcasted_iota(jnp.int32, (N_EXPERTS, LANES), 0)
        diag = sub_e == lane
        tile_ref[1:2, :] = jnp.sum(jnp.where(diag, starts + cnt_ref[...], 0.0), axis=0,
                                   keepdims=True).astype(jnp.int32)
        tile_ref[2:3, :] = jnp.sum(jnp.where(diag, ends, 0.0), axis=0, keepdims=True).astype(jnp.int32)

    @pl.when(ph == 1)
    def _():
        ar = lax.broadcasted_iota(jnp.int32, (blk, blk), 0)
        ac = lax.broadcasted_iota(jnp.int32, (blk, blk), 1)
        before = (ar < ac).astype(jnp.bfloat16)
        cum = jnp.dot(oh.astype(jnp.bfloat16), before, preferred_element_type=f32)
        slot = jnp.sum(oh * (base_ref[:, 0:1] + cum), axis=0, keepdims=True)
        dest_ref[0] = slot.astype(jnp.int32)
        base_ref[...] += blk_cnt


def _meta(eid_flat, tile_rows):
    na = eid_flat.shape[0]
    nblk = na // META_BLK
    eid3 = eid_flat.reshape(nblk, 1, META_BLK)
    dest, tiles = pl.pallas_call(
        functools.partial(_meta_kernel, tile_rows=tile_rows),
        grid=(2, nblk),
        in_specs=[pl.BlockSpec((1, 1, META_BLK), lambda ph, r: (r, 0, 0))],
        out_specs=[pl.BlockSpec((1, 1, META_BLK), lambda ph, r: (ph * r, 0, 0)),
                   pl.BlockSpec((3, LANES), lambda ph, r: (0, 0))],
        out_shape=[jax.ShapeDtypeStruct((nblk, 1, META_BLK), jnp.int32),
                   jax.ShapeDtypeStruct((3, LANES), jnp.int32)],
        scratch_shapes=[pltpu.VMEM((N_EXPERTS, LANES), jnp.float32),
                        pltpu.VMEM((N_EXPERTS, LANES), jnp.float32)],
        compiler_params=_cparams(("arbitrary", "arbitrary")),
        name="meta",
    )(eid3)
    return dest.reshape(na), tiles.reshape(3 * LANES)


def _dispatch_kernel(dest_ref, seg_ref, u_ref, xs_ref, zero_ref, sem, zsem, *, n_tok, tile_rows):
    i = pl.program_id(0)
    slab = xs_ref.shape[1]
    tm = u_ref.shape[0] // slab
    base = i * tm

    @pl.when(i == 0)
    def _():
        zero_ref[...] = jnp.zeros_like(zero_ref)

        def fills(e, act):
            lo = seg_ref[LANES + e]
            pad = seg_ref[2 * LANES + e] - lo
            pos = lo
            for bit in reversed(range(tile_rows.bit_length() - 1)):
                rows = 1 << bit
                take = (pad >> bit) & 1

                @pl.when(take == 1)
                def _(pos=pos, rows=rows):
                    act(pltpu.make_async_copy(zero_ref.at[pl.ds(0, rows)], xs_ref.at[pl.ds(pos, rows)], zsem))

                pos = pos + take * rows

        def start_fills(e, c):
            fills(e, lambda cp: cp.start())
            return c

        def wait_fills(e, c):
            fills(e, lambda cp: cp.wait())
            return c

        run = zero_ref.shape[0]
        used = seg_ref[2 * LANES + N_EXPERTS - 1]
        n_runs = (xs_ref.shape[0] - used) // run

        def tail_fill(j):
            return pltpu.make_async_copy(zero_ref, xs_ref.at[pl.ds(used + j * run, run)], zsem)

        def start_tail(j, c):
            tail_fill(j).start()
            return c

        def wait_tail(j, c):
            tail_fill(j).wait()
            return c

        lax.fori_loop(0, N_EXPERTS, start_fills, 0)
        lax.fori_loop(0, n_runs, start_tail, 0)
        lax.fori_loop(0, N_EXPERTS, wait_fills, 0)
        lax.fori_loop(0, n_runs, wait_tail, 0)

    def row_copy(r, k):
        d = dest_ref[k * n_tok + base + r]
        return pltpu.make_async_copy(u_ref.at[pl.ds(r * slab, slab), :], xs_ref.at[d], sem)

    for r in range(tm):
        for k in range(TOPK_IN_GROUP):
            row_copy(r, k).start(priority=(r * TOPK_IN_GROUP + k) % DMA_PRIORITIES)

    def drain(r, c):
        for k in range(TOPK_IN_GROUP):
            row_copy(r, k).wait()
        return c

    lax.fori_loop(0, tm, drain, 0, unroll=8)


def _dispatch(dest, tiles, u2s, n_tok, n_slots, tile_rows):
    slab = u2s.shape[0] // n_tok
    tm = ROW_TM
    assert tile_rows & (tile_rows - 1) == 0
    return pl.pallas_call(
        functools.partial(_dispatch_kernel, n_tok=n_tok, tile_rows=tile_rows),
        grid_spec=pltpu.PrefetchScalarGridSpec(
            num_scalar_prefetch=2,
            grid=(n_tok // tm,),
            in_specs=[pl.BlockSpec((tm * slab, LANES), lambda i, dest, seg: (i, 0))],
            out_specs=pl.BlockSpec(memory_space=pl.ANY),
            scratch_shapes=[pltpu.VMEM((tile_rows // 2, slab, LANES), jnp.float32),
                            pltpu.SemaphoreType.DMA(()),
                            pltpu.SemaphoreType.DMA(())]),
        out_shape=jax.ShapeDtypeStruct((n_slots, slab, LANES), jnp.float32),
        compiler_params=_cparams(("arbitrary",)),
        name="dispatch",
    )(dest, tiles, u2s)


def _expert_kernel(tile_ref, x_ref, w1_ref, w3_ref, w2_ref, y_ref, wf1, wf3, wf2, w1b, w3b, w2b, wslot_ref, wsem):
    t = pl.program_id(0)
    tm = EXP_TM
    bf16 = jnp.bfloat16
    n_used = tile_ref[LANES - 1]
    e = tile_ref[t]
    changed = (t == 0) | (e != tile_ref[jnp.maximum(t - 1, 0)])
    active = t < n_used

    def weight_copies(ex, ws):
        return (pltpu.make_async_copy(w1_ref.at[ex], wf1.at[ws], wsem.at[ws, 0]),
                pltpu.make_async_copy(w3_ref.at[ex], wf3.at[ws], wsem.at[ws, 1]),
                pltpu.make_async_copy(w2_ref.at[ex], wf2.at[ws], wsem.at[ws, 2]))

    @pl.when(t == 0)
    def _():
        for cp in weight_copies(e, 0):
            cp.start()
        wslot_ref[0] = 0

    @pl.when(active & changed)
    def _():
        ws = wslot_ref[0]
        for cp in weight_copies(e, ws):
            cp.wait()
        nxt = lax.while_loop(lambda t2: (t2 < n_used) & (tile_ref[t2] == e), lambda t2: t2 + 1, t + 1)

        @pl.when(nxt < n_used)
        def _():
            for cp in weight_copies(tile_ref[nxt], 1 - ws):
                cp.start(priority=DMA_PRIORITIES - 1)

        w1b[...] = wf1[ws].astype(bf16)
        w3b[...] = wf3[ws].astype(bf16)
        w2b[...] = wf2[ws].astype(bf16)
        wslot_ref[0] = 1 - ws

    @pl.when(active)
    def _():
        xb = _load_slabs(x_ref, tm).astype(bf16)
        h1 = jnp.dot(xb, w1b[...], preferred_element_type=jnp.float32)
        h3 = jnp.dot(xb, w3b[...], preferred_element_type=jnp.float32)
        hh = (h1 * _sigmoid(h1) * h3).astype(bf16)
        _store_slabs(y_ref, jnp.dot(hh, w2b[...], preferred_element_type=jnp.float32))

    @pl.when(jnp.logical_not(active))
    def _():
        y_ref[...] = jnp.zeros_like(y_ref)


def _experts(tiles, xs2, w1, w3, w2, n_tiles):
    d, f = w1.shape[1], w1.shape[2]
    tm = EXP_TM
    slab = d // LANES
    any_spec = pl.BlockSpec(memory_space=pl.ANY)

    def x_idx(t, tiles):
        return (jnp.maximum(jnp.minimum(t, tiles[LANES - 1] - 1), 0), 0)

    return pl.pallas_call(
        _expert_kernel,
        grid_spec=pltpu.PrefetchScalarGridSpec(
            num_scalar_prefetch=1,
            grid=(n_tiles,),
            in_specs=[pl.BlockSpec((tm * slab, LANES), x_idx), any_spec, any_spec, any_spec],
            out_specs=pl.BlockSpec((tm * slab, LANES), lambda t, tiles: (t, 0)),
            scratch_shapes=[pltpu.VMEM((2, d, f), jnp.float32),
                            pltpu.VMEM((2, d, f), jnp.float32),
                            pltpu.VMEM((2, f, d), jnp.float32),
                            pltpu.VMEM((d, f), jnp.bfloat16),
                            pltpu.VMEM((d, f), jnp.bfloat16),
                            pltpu.VMEM((f, d), jnp.bfloat16),
                            pltpu.SMEM((1,), jnp.int32),
                            pltpu.SemaphoreType.DMA((2, 3))]),
        out_shape=jax.ShapeDtypeStruct((n_tiles * tm * slab, LANES), jnp.float32),
        compiler_params=_cparams(("arbitrary",)),
        name="experts",
    )(tiles, xs2, w1, w3, w2)


def _combine_kernel(dest_ref, ys_ref, x1_ref, rw_ref, gt_ref, g2_ref, b2_ref, o_ref, ybuf0, ybuf1, sem,
                    *, stride, alpha):
    i = pl.program_id(0)
    tm = x1_ref.shape[0]
    slab = ys_ref.shape[1]
    ybufs = (ybuf0, ybuf1)

    def row_copy(slot, buf, k, r):
        return pltpu.make_async_copy(ys_ref.at[slot], ybufs[buf].at[k, pl.ds(r * slab, slab), :], sem.at[buf])

    def gather(step, buf):
        for r in range(tm):
            for k in range(TOPK_IN_GROUP):
                row_copy(dest_ref[k * stride + step * tm + r], buf, k, r).start(
                    priority=(r * TOPK_IN_GROUP + k) % DMA_PRIORITIES)

    def gather_wait(buf):
        def one(r, c):
            for k in range(TOPK_IN_GROUP):
                row_copy(0, buf, k, r).wait()
            return c

        lax.fori_loop(0, tm, one, 0, unroll=8)

    @pl.when(i == 0)
    def _():
        gather(0, 0)

    for buf in range(2):
        @pl.when(i % 2 == buf)
        def _(buf=buf):
            gather_wait(buf)
            gather(i + 1, 1 - buf)
            rw = rw_ref[...]
            f = (rw[:, 0:1] * _load_slabs(ybufs[buf].at[0], tm)
                 + rw[:, 1:2] * _load_slabs(ybufs[buf].at[1], tm))
            y = _layer_norm_rows(alpha * x1_ref[...] + (1.0 + gt_ref[0]) * f)
            o_ref[...] = y * g2_ref[...] + b2_ref[...]

        @pl.when((i == pl.num_programs(0) - 1) & (i % 2 == buf))
        def _(buf=buf):
            gather_wait(1 - buf)


def _combine(dest, ys, x1, rw, gt2, g2, b2, seq, alpha):
    n, d = x1.shape
    tm = ROW_TM
    per_batch = seq // tm
    dest_pad = jnp.concatenate([dest.reshape(TOPK_IN_GROUP, n), jnp.zeros((TOPK_IN_GROUP, tm), jnp.int32)],
                               axis=1).reshape(TOPK_IN_GROUP * (n + tm))
    ybuf = pltpu.VMEM((TOPK_IN_GROUP, tm * (d // LANES), LANES), jnp.float32)
    return pl.pallas_call(
        functools.partial(_combine_kernel, stride=n + tm, alpha=alpha),
        grid_spec=pltpu.PrefetchScalarGridSpec(
            num_scalar_prefetch=1,
            grid=(n // tm,),
            in_specs=[pl.BlockSpec(memory_space=pl.ANY),
                      pl.BlockSpec((tm, d), lambda i, dest: (i, 0)),
                      pl.BlockSpec((tm, LANES), lambda i, dest: (i, 0)),
                      pl.BlockSpec((1, 1, d), lambda i, dest: (i // per_batch, 0, 0)),
                      pl.BlockSpec((1, d), lambda i, dest: (0, 0)),
                      pl.BlockSpec((1, d), lambda i, dest: (0, 0))],
            out_specs=pl.BlockSpec((tm, d), lambda i, dest: (i, 0)),
            scratch_shapes=[ybuf, ybuf, pltpu.SemaphoreType.DMA((2,))]),
        out_shape=jax.ShapeDtypeStruct((n, d), jnp.float32),
        compiler_params=_cparams(("arbitrary",)),
        name="combine",
    )(dest_pad, ys, x1, rw, gt2, g2, b2)


def kernel(x, c, positions, w_cond, b_cond, w_in, b_glu, w_dw, b_dw, g_cn, b_cn, w_conv_out, b_conv_out, w_attn_out, w_mix_out, g_ln1, b_ln1, w_grp, b_grp, w_erouter, b_erouter, w1, w3, w2, g_ln2, b_ln2):
    nb, seq, d = x.shape
    depth = w_cond.shape[0]
    n = nb * seq
    ch = w_dw.shape[2]
    aw = N_HEADS * HEAD_DIM
    in_cols = w_in.shape[2]
    alpha = (2.0 * depth) ** 0.25
    bf16 = jnp.bfloat16
    assert 2 * ch + 3 * aw + 2 * d == in_cols and ch == PROJ_TN and aw == PROJ_TN and d == 2 * PROJ_TN
    assert seq % MOBA_BLOCK == 0 and seq // MOBA_BLOCK <= LANES and N_GROUPS + N_EXPERTS <= LANES
    t_a, t_b, t_q, t_k, t_v, t_gc, t_ga = 0, 1, 2, 3, 4, 5, 7
    heads_per_tile = PROJ_TN // HEAD_DIM
    nblk = seq // MOBA_BLOCK
    n_tiles = (n * TOPK_IN_GROUP) // EXP_TM + N_EXPERTS
    assert (n * TOPK_IN_GROUP) % EXP_TM == 0 and n_tiles < LANES

    cos_t, sin_t = _rope_tables(positions)
    x2 = x.reshape(n, d)
    for l in range(depth):
        mod = _mod(c, w_cond[l], b_cond[l]).reshape(nb, 6, 1, d)
        sh1, sc1, gt1, sh2, sc2, gt2 = (mod[:, s] for s in range(6))

        bias = jnp.concatenate([b_glu[l], jnp.zeros((in_cols - 2 * ch,), jnp.float32)]).reshape(1, in_cols)
        z, km = _in_proj(x2, sc1, sh1, w_in[l], bias, cos_t, sin_t, seq,
                         rope_tiles=(t_q, t_k + 1), sig_from=t_gc)
        km = km.reshape(nb, nblk, aw)
        qx = _gate(z, km, nb, seq, t_q)
        o_attn = _attention(qx, z, nb, seq, t_k * heads_per_tile, t_v * heads_per_tile)
        hc = _conv(z, w_dw[l], b_dw[l], g_cn[l], b_cn[l], nb, seq, t_a, t_b)

        w_r = jnp.concatenate([w_grp[l], w_erouter[l],
                               jnp.zeros((d, LANES - N_GROUPS - N_EXPERTS), jnp.float32)], axis=1)
        b_r = jnp.concatenate([b_grp[l], b_erouter[l],
                               jnp.zeros((LANES - N_GROUPS - N_EXPERTS,), jnp.float32)]).reshape(1, LANES)
        x1, u2, ri, rw = _mix(hc, o_attn, z, x2, w_conv_out[l].astype(bf16), b_conv_out[l].reshape(1, d),
                              w_attn_out[l].astype(bf16), w_mix_out[l].astype(bf16), gt1,
                              g_ln1[l].reshape(1, d), b_ln1[l].reshape(1, d), sc2, sh2, w_r, b_r,
                              seq, t_gc, t_ga, alpha)

        eid = ri[:, :TOPK_IN_GROUP].T.reshape(TOPK_IN_GROUP * n)
        dest, tiles = _meta(eid, EXP_TM)
        f_in = w1.shape[-1]
        slab = d // LANES
        xs = _dispatch(dest, tiles, u2, n, n_tiles * EXP_TM, EXP_TM)
        ys = _experts(tiles, xs.reshape(n_tiles * EXP_TM * slab, LANES), w1[l].reshape(N_EXPERTS, d, f_in),
                      w3[l].reshape(N_EXPERTS, d, f_in), w2[l].reshape(N_EXPERTS, f_in, d), n_tiles)
        x2 = _combine(dest, ys.reshape(n_tiles * EXP_TM, slab, LANES), x1, rw, gt2, g_ln2[l].reshape(1, d), b_ln2[l].reshape(1, d), seq, alpha)
    return x2.reshape(nb, seq, d)
```

```python
import functools

import jax
import jax.numpy as jnp
from jax import lax
from jax.experimental import pallas as pl
from jax.experimental.pallas import tpu as pltpu

CONV_WIDTH = 31
N_HEADS = 8
HEAD_DIM = 128
ROPE_DIM = HEAD_DIM // 4
ROPE_THETA = 500000.0
MOBA_BLOCK = 256
MOBA_TOPK = 3
N_GROUPS = 4
EXPERTS_PER_GROUP = 8
N_EXPERTS = N_GROUPS * EXPERTS_PER_GROUP
TOPK_IN_GROUP = 2
LN_EPS = 1e-5
NEG_INF = -1e30

LANES = 128
SUBLANES = 8
DMA_PRIORITIES = 2
VMEM_LIMIT = 56 * 1024 * 1024
ATTN_VMEM_LIMIT = 60 * 1024 * 1024

PROJ_TM = 1024
PROJ_TN = 1024
PROJ_CHUNK = 256
CONV_TT = 256
CONV_HALO = 32
ATTN_GROUP = 4
ATTN_HEADS = 4
ATTN_STRIP = 128
MIX_TM = 256
EXP_TM = 256
ROW_TM = 256
META_BLK = 1024


def _cparams(sem, vmem=VMEM_LIMIT, flags=None):
    return pltpu.CompilerParams(dimension_semantics=sem, vmem_limit_bytes=vmem, flags=flags)


def _layer_norm_rows(x):
    mu = jnp.mean(x, axis=-1, keepdims=True)
    xc = x - mu
    var = jnp.mean(xc * xc, axis=-1, keepdims=True)
    return xc * lax.rsqrt(var + LN_EPS)


def _sigmoid(x):
    return 1.0 / (1.0 + jnp.exp(-x))


def _store_slabs(ref, val):
    m, w = val.shape
    c = w // LANES
    for k in range(c):
        ref[pl.ds(k, m, stride=c), :] = val[:, k * LANES:(k + 1) * LANES]


def _load_slabs(ref, m):
    c = ref.shape[0] // m
    return jnp.concatenate([ref[pl.ds(k, m, stride=c), :] for k in range(c)], axis=1)


def _dot_nt(a, b):
    return lax.dot_general(a, b, (((1,), (1,)), ((), ())), preferred_element_type=jnp.float32)


def _mod_kernel(cb_ref, w_ref, b_ref, o_ref, s_ref):
    @pl.when(pl.program_id(0) == 0)
    def _():
        cb = cb_ref[...]
        s_ref[...] = cb * _sigmoid(cb)

    nb, d, _ = cb_ref.shape
    tn = w_ref.shape[1]
    for cc in range(tn // LANES):
        sl = slice(cc * LANES, (cc + 1) * LANES)
        wc = w_ref[:, sl]
        for b in range(nb):
            p = (wc * s_ref[b]).reshape(d // SUBLANES, SUBLANES, LANES)
            r = jnp.sum(jnp.sum(p, axis=0), axis=0, keepdims=True)
            o_ref[b:b + 1, sl] = r + b_ref[:, sl]


def _mod(c, w_cond, b_cond, tn=1024):
    nb, d = c.shape
    n_out = w_cond.shape[1]
    cb = jnp.broadcast_to(c[:, :, None], (nb, d, LANES))
    return pl.pallas_call(
        _mod_kernel,
        grid=(n_out // tn,),
        in_specs=[pl.BlockSpec((nb, d, LANES), lambda j: (0, 0, 0)),
                  pl.BlockSpec((d, tn), lambda j: (0, j)),
                  pl.BlockSpec((1, tn), lambda j: (0, j))],
        out_specs=pl.BlockSpec((nb, tn), lambda j: (0, j)),
        out_shape=jax.ShapeDtypeStruct((nb, n_out), jnp.float32),
        scratch_shapes=[pltpu.VMEM((nb, d, LANES), jnp.float32)],
        compiler_params=_cparams(("arbitrary",)),
        name="mod",
    )(cb, w_cond, b_cond.reshape(1, n_out))


def _rope_tab_kernel(pos_ref, invf_ref, sign_ref, cos_ref, sin_ref):
    pack = LANES // ROPE_DIM
    ang = pos_ref[...] * invf_ref[...]
    cos_p = jnp.cos(ang)
    sin_p = jnp.sin(ang) * sign_ref[...]
    rows = ang.shape[0]
    rotated = lax.broadcasted_iota(jnp.int32, ang.shape, 1) < ROPE_DIM
    for t in range(pack):
        shift = (LANES - t * ROPE_DIM) % LANES
        c = pltpu.roll(cos_p, shift, 1) if shift else cos_p
        s = pltpu.roll(sin_p, shift, 1) if shift else sin_p
        cos_ref[pl.ds(t, rows, stride=pack), :] = jnp.where(rotated, c, 1.0)
        sin_ref[pl.ds(t, rows, stride=pack), :] = jnp.where(rotated, s, 0.0)


def _rope_tables(positions, tm=1024):
    n = positions.size
    half = ROPE_DIM // 2
    pack = LANES // ROPE_DIM
    inv_freq = jnp.power(ROPE_THETA, -jnp.arange(half, dtype=jnp.float32) / half)
    invf = jnp.tile(jnp.concatenate([inv_freq, inv_freq]), pack).reshape(1, LANES)
    sign = jnp.tile(jnp.concatenate([-jnp.ones((half,), jnp.float32), jnp.ones((half,), jnp.float32)]),
                    pack).reshape(1, LANES)
    posb = jnp.repeat(positions.astype(jnp.float32).reshape(n // pack, pack), ROPE_DIM, axis=1)
    vec = pl.BlockSpec((1, LANES), lambda i: (0, 0))
    tab = pl.BlockSpec((tm, LANES), lambda i: (i, 0))
    return pl.pallas_call(
        _rope_tab_kernel,
        grid=(n // tm,),
        in_specs=[pl.BlockSpec((tm // pack, LANES), lambda i: (i, 0)), vec, vec],
        out_specs=[tab, tab],
        out_shape=[jax.ShapeDtypeStruct((n, LANES), jnp.float32)] * 2,
        compiler_params=_cparams(("parallel",)),
        name="rope_tab",
    )(posb, invf, sign)


def _in_proj_kernel(x_ref, sc_ref, sh_ref, w_ref, bias_ref, cos_ref, sin_ref, o_ref, km_ref, u_ref,
                    *, rope_tiles, sig_from):
    j = pl.program_id(1)

    @pl.when(j == 0)
    def _():
        xn = _layer_norm_rows(x_ref[...])
        u_ref[...] = (xn * (1.0 + sc_ref[0]) + sh_ref[0]).astype(u_ref.dtype)

    is_rope = (j >= rope_tiles[0]) & (j < rope_tiles[1])

    def project(epilogue, block_means=False):
        for n in range(w_ref.shape[1] // PROJ_CHUNK):
            cs = slice(n * PROJ_CHUNK, (n + 1) * PROJ_CHUNK)
            acc = jnp.dot(u_ref[...], w_ref[:, cs].astype(u_ref.dtype),
                          preferred_element_type=jnp.float32) + bias_ref[:, cs]
            out = epilogue(acc).astype(o_ref.dtype)
            o_ref[:, cs] = out
            if block_means:
                stored = out.astype(jnp.float32)
                for blk in range(km_ref.shape[0]):
                    rows = stored[blk * MOBA_BLOCK:(blk + 1) * MOBA_BLOCK]
                    km_ref[blk, :, cs] = (jnp.sum(rows, axis=0, keepdims=True)
                                          * (1.0 / MOBA_BLOCK)).astype(km_ref.dtype)

    @pl.when(is_rope)
    def _():
        cosf = cos_ref[...]
        sinf = sin_ref[...]
        lane = lax.broadcasted_iota(jnp.int32, cosf.shape, 1)
        first = lane < (ROPE_DIM // 2)

        def rope(acc):
            heads = []
            for h in range(acc.shape[1] // HEAD_DIM):
                xh = acc[:, h * HEAD_DIM:(h + 1) * HEAD_DIM]
                partner = jnp.where(first, pltpu.roll(xh, HEAD_DIM - ROPE_DIM // 2, 1),
                                    pltpu.roll(xh, ROPE_DIM // 2, 1))
                heads.append(xh * cosf + partner * sinf)
            return jnp.concatenate(heads, axis=1)

        project(rope, block_means=True)

    @pl.when(j >= sig_from)
    def _():
        project(_sigmoid)

    @pl.when(jnp.logical_not(is_rope) & (j < sig_from))
    def _():
        project(lambda acc: acc)


def _in_proj(x2, sc1, sh1, w_in, bias, cos_t, sin_t, seq, rope_tiles, sig_from):
    n, d = x2.shape
    cols = w_in.shape[1]
    tm, tn = PROJ_TM, PROJ_TN
    per_batch = seq // tm
    kern = functools.partial(_in_proj_kernel, rope_tiles=rope_tiles, sig_from=sig_from)
    return pl.pallas_call(
        kern,
        grid=(n // tm, cols // tn),
        in_specs=[pl.BlockSpec((tm, d), lambda i, j: (i, 0)),
                  pl.BlockSpec((1, 1, d), lambda i, j: (i // per_batch, 0, 0)),
                  pl.BlockSpec((1, 1, d), lambda i, j: (i // per_batch, 0, 0)),
                  pl.BlockSpec((d, tn), lambda i, j: (0, j)),
                  pl.BlockSpec((1, tn), lambda i, j: (0, j)),
                  pl.BlockSpec((tm, LANES), lambda i, j: (i, 0)),
                  pl.BlockSpec((tm, LANES), lambda i, j: (i, 0))],
        out_specs=[pl.BlockSpec((tm, tn), lambda i, j: (i, j)),
                   pl.BlockSpec((tm // MOBA_BLOCK, 1, tn), lambda i, j: (i, 0, 0))],
        out_shape=[jax.ShapeDtypeStruct((n, cols), jnp.bfloat16),
                   jax.ShapeDtypeStruct((n // MOBA_BLOCK, 1, tn), jnp.bfloat16)],
        scratch_shapes=[pltpu.VMEM((tm, d), jnp.bfloat16)],
        compiler_params=_cparams(("parallel", "arbitrary")),
        name="in_proj",
    )(x2, sc1, sh1, w_in, bias, cos_t, sin_t)


def _gate_kernel(q_ref, km_ref, qx_ref):
    i = pl.program_id(1)
    bs = q_ref.shape[0]
    nblk = km_ref.shape[1]
    sub = lax.broadcasted_iota(jnp.int32, (nblk, bs), 0)
    past = sub < i
    for h in range(N_HEADS):
        hs = slice(h * HEAD_DIM, (h + 1) * HEAD_DIM)
        q = q_ref[:, hs]
        g = jnp.where(past, _dot_nt(km_ref[0, :, hs], q), -jnp.inf)
        attend = sub == i
        for _ in range(MOBA_TOPK):
            m = jnp.max(g, axis=0, keepdims=True)
            idx = jnp.min(jnp.where(g == m, sub, nblk), axis=0, keepdims=True)
            hit = sub == idx
            attend = attend | (hit & past)
            g = jnp.where(hit, -jnp.inf, g)
        att = jnp.concatenate([jnp.where(attend, 1.0, 0.0), jnp.ones((LANES - nblk, bs), jnp.float32)], axis=0)
        bias = jnp.where(att.T > 0.5, 0.0, NEG_INF)
        qx_ref[:, 2 * h * HEAD_DIM:(2 * h + 1) * HEAD_DIM] = q
        qx_ref[:, (2 * h + 1) * HEAD_DIM:(2 * h + 2) * HEAD_DIM] = bias.astype(qx_ref.dtype)


def _gate(z, kmean, nb, seq, q_tile):
    n = z.shape[0]
    nq = seq // MOBA_BLOCK
    aw = N_HEADS * HEAD_DIM
    return pl.pallas_call(
        _gate_kernel,
        grid=(nb, nq),
        in_specs=[pl.BlockSpec((MOBA_BLOCK, aw), lambda b, i: (b * nq + i, q_tile)),
                  pl.BlockSpec((1, nq, aw), lambda b, i: (b, 0, 0))],
        out_specs=pl.BlockSpec((MOBA_BLOCK, 2 * aw), lambda b, i: (b * nq + i, 0)),
        out_shape=jax.ShapeDtypeStruct((n, 2 * aw), jnp.bfloat16),
        compiler_params=_cparams(("parallel", "arbitrary")),
        name="gate",
    )(z, kmean)


def _attn_kernel(qx_ref, k_ref, v_ref, o_ref, kx_ref, vx_ref, s_ref, p_ref):
    i = pl.program_id(2)
    bs = MOBA_BLOCK
    nblk = k_ref.shape[0] // bs
    wide = HEAD_DIM + LANES
    bf16 = jnp.bfloat16
    f32 = jnp.float32
    exp2_scale = (HEAD_DIM ** -0.5) * 1.4426950408889634
    grp = ATTN_GROUP * bs

    @pl.when(i == 0)
    def _():
        sub_k = lax.broadcasted_iota(jnp.int32, (LANES, bs), 0)
        for hh in range(ATTN_HEADS):
            hs = slice(hh * HEAD_DIM, (hh + 1) * HEAD_DIM)
            vx_ref[hh, :, :HEAD_DIM] = v_ref[:, hs]
            vx_ref[hh, :, HEAD_DIM:] = jnp.ones((nblk * bs, LANES), bf16)
            for jb in range(nblk):
                cs = slice(jb * bs, (jb + 1) * bs)
                kx_ref[hh, :HEAD_DIM, cs] = k_ref[cs, hs].astype(f32).T.astype(bf16)
                kx_ref[hh, HEAD_DIM:, cs] = jnp.where(sub_k == jb, 1.0, 0.0).astype(bf16)

    for c in range(nblk // ATTN_GROUP):
        @pl.when(i // ATTN_GROUP == c)
        def _(c=c):
            nk = (c + 1) * grp
            row = lax.broadcasted_iota(jnp.int32, (ATTN_STRIP, LANES), 0) + i * bs
            col = lax.broadcasted_iota(jnp.int32, (ATTN_STRIP, LANES), 1)
            for hh in range(ATTN_HEADS):
                s_ref[hh, :, :nk] = jnp.dot(qx_ref[:, hh * wide:(hh + 1) * wide], kx_ref[hh, :, 0:nk],
                                            preferred_element_type=f32)
            for hh in range(ATTN_HEADS):
                for r0 in range(0, bs, ATTN_STRIP):
                    rs = slice(r0, r0 + ATTN_STRIP)
                    mx = None
                    for t in range(nk // LANES):
                        ts = slice(t * LANES, (t + 1) * LANES)
                        st = s_ref[hh, rs, ts]
                        if t * LANES >= c * grp:
                            st = jnp.where(col + t * LANES <= row + r0, st, NEG_INF)
                            s_ref[hh, rs, ts] = st
                        mx = st if mx is None else jnp.maximum(mx, st)
                    mb = jnp.broadcast_to(jnp.max(mx, axis=-1, keepdims=True) * exp2_scale, (ATTN_STRIP, LANES))
                    for t in range(nk // LANES):
                        ts = slice(t * LANES, (t + 1) * LANES)
                        p_ref[hh, rs, ts] = jnp.exp2(s_ref[hh, rs, ts] * exp2_scale - mb).astype(bf16)
            for hh in range(ATTN_HEADS):
                acc = jnp.dot(p_ref[hh, :, :nk], vx_ref[hh, 0:nk, :], preferred_element_type=f32)
                o_ref[:, hh * HEAD_DIM:(hh + 1) * HEAD_DIM] = (
                    acc[:, :HEAD_DIM] / acc[:, HEAD_DIM:]).astype(o_ref.dtype)


def _attention(qx, z, nb, seq, k_col, v_col):
    n = z.shape[0]
    nq = seq // MOBA_BLOCK
    bs = MOBA_BLOCK
    wide = HEAD_DIM + LANES
    hps = ATTN_HEADS
    assert N_HEADS % hps == 0 and k_col % hps == 0 and v_col % hps == 0
    return pl.pallas_call(
        _attn_kernel,
        grid=(nb, N_HEADS // hps, nq),
        in_specs=[pl.BlockSpec((bs, hps * wide), lambda b, h, i: (b * nq + i, h)),
                  pl.BlockSpec((seq, hps * HEAD_DIM), lambda b, h, i: (b, k_col // hps + h),
                               pipeline_mode=pl.Buffered(1)),
                  pl.BlockSpec((seq, hps * HEAD_DIM), lambda b, h, i: (b, v_col // hps + h),
                               pipeline_mode=pl.Buffered(1))],
        out_specs=pl.BlockSpec((bs, hps * HEAD_DIM), lambda b, h, i: (b * nq + i, h)),
        out_shape=jax.ShapeDtypeStruct((n, N_HEADS * HEAD_DIM), jnp.bfloat16),
        scratch_shapes=[pltpu.VMEM((hps, wide, seq), jnp.bfloat16),
                        pltpu.VMEM((hps, seq, wide), jnp.bfloat16),
                        pltpu.VMEM((hps, bs, seq), jnp.float32),
                        pltpu.VMEM((hps, bs, seq), jnp.bfloat16)],
        compiler_params=_cparams(("parallel", "parallel", "arbitrary"), vmem=ATTN_VMEM_LIMIT),
        name="attn",
    )(qx, z, z)


def _conv_kernel(a_ref, b_ref, ha_ref, hb_ref, w_ref, bdw_ref, g_ref, bcn_ref, o_ref, hext_ref, grp_ref):
    i = pl.program_id(1)
    tt = a_ref.shape[0]
    halo = ha_ref.shape[0]
    ch = a_ref.shape[1]
    glu_h = ha_ref[...].astype(jnp.float32) * _sigmoid(hb_ref[...].astype(jnp.float32))
    hext_ref[0:halo, :] = jnp.where(i > 0, glu_h, 0.0)
    hext_ref[halo:halo + tt, :] = a_ref[...].astype(jnp.float32) * _sigmoid(b_ref[...].astype(jnp.float32))
    hext_ref[halo + tt:, :] = jnp.zeros((SUBLANES, ch), jnp.float32)

    first = halo - (CONV_WIDTH - 1)
    acc = jnp.zeros(o_ref.shape, jnp.float32) + bdw_ref[...]
    for s in range(SUBLANES):
        g = None
        for j in range(CONV_WIDTH):
            off = first + j
            if off % SUBLANES != s:
                continue
            term = w_ref[j:j + 1, :] * hext_ref[pl.ds(off - s, tt + SUBLANES), :]
            g = term if g is None else g + term
        if g is None:
            continue
        if s == 0:
            acc = acc + g[:tt]
        else:
            grp_ref[s] = g
            acc = acc + grp_ref[s, pl.ds(s, tt), :]
    y = _layer_norm_rows(acc) * g_ref[...] + bcn_ref[...]
    o_ref[...] = (y * _sigmoid(y)).astype(o_ref.dtype)


def _conv(z, w_dw, b_dw, g_cn, b_cn, nb, seq, a_tile, b_tile):
    n = z.shape[0]
    ch = w_dw.shape[1]
    tt, halo = CONV_TT, CONV_HALO
    nt = seq // tt
    r = tt // halo

    def cur(tile):
        return pl.BlockSpec((tt, ch), lambda b, i: (b * nt + i, tile))

    def prev(tile):
        return pl.BlockSpec((halo, ch), lambda b, i: (jnp.maximum((b * nt + i) * r - 1, 0), tile))

    vec = pl.BlockSpec((1, ch), lambda b, i: (0, 0))
    return pl.pallas_call(
        _conv_kernel,
        grid=(nb, nt),
        in_specs=[cur(a_tile), cur(b_tile), prev(a_tile), prev(b_tile),
                  pl.BlockSpec((CONV_WIDTH, ch), lambda b, i: (0, 0)), vec, vec, vec],
        out_specs=pl.BlockSpec((tt, ch), lambda b, i: (b * nt + i, 0)),
        out_shape=jax.ShapeDtypeStruct((n, ch), jnp.bfloat16),
        scratch_shapes=[pltpu.VMEM((halo + tt + SUBLANES, ch), jnp.float32),
                        pltpu.VMEM((SUBLANES, tt + SUBLANES, ch), jnp.float32)],
        compiler_params=_cparams(("parallel", "parallel")),
        name="conv",
    )(z, z, z, z, w_dw, b_dw.reshape(1, ch), g_cn.reshape(1, ch), b_cn.reshape(1, ch))


def _mix_kernel(hc_ref, o_ref, sgc0_ref, sgc1_ref, sga0_ref, sga1_ref, x_ref, wco_ref, bco_ref, wao_ref, wmx_ref,
                gt_ref, g1_ref, b1_ref, sc_ref, sh_ref, wr_ref, br_ref,
                x1_ref, u2_ref, ri_ref, rw_ref, *, alpha):
    f32 = jnp.float32
    def cols(a, w_ref):
        return jnp.concatenate(
            [jnp.dot(a, w_ref[:, n * PROJ_CHUNK:(n + 1) * PROJ_CHUNK].astype(jnp.bfloat16),
                     preferred_element_type=f32) for n in range(w_ref.shape[1] // PROJ_CHUNK)], axis=1)

    y_conv = cols(hc_ref[...], wco_ref) + bco_ref[...]
    y_attn = cols(o_ref[...], wao_ref)
    sgc = jnp.concatenate([sgc0_ref[...], sgc1_ref[...]], axis=1).astype(f32)
    sga = jnp.concatenate([sga0_ref[...], sga1_ref[...]], axis=1).astype(f32)
    merged = sgc * y_conv + sga * y_attn
    t_out = cols(merged.astype(jnp.bfloat16), wmx_ref)
    x1 = _layer_norm_rows(alpha * x_ref[...] + (1.0 + gt_ref[0]) * t_out) * g1_ref[...] + b1_ref[...]
    x1_ref[...] = x1
    u2 = _layer_norm_rows(x1) * (1.0 + sc_ref[0]) + sh_ref[0]
    _store_slabs(u2_ref, u2)

    bf16 = jnp.bfloat16
    u_hi = u2.astype(bf16)
    u_lo = (u2 - u_hi.astype(f32)).astype(bf16)
    w_hi = wr_ref[...].astype(bf16)
    w_lo = (wr_ref[...] - w_hi.astype(f32)).astype(bf16)
    hi_terms = jnp.dot(u_hi, jnp.concatenate([w_hi, w_lo], axis=1), preferred_element_type=f32)
    logits = (hi_terms[:, :LANES] + hi_terms[:, LANES:]
              + jnp.dot(u_lo, w_hi, preferred_element_type=f32) + br_ref[...])
    lane = lax.broadcasted_iota(jnp.int32, logits.shape, 1)
    is_grp = lane < N_GROUPS
    gl = jnp.where(is_grp, logits, -jnp.inf)
    gmax = jnp.max(gl, axis=-1, keepdims=True)
    gidx = jnp.min(jnp.where(gl == gmax, lane, LANES), axis=-1, keepdims=True)
    p_top = 1.0 / jnp.sum(jnp.where(is_grp, jnp.exp(logits - gmax), 0.0), axis=-1, keepdims=True)
    lo = N_GROUPS + gidx * EXPERTS_PER_GROUP
    el = jnp.where((lane >= lo) & (lane < lo + EXPERTS_PER_GROUP), logits, -jnp.inf)
    v1 = jnp.max(el, axis=-1, keepdims=True)
    i1 = jnp.min(jnp.where(el == v1, lane, LANES), axis=-1, keepdims=True)
    el = jnp.where(lane == i1, -jnp.inf, el)
    v2 = jnp.max(el, axis=-1, keepdims=True)
    i2 = jnp.min(jnp.where(el == v2, lane, LANES), axis=-1, keepdims=True)
    e2 = jnp.exp(v2 - v1)
    w1 = p_top / (1.0 + e2)
    w2 = p_top * e2 / (1.0 + e2)
    ri_ref[...] = jnp.where(lane == 0, i1 - N_GROUPS, jnp.where(lane == 1, i2 - N_GROUPS, 0))
    rw_ref[...] = jnp.where(lane == 0, w1, jnp.where(lane == 1, w2, 0.0))


def _mix(hc, o_attn, z, x2, wco, bco, wao, wmx, gt1, g1, b1, sc2, sh2, w_r, b_r, seq, gc_tile, ga_tile, alpha):
    n, d = x2.shape
    ch = hc.shape[1]
    tm = MIX_TM
    per_batch = seq // tm

    def rows(width):
        return pl.BlockSpec((tm, width), lambda i: (i, 0))

    def const(shape):
        return pl.BlockSpec(shape, lambda i: (0,) * len(shape), pipeline_mode=pl.Buffered(1))

    def per_b():
        return pl.BlockSpec((1, 1, d), lambda i: (i // per_batch, 0, 0))

    return pl.pallas_call(
        functools.partial(_mix_kernel, alpha=alpha),
        grid=(n // tm,),
        in_specs=[rows(ch), rows(ch),
                  pl.BlockSpec((tm, ch), lambda i: (i, gc_tile)),
                  pl.BlockSpec((tm, ch), lambda i: (i, gc_tile + 1)),
                  pl.BlockSpec((tm, ch), lambda i: (i, ga_tile)),
                  pl.BlockSpec((tm, ch), lambda i: (i, ga_tile + 1)),
                  rows(d),
                  const((ch, d)), const((1, d)), const((ch, d)), const((d, d)),
                  per_b(), const((1, d)), const((1, d)), per_b(), per_b(),
                  const((d, LANES)), const((1, LANES))],
        out_specs=[rows(d), pl.BlockSpec((tm * (d // LANES), LANES), lambda i: (i, 0)), rows(LANES), rows(LANES)],
        out_shape=[jax.ShapeDtypeStruct((n, d), jnp.float32),
                   jax.ShapeDtypeStruct((n * (d // LANES), LANES), jnp.float32),
                   jax.ShapeDtypeStruct((n, LANES), jnp.int32),
                   jax.ShapeDtypeStruct((n, LANES), jnp.float32)],
        compiler_params=_cparams(("parallel",)),
        name="mix",
    )(hc, o_attn, z, z, z, z, x2, wco, bco, wao, wmx, gt1, g1, b1, sc2, sh2, w_r, b_r)


def _meta_kernel(eid_ref, dest_ref, tile_ref, cnt_ref, base_ref, *, tile_rows):
    ph = pl.program_id(0)
    r = pl.program_id(1)
    f32 = jnp.float32
    blk = eid_ref.shape[2]
    e = eid_ref[0]
    sub = lax.broadcasted_iota(jnp.int32, (N_EXPERTS, blk), 0)
    oh = (sub == e).astype(f32)
    blk_cnt = jnp.broadcast_to(jnp.sum(oh, axis=1, keepdims=True), (N_EXPERTS, LANES))

    @pl.when((ph == 0) & (r == 0))
    def _():
        cnt_ref[...] = jnp.zeros_like(cnt_ref)

    @pl.when(ph == 0)
    def _():
        cnt_ref[...] += blk_cnt

    @pl.when((ph == 1) & (r == 0))
    def _():
        padded = jnp.ceil(cnt_ref[...] * (1.0 / tile_rows)) * tile_rows
        er = lax.broadcasted_iota(jnp.int32, (N_EXPERTS, N_EXPERTS), 0)
        ec = lax.broadcasted_iota(jnp.int32, (N_EXPERTS, N_EXPERTS), 1)
        starts = jnp.dot((ec < er).astype(f32), padded, preferred_element_type=f32,
                         precision=lax.Precision.HIGHEST)
        base_ref[...] = starts
        ends = starts + padded
        lane = lax.broadcasted_iota(jnp.int32, (N_EXPERTS, LANES), 1)
        tile_start = (lane * tile_rows).astype(f32)
        owner = jnp.sum((ends <= tile_start).astype(f32), axis=0, keepdims=True)
        total_tiles = jnp.max(ends, axis=0, keepdims=True) * (1.0 / tile_rows)
        lane1 = lax.broadcasted_iota(jnp.int32, (1, LANES), 1)
        tile_ref[0:1, :] = jnp.where(lane1 == LANES - 1, total_tiles, owner).astype(jnp.int32)
        sub_e = lax.broadcasted_iota(jnp.int32, (N_EXPERTS, LANES), 0)
        diag = sub_e == lane
        tile_ref[1:2, :] = jnp.sum(jnp.where(diag, starts + cnt_ref[...], 0.0), axis=0,
                                   keepdims=True).astype(jnp.int32)
        tile_ref[2:3, :] = jnp.sum(jnp.where(diag, ends, 0.0), axis=0, keepdims=True).astype(jnp.int32)

    @pl.when(ph == 1)
    def _():
        ar = lax.broadcasted_iota(jnp.int32, (blk, blk), 0)
        ac = lax.broadcasted_iota(jnp.int32, (blk, blk), 1)
        before = (ar < ac).astype(jnp.bfloat16)
        cum = jnp.dot(oh.astype(jnp.bfloat16), before, preferred_element_type=f32)
        slot = jnp.sum(oh * (base_ref[:, 0:1] + cum), axis=0, keepdims=True)
        dest_ref[0] = slot.astype(jnp.int32)
        base_ref[...] += blk_cnt


def _meta(eid_flat, tile_rows):
    na = eid_flat.shape[0]
    nblk = na // META_BLK
    eid3 = eid_flat.reshape(nblk, 1, META_BLK)
    dest, tiles = pl.pallas_call(
        functools.partial(_meta_kernel, tile_rows=tile_rows),
        grid=(2, nblk),
        in_specs=[pl.BlockSpec((1, 1, META_BLK), lambda ph, r: (r, 0, 0))],
        out_specs=[pl.BlockSpec((1, 1, META_BLK), lambda ph, r: (ph * r, 0, 0)),
                   pl.BlockSpec((3, LANES), lambda ph, r: (0, 0))],
        out_shape=[jax.ShapeDtypeStruct((nblk, 1, META_BLK), jnp.int32),
                   jax.ShapeDtypeStruct((3, LANES), jnp.int32)],
        scratch_shapes=[pltpu.VMEM((N_EXPERTS, LANES), jnp.float32),
                        pltpu.VMEM((N_EXPERTS, LANES), jnp.float32)],
        compiler_params=_cparams(("arbitrary", "arbitrary")),
        name="meta",
    )(eid3)
    return dest.reshape(na), tiles.reshape(3 * LANES)


def _dispatch_kernel(dest_ref, seg_ref, u_ref, xs_ref, zero_ref, sem, zsem, *, n_tok, tile_rows):
    i = pl.program_id(0)
    slab = xs_ref.shape[1]
    tm = u_ref.shape[0] // slab
    base = i * tm

    @pl.when(i == 0)
    def _():
        zero_ref[...] = jnp.zeros_like(zero_ref)

        def fills(e, act):
            lo = seg_ref[LANES + e]
            pad = seg_ref[2 * LANES + e] - lo
            pos = lo
            for bit in reversed(range(tile_rows.bit_length() - 1)):
                rows = 1 << bit
                take = (pad >> bit) & 1

                @pl.when(take == 1)
                def _(pos=pos, rows=rows):
                    act(pltpu.make_async_copy(zero_ref.at[pl.ds(0, rows)], xs_ref.at[pl.ds(pos, rows)], zsem))

                pos = pos + take * rows

        def start_fills(e, c):
            fills(e, lambda cp: cp.start())
            return c

        def wait_fills(e, c):
            fills(e, lambda cp: cp.wait())
            return c

        run = zero_ref.shape[0]
        used = seg_ref[2 * LANES + N_EXPERTS - 1]
        n_runs = (xs_ref.shape[0] - used) // run

        def tail_fill(j):
            return pltpu.make_async_copy(zero_ref, xs_ref.at[pl.ds(used + j * run, run)], zsem)

        def start_tail(j, c):
            tail_fill(j).start()
            return c

        def wait_tail(j, c):
            tail_fill(j).wait()
            return c

        lax.fori_loop(0, N_EXPERTS, start_fills, 0)
        lax.fori_loop(0, n_runs, start_tail, 0)
        lax.fori_loop(0, N_EXPERTS, wait_fills, 0)
        lax.fori_loop(0, n_runs, wait_tail, 0)

    def row_copy(r, k):
        d = dest_ref[k * n_tok + base + r]
        return pltpu.make_async_copy(u_ref.at[pl.ds(r * slab, slab), :], xs_ref.at[d], sem)

    for r in range(tm):
        for k in range(TOPK_IN_GROUP):
            row_copy(r, k).start(priority=(r * TOPK_IN_GROUP + k) % DMA_PRIORITIES)

    def drain(r, c):
        for k in range(TOPK_IN_GROUP):
            row_copy(r, k).wait()
        return c

    lax.fori_loop(0, tm, drain, 0, unroll=8)


def _dispatch(dest, tiles, u2s, n_tok, n_slots, tile_rows):
    slab = u2s.shape[0] // n_tok
    tm = ROW_TM
    assert tile_rows & (tile_rows - 1) == 0
    return pl.pallas_call(
        functools.partial(_dispatch_kernel, n_tok=n_tok, tile_rows=tile_rows),
        grid_spec=pltpu.PrefetchScalarGridSpec(
            num_scalar_prefetch=2,
            grid=(n_tok // tm,),
            in_specs=[pl.BlockSpec((tm * slab, LANES), lambda i, dest, seg: (i, 0))],
            out_specs=pl.BlockSpec(memory_space=pl.ANY),
            scratch_shapes=[pltpu.VMEM((tile_rows // 2, slab, LANES), jnp.float32),
                            pltpu.SemaphoreType.DMA(()),
                            pltpu.SemaphoreType.DMA(())]),
        out_shape=jax.ShapeDtypeStruct((n_slots, slab, LANES), jnp.float32),
        compiler_params=_cparams(("arbitrary",)),
        name="dispatch",
    )(dest, tiles, u2s)


def _expert_kernel(tile_ref, x_ref, w1_ref, w3_ref, w2_ref, y_ref, wf1, wf3, wf2, w1b, w3b, w2b, wslot_ref, wsem):
    t = pl.program_id(0)
    tm = EXP_TM
    bf16 = jnp.bfloat16
    n_used = tile_ref[LANES - 1]
    e = tile_ref[t]
    changed = (t == 0) | (e != tile_ref[jnp.maximum(t - 1, 0)])
    active = t < n_used

    def weight_copies(ex, ws):
        return (pltpu.make_async_copy(w1_ref.at[ex], wf1.at[ws], wsem.at[ws, 0]),
                pltpu.make_async_copy(w3_ref.at[ex], wf3.at[ws], wsem.at[ws, 1]),
                pltpu.make_async_copy(w2_ref.at[ex], wf2.at[ws], wsem.at[ws, 2]))

    @pl.when(t == 0)
    def _():
        for cp in weight_copies(e, 0):
            cp.start()
        wslot_ref[0] = 0

    @pl.when(active & changed)
    def _():
        ws = wslot_ref[0]
        for cp in weight_copies(e, ws):
            cp.wait()
        nxt = lax.while_loop(lambda t2: (t2 < n_used) & (tile_ref[t2] == e), lambda t2: t2 + 1, t + 1)

        @pl.when(nxt < n_used)
        def _():
            for cp in weight_copies(tile_ref[nxt], 1 - ws):
                cp.start(priority=DMA_PRIORITIES - 1)

        w1b[...] = wf1[ws].astype(bf16)
        w3b[...] = wf3[ws].astype(bf16)
        w2b[...] = wf2[ws].astype(bf16)
        wslot_ref[0] = 1 - ws

    @pl.when(active)
    def _():
        xb = _load_slabs(x_ref, tm).astype(bf16)
        h1 = jnp.dot(xb, w1b[...], preferred_element_type=jnp.float32)
        h3 = jnp.dot(xb, w3b[...], preferred_element_type=jnp.float32)
        hh = (h1 * _sigmoid(h1) * h3).astype(bf16)
        _store_slabs(y_ref, jnp.dot(hh, w2b[...], preferred_element_type=jnp.float32))

    @pl.when(jnp.logical_not(active))
    def _():
        y_ref[...] = jnp.zeros_like(y_ref)


def _experts(tiles, xs2, w1, w3, w2, n_tiles):
    d, f = w1.shape[1], w1.shape[2]
    tm = EXP_TM
    slab = d // LANES
    any_spec = pl.BlockSpec(memory_space=pl.ANY)

    def x_idx(t, tiles):
        return (jnp.maximum(jnp.minimum(t, tiles[LANES - 1] - 1), 0), 0)

    return pl.pallas_call(
        _expert_kernel,
        grid_spec=pltpu.PrefetchScalarGridSpec(
            num_scalar_prefetch=1,
            grid=(n_tiles,),
            in_specs=[pl.BlockSpec((tm * slab, LANES), x_idx), any_spec, any_spec, any_spec],
            out_specs=pl.BlockSpec((tm * slab, LANES), lambda t, tiles: (t, 0)),
            scratch_shapes=[pltpu.VMEM((2, d, f), jnp.float32),
                            pltpu.VMEM((2, d, f), jnp.float32),
                            pltpu.VMEM((2, f, d), jnp.float32),
                            pltpu.VMEM((d, f), jnp.bfloat16),
                            pltpu.VMEM((d, f), jnp.bfloat16),
                            pltpu.VMEM((f, d), jnp.bfloat16),
                            pltpu.SMEM((1,), jnp.int32),
                            pltpu.SemaphoreType.DMA((2, 3))]),
        out_shape=jax.ShapeDtypeStruct((n_tiles * tm * slab, LANES), jnp.float32),
        compiler_params=_cparams(("arbitrary",)),
        name="experts",
    )(tiles, xs2, w1, w3, w2)


def _combine_kernel(dest_ref, ys_ref, x1_ref, rw_ref, gt_ref, g2_ref, b2_ref, o_ref, ybuf0, ybuf1, sem,
                    *, stride, alpha):
    i = pl.program_id(0)
    tm = x1_ref.shape[0]
    slab = ys_ref.shape[1]
    ybufs = (ybuf0, ybuf1)

    def row_copy(slot, buf, k, r):
        return pltpu.make_async_copy(ys_ref.at[slot], ybufs[buf].at[k, pl.ds(r * slab, slab), :], sem.at[buf])

    def gather(step, buf):
        for r in range(tm):
            for k in range(TOPK_IN_GROUP):
                row_copy(dest_ref[k * stride + step * tm + r], buf, k, r).start(
                    priority=(r * TOPK_IN_GROUP + k) % DMA_PRIORITIES)

    def gather_wait(buf):
        def one(r, c):
            for k in range(TOPK_IN_GROUP):
                row_copy(0, buf, k, r).wait()
            return c

        lax.fori_loop(0, tm, one, 0, unroll=8)

    @pl.when(i == 0)
    def _():
        gather(0, 0)

    for buf in range(2):
        @pl.when(i % 2 == buf)
        def _(buf=buf):
            gather_wait(buf)
            gather(i + 1, 1 - buf)
            rw = rw_ref[...]
            f = (rw[:, 0:1] * _load_slabs(ybufs[buf].at[0], tm)
                 + rw[:, 1:2] * _load_slabs(ybufs[buf].at[1], tm))
            y = _layer_norm_rows(alpha * x1_ref[...] + (1.0 + gt_ref[0]) * f)
            o_ref[...] = y * g2_ref[...] + b2_ref[...]

        @pl.when((i == pl.num_programs(0) - 1) & (i % 2 == buf))
        def _(buf=buf):
            gather_wait(1 - buf)


def _combine(dest, ys, x1, rw, gt2, g2, b2, seq, alpha):
    n, d = x1.shape
    tm = ROW_TM
    per_batch = seq // tm
    dest_pad = jnp.concatenate([dest.reshape(TOPK_IN_GROUP, n), jnp.zeros((TOPK_IN_GROUP, tm), jnp.int32)],
                               axis=1).reshape(TOPK_IN_GROUP * (n + tm))
    ybuf = pltpu.VMEM((TOPK_IN_GROUP, tm * (d // LANES), LANES), jnp.float32)
    return pl.pallas_call(
        functools.partial(_combine_kernel, stride=n + tm, alpha=alpha),
        grid_spec=pltpu.PrefetchScalarGridSpec(
            num_scalar_prefetch=1,
            grid=(n // tm,),
            in_specs=[pl.BlockSpec(memory_space=pl.ANY),
                      pl.BlockSpec((tm, d), lambda i, dest: (i, 0)),
                      pl.BlockSpec((tm, LANES), lambda i, dest: (i, 0)),
                      pl.BlockSpec((1, 1, d), lambda i, dest: (i // per_batch, 0, 0)),
                      pl.BlockSpec((1, d), lambda i, dest: (0, 0)),
                      pl.BlockSpec((1, d), lambda i, dest: (0, 0))],
            out_specs=pl.BlockSpec((tm, d), lambda i, dest: (i, 0)),
            scratch_shapes=[ybuf, ybuf, pltpu.SemaphoreType.DMA((2,))]),
        out_shape=jax.ShapeDtypeStruct((n, d), jnp.float32),
        compiler_params=_cparams(("arbitrary",)),
        name="combine",
    )(dest_pad, ys, x1, rw, gt2, g2, b2)


def kernel(x, c, positions, w_cond, b_cond, w_in, b_glu, w_dw, b_dw, g_cn, b_cn, w_conv_out, b_conv_out, w_attn_out, w_mix_out, g_ln1, b_ln1, w_grp, b_grp, w_erouter, b_erouter, w1, w3, w2, g_ln2, b_ln2):
    nb, seq, d = x.shape
    depth = w_cond.shape[0]
    n = nb * seq
    ch = w_dw.shape[2]
    aw = N_HEADS * HEAD_DIM
    in_cols = w_in.shape[2]
    alpha = (2.0 * depth) ** 0.25
    bf16 = jnp.bfloat16
    assert 2 * ch + 3 * aw + 2 * d == in_cols and ch == PROJ_TN and aw == PROJ_TN and d == 2 * PROJ_TN
    assert seq % MOBA_BLOCK == 0 and seq // MOBA_BLOCK <= LANES and N_GROUPS + N_EXPERTS <= LANES
    t_a, t_b, t_q, t_k, t_v, t_gc, t_ga = 0, 1, 2, 3, 4, 5, 7
    heads_per_tile = PROJ_TN // HEAD_DIM
    nblk = seq // MOBA_BLOCK
    n_tiles = (n * TOPK_IN_GROUP) // EXP_TM + N_EXPERTS
    assert (n * TOPK_IN_GROUP) % EXP_TM == 0 and n_tiles < LANES

    cos_t, sin_t = _rope_tables(positions)
    x2 = x.reshape(n, d)
    for l in range(depth):
        mod = _mod(c, w_cond[l], b_cond[l]).reshape(nb, 6, 1, d)
        sh1, sc1, gt1, sh2, sc2, gt2 = (mod[:, s] for s in range(6))

        bias = jnp.concatenate([b_glu[l], jnp.zeros((in_cols - 2 * ch,), jnp.float32)]).reshape(1, in_cols)
        z, km = _in_proj(x2, sc1, sh1, w_in[l], bias, cos_t, sin_t, seq,
                         rope_tiles=(t_q, t_k + 1), sig_from=t_gc)
        km = km.reshape(nb, nblk, aw)
        qx = _gate(z, km, nb, seq, t_q)
        o_attn = _attention(qx, z, nb, seq, t_k * heads_per_tile, t_v * heads_per_tile)
        hc = _conv(z, w_dw[l], b_dw[l], g_cn[l], b_cn[l], nb, seq, t_a, t_b)

        w_r = jnp.concatenate([w_grp[l], w_erouter[l],
                               jnp.zeros((d, LANES - N_GROUPS - N_EXPERTS), jnp.float32)], axis=1)
        b_r = jnp.concatenate([b_grp[l], b_erouter[l],
                               jnp.zeros((LANES - N_GROUPS - N_EXPERTS,), jnp.float32)]).reshape(1, LANES)
        x1, u2, ri, rw = _mix(hc, o_attn, z, x2, w_conv_out[l], b_conv_out[l].reshape(1, d),
                              w_attn_out[l], w_mix_out[l], gt1,
                              g_ln1[l].reshape(1, d), b_ln1[l].reshape(1, d), sc2, sh2, w_r, b_r,
                              seq, t_gc, t_ga, alpha)

        eid = ri[:, :TOPK_IN_GROUP].T.reshape(TOPK_IN_GROUP * n)
        dest, tiles = _meta(eid, EXP_TM)
        f_in = w1.shape[-1]
        slab = d // LANES
        xs = _dispatch(dest, tiles, u2, n, n_tiles * EXP_TM, EXP_TM)
        ys = _experts(tiles, xs.reshape(n_tiles * EXP_TM * slab, LANES), w1[l].reshape(N_EXPERTS, d, f_in),
                      w3[l].reshape(N_EXPERTS, d, f_in), w2[l].reshape(N_EXPERTS, f_in, d), n_tiles)
        x2 = _combine(dest, ys.reshape(n_tiles * EXP_TM, slab, LANES), x1, rw, gt2, g_ln2[l].reshape(1, d), b_ln2[l].reshape(1, d), seq, alpha)
    return x2.reshape(nb, seq, d)
```

```python
import functools

import jax
import jax.numpy as jnp
from jax import lax
from jax.experimental import pallas as pl
from jax.experimental.pallas import tpu as pltpu

CONV_WIDTH = 31
N_HEADS = 8
HEAD_DIM = 128
ROPE_DIM = HEAD_DIM // 4
ROPE_THETA = 500000.0
MOBA_BLOCK = 256
MOBA_TOPK = 3
N_GROUPS = 4
EXPERTS_PER_GROUP = 8
N_EXPERTS = N_GROUPS * EXPERTS_PER_GROUP
TOPK_IN_GROUP = 2
LN_EPS = 1e-5
NEG_INF = -1e30

LANES = 128
SUBLANES = 8
DMA_PRIORITIES = 2
VMEM_LIMIT = 56 * 1024 * 1024
ATTN_VMEM_LIMIT = 60 * 1024 * 1024

PROJ_TM = 1024
PROJ_TN = 1024
PROJ_CHUNK = 256
CONV_TT = 256
CONV_HALO = 32
ATTN_GROUP = 4
ATTN_HEADS = 4
ATTN_STRIP = 128
MIX_TM = 256
EXP_TM = 256
ROW_TM = 512
META_BLK = 1024


def _cparams(sem, vmem=VMEM_LIMIT, flags=None):
    return pltpu.CompilerParams(dimension_semantics=sem, vmem_limit_bytes=vmem, flags=flags)


def _layer_norm_rows(x):
    mu = jnp.mean(x, axis=-1, keepdims=True)
    xc = x - mu
    var = jnp.mean(xc * xc, axis=-1, keepdims=True)
    return xc * lax.rsqrt(var + LN_EPS)


def _sigmoid(x):
    return 1.0 / (1.0 + jnp.exp(-x))


def _store_slabs(ref, val):
    m, w = val.shape
    c = w // LANES
    for k in range(c):
        ref[pl.ds(k, m, stride=c), :] = val[:, k * LANES:(k + 1) * LANES]


def _load_slabs(ref, m):
    c = ref.shape[0] // m
    return jnp.concatenate([ref[pl.ds(k, m, stride=c), :] for k in range(c)], axis=1)


def _dot_nt(a, b):
    return lax.dot_general(a, b, (((1,), (1,)), ((), ())), preferred_element_type=jnp.float32)


def _mod_kernel(cb_ref, w_ref, b_ref, o_ref, s_ref):
    @pl.when(pl.program_id(0) == 0)
    def _():
        cb = cb_ref[...]
        s_ref[...] = cb * _sigmoid(cb)

    nb, d, _ = cb_ref.shape
    tn = w_ref.shape[1]
    for cc in range(tn // LANES):
        sl = slice(cc * LANES, (cc + 1) * LANES)
        wc = w_ref[:, sl]
        for b in range(nb):
            p = (wc * s_ref[b]).reshape(d // SUBLANES, SUBLANES, LANES)
            r = jnp.sum(jnp.sum(p, axis=0), axis=0, keepdims=True)
            o_ref[b:b + 1, sl] = r + b_ref[:, sl]


def _mod(c, w_cond, b_cond, tn=1024):
    nb, d = c.shape
    n_out = w_cond.shape[1]
    cb = jnp.broadcast_to(c[:, :, None], (nb, d, LANES))
    return pl.pallas_call(
        _mod_kernel,
        grid=(n_out // tn,),
        in_specs=[pl.BlockSpec((nb, d, LANES), lambda j: (0, 0, 0)),
                  pl.BlockSpec((d, tn), lambda j: (0, j)),
                  pl.BlockSpec((1, tn), lambda j: (0, j))],
        out_specs=pl.BlockSpec((nb, tn), lambda j: (0, j)),
        out_shape=jax.ShapeDtypeStruct((nb, n_out), jnp.float32),
        scratch_shapes=[pltpu.VMEM((nb, d, LANES), jnp.float32)],
        compiler_params=_cparams(("arbitrary",)),
        name="mod",
    )(cb, w_cond, b_cond.reshape(1, n_out))


def _rope_tab_kernel(pos_ref, invf_ref, sign_ref, cos_ref, sin_ref):
    pack = LANES // ROPE_DIM
    ang = pos_ref[...] * invf_ref[...]
    cos_p = jnp.cos(ang)
    sin_p = jnp.sin(ang) * sign_ref[...]
    rows = ang.shape[0]
    rotated = lax.broadcasted_iota(jnp.int32, ang.shape, 1) < ROPE_DIM
    for t in range(pack):
        shift = (LANES - t * ROPE_DIM) % LANES
        c = pltpu.roll(cos_p, shift, 1) if shift else cos_p
        s = pltpu.roll(sin_p, shift, 1) if shift else sin_p
        cos_ref[pl.ds(t, rows, stride=pack), :] = jnp.where(rotated, c, 1.0)
        sin_ref[pl.ds(t, rows, stride=pack), :] = jnp.where(rotated, s, 0.0)


def _rope_tables(positions, tm=1024):
    n = positions.size
    half = ROPE_DIM // 2
    pack = LANES // ROPE_DIM
    inv_freq = jnp.power(ROPE_THETA, -jnp.arange(half, dtype=jnp.float32) / half)
    invf = jnp.tile(jnp.concatenate([inv_freq, inv_freq]), pack).reshape(1, LANES)
    sign = jnp.tile(jnp.concatenate([-jnp.ones((half,), jnp.float32), jnp.ones((half,), jnp.float32)]),
                    pack).reshape(1, LANES)
    posb = jnp.repeat(positions.astype(jnp.float32).reshape(n // pack, pack), ROPE_DIM, axis=1)
    vec = pl.BlockSpec((1, LANES), lambda i: (0, 0))
    tab = pl.BlockSpec((tm, LANES), lambda i: (i, 0))
    return pl.pallas_call(
        _rope_tab_kernel,
        grid=(n // tm,),
        in_specs=[pl.BlockSpec((tm // pack, LANES), lambda i: (i, 0)), vec, vec],
        out_specs=[tab, tab],
        out_shape=[jax.ShapeDtypeStruct((n, LANES), jnp.float32)] * 2,
        compiler_params=_cparams(("parallel",)),
        name="rope_tab",
    )(posb, invf, sign)


def _in_proj_kernel(x_ref, sc_ref, sh_ref, w_ref, bias_ref, cos_ref, sin_ref, o_ref, km_ref, u_ref,
                    *, rope_tiles, sig_from):
    j = pl.program_id(1)

    @pl.when(j == 0)
    def _():
        xn = _layer_norm_rows(x_ref[...])
        u_ref[...] = (xn * (1.0 + sc_ref[0]) + sh_ref[0]).astype(u_ref.dtype)

    is_rope = (j >= rope_tiles[0]) & (j < rope_tiles[1])

    def project(epilogue, block_means=False):
        for n in range(w_ref.shape[1] // PROJ_CHUNK):
            cs = slice(n * PROJ_CHUNK, (n + 1) * PROJ_CHUNK)
            acc = jnp.dot(u_ref[...], w_ref[:, cs].astype(u_ref.dtype),
                          preferred_element_type=jnp.float32) + bias_ref[:, cs]
            out = epilogue(acc).astype(o_ref.dtype)
            o_ref[:, cs] = out
            if block_means:
                stored = out.astype(jnp.float32)
                for blk in range(km_ref.shape[0]):
                    rows = stored[blk * MOBA_BLOCK:(blk + 1) * MOBA_BLOCK]
                    km_ref[blk, :, cs] = (jnp.sum(rows, axis=0, keepdims=True)
                                          * (1.0 / MOBA_BLOCK)).astype(km_ref.dtype)

    @pl.when(is_rope)
    def _():
        cosf = cos_ref[...]
        sinf = sin_ref[...]
        lane = lax.broadcasted_iota(jnp.int32, cosf.shape, 1)
        first = lane < (ROPE_DIM // 2)

        def rope(acc):
            heads = []
            for h in range(acc.shape[1] // HEAD_DIM):
                xh = acc[:, h * HEAD_DIM:(h + 1) * HEAD_DIM]
                partner = jnp.where(first, pltpu.roll(xh, HEAD_DIM - ROPE_DIM // 2, 1),
                                    pltpu.roll(xh, ROPE_DIM // 2, 1))
                heads.append(xh * cosf + partner * sinf)
            return jnp.concatenate(heads, axis=1)

        project(rope, block_means=True)

    @pl.when(j >= sig_from)
    def _():
        project(_sigmoid)

    @pl.when(jnp.logical_not(is_rope) & (j < sig_from))
    def _():
        project(lambda acc: acc)


def _in_proj(x2, sc1, sh1, w_in, bias, cos_t, sin_t, seq, rope_tiles, sig_from):
    n, d = x2.shape
    cols = w_in.shape[1]
    tm, tn = PROJ_TM, PROJ_TN
    per_batch = seq // tm
    kern = functools.partial(_in_proj_kernel, rope_tiles=rope_tiles, sig_from=sig_from)
    return pl.pallas_call(
        kern,
        grid=(n // tm, cols // tn),
        in_specs=[pl.BlockSpec((tm, d), lambda i, j: (i, 0)),
                  pl.BlockSpec((1, 1, d), lambda i, j: (i // per_batch, 0, 0)),
                  pl.BlockSpec((1, 1, d), lambda i, j: (i // per_batch, 0, 0)),
                  pl.BlockSpec((d, tn), lambda i, j: (0, j)),
                  pl.BlockSpec((1, tn), lambda i, j: (0, j)),
                  pl.BlockSpec((tm, LANES), lambda i, j: (i, 0)),
                  pl.BlockSpec((tm, LANES), lambda i, j: (i, 0))],
        out_specs=[pl.BlockSpec((tm, tn), lambda i, j: (i, j)),
                   pl.BlockSpec((tm // MOBA_BLOCK, 1, tn), lambda i, j: (i, 0, 0))],
        out_shape=[jax.ShapeDtypeStruct((n, cols), jnp.bfloat16),
                   jax.ShapeDtypeStruct((n // MOBA_BLOCK, 1, tn), jnp.bfloat16)],
        scratch_shapes=[pltpu.VMEM((tm, d), jnp.bfloat16)],
        compiler_params=_cparams(("parallel", "arbitrary")),
        name="in_proj",
    )(x2, sc1, sh1, w_in, bias, cos_t, sin_t)


def _gate_kernel(q_ref, km_ref, qx_ref):
    i = pl.program_id(1)
    bs = q_ref.shape[0]
    nblk = km_ref.shape[1]
    sub = lax.broadcasted_iota(jnp.int32, (nblk, bs), 0)
    past = sub < i
    for h in range(N_HEADS):
        hs = slice(h * HEAD_DIM, (h + 1) * HEAD_DIM)
        q = q_ref[:, hs]
        g = jnp.where(past, _dot_nt(km_ref[0, :, hs], q), -jnp.inf)
        attend = sub == i
        for _ in range(MOBA_TOPK):
            m = jnp.max(g, axis=0, keepdims=True)
            idx = jnp.min(jnp.where(g == m, sub, nblk), axis=0, keepdims=True)
            hit = sub == idx
            attend = attend | (hit & past)
            g = jnp.where(hit, -jnp.inf, g)
        att = jnp.concatenate([jnp.where(attend, 1.0, 0.0), jnp.ones((LANES - nblk, bs), jnp.float32)], axis=0)
        bias = jnp.where(att.T > 0.5, 0.0, NEG_INF)
        qx_ref[:, 2 * h * HEAD_DIM:(2 * h + 1) * HEAD_DIM] = q
        qx_ref[:, (2 * h + 1) * HEAD_DIM:(2 * h + 2) * HEAD_DIM] = bias.astype(qx_ref.dtype)


def _gate(z, kmean, nb, seq, q_tile):
    n = z.shape[0]
    nq = seq // MOBA_BLOCK
    aw = N_HEADS * HEAD_DIM
    return pl.pallas_call(
        _gate_kernel,
        grid=(nb, nq),
        in_specs=[pl.BlockSpec((MOBA_BLOCK, aw), lambda b, i: (b * nq + i, q_tile)),
                  pl.BlockSpec((1, nq, aw), lambda b, i: (b, 0, 0))],
        out_specs=pl.BlockSpec((MOBA_BLOCK, 2 * aw), lambda b, i: (b * nq + i, 0)),
        out_shape=jax.ShapeDtypeStruct((n, 2 * aw), jnp.bfloat16),
        compiler_params=_cparams(("parallel", "arbitrary")),
        name="gate",
    )(z, kmean)


def _attn_kernel(qx_ref, k_ref, v_ref, o_ref, kx_ref, vx_ref, s_ref, p_ref):
    i = pl.program_id(2)
    bs = MOBA_BLOCK
    nblk = k_ref.shape[0] // bs
    wide = HEAD_DIM + LANES
    bf16 = jnp.bfloat16
    f32 = jnp.float32
    exp2_scale = (HEAD_DIM ** -0.5) * 1.4426950408889634
    grp = ATTN_GROUP * bs

    @pl.when(i == 0)
    def _():
        sub_k = lax.broadcasted_iota(jnp.int32, (LANES, bs), 0)
        for hh in range(ATTN_HEADS):
            hs = slice(hh * HEAD_DIM, (hh + 1) * HEAD_DIM)
            vx_ref[hh, :, :HEAD_DIM] = v_ref[:, hs]
            vx_ref[hh, :, HEAD_DIM:] = jnp.ones((nblk * bs, LANES), bf16)
            for jb in range(nblk):
                cs = slice(jb * bs, (jb + 1) * bs)
                kx_ref[hh, :HEAD_DIM, cs] = k_ref[cs, hs].astype(f32).T.astype(bf16)
                kx_ref[hh, HEAD_DIM:, cs] = jnp.where(sub_k == jb, 1.0, 0.0).astype(bf16)

    for c in range(nblk // ATTN_GROUP):
        @pl.when(i // ATTN_GROUP == c)
        def _(c=c):
            nk = (c + 1) * grp
            row = lax.broadcasted_iota(jnp.int32, (ATTN_STRIP, LANES), 0) + i * bs
            col = lax.broadcasted_iota(jnp.int32, (ATTN_STRIP, LANES), 1)
            for hh in range(ATTN_HEADS):
                s_ref[hh, :, :nk] = jnp.dot(qx_ref[:, hh * wide:(hh + 1) * wide], kx_ref[hh, :, 0:nk],
                                            preferred_element_type=f32)
            for hh in range(ATTN_HEADS):
                for r0 in range(0, bs, ATTN_STRIP):
                    rs = slice(r0, r0 + ATTN_STRIP)
                    mx = None
                    for t in range(nk // LANES):
                        ts = slice(t * LANES, (t + 1) * LANES)
                        st = s_ref[hh, rs, ts]
                        if t * LANES >= c * grp:
                            st = jnp.where(col + t * LANES <= row + r0, st, NEG_INF)
                            s_ref[hh, rs, ts] = st
                        mx = st if mx is None else jnp.maximum(mx, st)
                    mb = jnp.broadcast_to(jnp.max(mx, axis=-1, keepdims=True) * exp2_scale, (ATTN_STRIP, LANES))
                    for t in range(nk // LANES):
                        ts = slice(t * LANES, (t + 1) * LANES)
                        p_ref[hh, rs, ts] = jnp.exp2(s_ref[hh, rs, ts] * exp2_scale - mb).astype(bf16)
            for hh in range(ATTN_HEADS):
                acc = jnp.dot(p_ref[hh, :, :nk], vx_ref[hh, 0:nk, :], preferred_element_type=f32)
                o_ref[:, hh * HEAD_DIM:(hh + 1) * HEAD_DIM] = (
                    acc[:, :HEAD_DIM] / acc[:, HEAD_DIM:]).astype(o_ref.dtype)


def _attention(qx, z, nb, seq, k_col, v_col):
    n = z.shape[0]
    nq = seq // MOBA_BLOCK
    bs = MOBA_BLOCK
    wide = HEAD_DIM + LANES
    hps = ATTN_HEADS
    assert N_HEADS % hps == 0 and k_col % hps == 0 and v_col % hps == 0
    return pl.pallas_call(
        _attn_kernel,
        grid=(nb, N_HEADS // hps, nq),
        in_specs=[pl.BlockSpec((bs, hps * wide), lambda b, h, i: (b * nq + i, h)),
                  pl.BlockSpec((seq, hps * HEAD_DIM), lambda b, h, i: (b, k_col // hps + h),
                               pipeline_mode=pl.Buffered(1)),
                  pl.BlockSpec((seq, hps * HEAD_DIM), lambda b, h, i: (b, v_col // hps + h),
                               pipeline_mode=pl.Buffered(1))],
        out_specs=pl.BlockSpec((bs, hps * HEAD_DIM), lambda b, h, i: (b * nq + i, h)),
        out_shape=jax.ShapeDtypeStruct((n, N_HEADS * HEAD_DIM), jnp.bfloat16),
        scratch_shapes=[pltpu.VMEM((hps, wide, seq), jnp.bfloat16),
                        pltpu.VMEM((hps, seq, wide), jnp.bfloat16),
                        pltpu.VMEM((hps, bs, seq), jnp.float32),
                        pltpu.VMEM((hps, bs, seq), jnp.bfloat16)],
        compiler_params=_cparams(("parallel", "parallel", "arbitrary"), vmem=ATTN_VMEM_LIMIT),
        name="attn",
    )(qx, z, z)


def _conv_kernel(a_ref, b_ref, ha_ref, hb_ref, w_ref, bdw_ref, g_ref, bcn_ref, o_ref, hext_ref, grp_ref):
    i = pl.program_id(1)
    tt = a_ref.shape[0]
    halo = ha_ref.shape[0]
    ch = a_ref.shape[1]
    glu_h = ha_ref[...].astype(jnp.float32) * _sigmoid(hb_ref[...].astype(jnp.float32))
    hext_ref[0:halo, :] = jnp.where(i > 0, glu_h, 0.0)
    hext_ref[halo:halo + tt, :] = a_ref[...].astype(jnp.float32) * _sigmoid(b_ref[...].astype(jnp.float32))
    hext_ref[halo + tt:, :] = jnp.zeros((SUBLANES, ch), jnp.float32)

    first = halo - (CONV_WIDTH - 1)
    acc = jnp.zeros(o_ref.shape, jnp.float32) + bdw_ref[...]
    for s in range(SUBLANES):
        g = None
        for j in range(CONV_WIDTH):
            off = first + j
            if off % SUBLANES != s:
                continue
            term = w_ref[j:j + 1, :] * hext_ref[pl.ds(off - s, tt + SUBLANES), :]
            g = term if g is None else g + term
        if g is None:
            continue
        if s == 0:
            acc = acc + g[:tt]
        else:
            grp_ref[s] = g
            acc = acc + grp_ref[s, pl.ds(s, tt), :]
    y = _layer_norm_rows(acc) * g_ref[...] + bcn_ref[...]
    o_ref[...] = (y * _sigmoid(y)).astype(o_ref.dtype)


def _conv(z, w_dw, b_dw, g_cn, b_cn, nb, seq, a_tile, b_tile):
    n = z.shape[0]
    ch = w_dw.shape[1]
    tt, halo = CONV_TT, CONV_HALO
    nt = seq // tt
    r = tt // halo

    def cur(tile):
        return pl.BlockSpec((tt, ch), lambda b, i: (b * nt + i, tile))

    def prev(tile):
        return pl.BlockSpec((halo, ch), lambda b, i: (jnp.maximum((b * nt + i) * r - 1, 0), tile))

    vec = pl.BlockSpec((1, ch), lambda b, i: (0, 0))
    return pl.pallas_call(
        _conv_kernel,
        grid=(nb, nt),
        in_specs=[cur(a_tile), cur(b_tile), prev(a_tile), prev(b_tile),
                  pl.BlockSpec((CONV_WIDTH, ch), lambda b, i: (0, 0)), vec, vec, vec],
        out_specs=pl.BlockSpec((tt, ch), lambda b, i: (b * nt + i, 0)),
        out_shape=jax.ShapeDtypeStruct((n, ch), jnp.bfloat16),
        scratch_shapes=[pltpu.VMEM((halo + tt + SUBLANES, ch), jnp.float32),
                        pltpu.VMEM((SUBLANES, tt + SUBLANES, ch), jnp.float32)],
        compiler_params=_cparams(("parallel", "parallel")),
        name="conv",
    )(z, z, z, z, w_dw, b_dw.reshape(1, ch), g_cn.reshape(1, ch), b_cn.reshape(1, ch))


def _mix_kernel(hc_ref, o_ref, sgc0_ref, sgc1_ref, sga0_ref, sga1_ref, x_ref, wco_ref, bco_ref, wao_ref, wmx_ref,
                gt_ref, g1_ref, b1_ref, sc_ref, sh_ref, wr_ref, br_ref,
                x1_ref, u2_ref, ri_ref, rw_ref, *, alpha):
    f32 = jnp.float32
    def cols(a, w_ref):
        return jnp.concatenate(
            [jnp.dot(a, w_ref[:, n * PROJ_CHUNK:(n + 1) * PROJ_CHUNK].astype(jnp.bfloat16),
                     preferred_element_type=f32) for n in range(w_ref.shape[1] // PROJ_CHUNK)], axis=1)

    y_conv = cols(hc_ref[...], wco_ref) + bco_ref[...]
    y_attn = cols(o_ref[...], wao_ref)
    sgc = jnp.concatenate([sgc0_ref[...], sgc1_ref[...]], axis=1).astype(f32)
    sga = jnp.concatenate([sga0_ref[...], sga1_ref[...]], axis=1).astype(f32)
    merged = sgc * y_conv + sga * y_attn
    t_out = cols(merged.astype(jnp.bfloat16), wmx_ref)
    x1 = _layer_norm_rows(alpha * x_ref[...] + (1.0 + gt_ref[0]) * t_out) * g1_ref[...] + b1_ref[...]
    x1_ref[...] = x1
    u2 = _layer_norm_rows(x1) * (1.0 + sc_ref[0]) + sh_ref[0]
    _store_slabs(u2_ref, u2)

    bf16 = jnp.bfloat16
    u_hi = u2.astype(bf16)
    u_lo = (u2 - u_hi.astype(f32)).astype(bf16)
    w_hi = wr_ref[...].astype(bf16)
    w_lo = (wr_ref[...] - w_hi.astype(f32)).astype(bf16)
    hi_terms = jnp.dot(u_hi, jnp.concatenate([w_hi, w_lo], axis=1), preferred_element_type=f32)
    logits = (hi_terms[:, :LANES] + hi_terms[:, LANES:]
              + jnp.dot(u_lo, w_hi, preferred_element_type=f32) + br_ref[...])
    lane = lax.broadcasted_iota(jnp.int32, logits.shape, 1)
    is_grp = lane < N_GROUPS
    gl = jnp.where(is_grp, logits, -jnp.inf)
    gmax = jnp.max(gl, axis=-1, keepdims=True)
    gidx = jnp.min(jnp.where(gl == gmax, lane, LANES), axis=-1, keepdims=True)
    p_top = 1.0 / jnp.sum(jnp.where(is_grp, jnp.exp(logits - gmax), 0.0), axis=-1, keepdims=True)
    lo = N_GROUPS + gidx * EXPERTS_PER_GROUP
    el = jnp.where((lane >= lo) & (lane < lo + EXPERTS_PER_GROUP), logits, -jnp.inf)
    v1 = jnp.max(el, axis=-1, keepdims=True)
    i1 = jnp.min(jnp.where(el == v1, lane, LANES), axis=-1, keepdims=True)
    el = jnp.where(lane == i1, -jnp.inf, el)
    v2 = jnp.max(el, axis=-1, keepdims=True)
    i2 = jnp.min(jnp.where(el == v2, lane, LANES), axis=-1, keepdims=True)
    e2 = jnp.exp(v2 - v1)
    w1 = p_top / (1.0 + e2)
    w2 = p_top * e2 / (1.0 + e2)
    ri_ref[...] = jnp.where(lane == 0, i1 - N_GROUPS, jnp.where(lane == 1, i2 - N_GROUPS, 0))
    rw_ref[...] = jnp.where(lane == 0, w1, jnp.where(lane == 1, w2, 0.0))


def _mix(hc, o_attn, z, x2, wco, bco, wao, wmx, gt1, g1, b1, sc2, sh2, w_r, b_r, seq, gc_tile, ga_tile, alpha):
    n, d = x2.shape
    ch = hc.shape[1]
    tm = MIX_TM
    per_batch = seq // tm

    def rows(width):
        return pl.BlockSpec((tm, width), lambda i: (i, 0))

    def const(shape):
        return pl.BlockSpec(shape, lambda i: (0,) * len(shape), pipeline_mode=pl.Buffered(1))

    def per_b():
        return pl.BlockSpec((1, 1, d), lambda i: (i // per_batch, 0, 0))

    return pl.pallas_call(
        functools.partial(_mix_kernel, alpha=alpha),
        grid=(n // tm,),
        in_specs=[rows(ch), rows(ch),
                  pl.BlockSpec((tm, ch), lambda i: (i, gc_tile)),
                  pl.BlockSpec((tm, ch), lambda i: (i, gc_tile + 1)),
                  pl.BlockSpec((tm, ch), lambda i: (i, ga_tile)),
                  pl.BlockSpec((tm, ch), lambda i: (i, ga_tile + 1)),
                  rows(d),
                  const((ch, d)), const((1, d)), const((ch, d)), const((d, d)),
                  per_b(), const((1, d)), const((1, d)), per_b(), per_b(),
                  const((d, LANES)), const((1, LANES))],
        out_specs=[rows(d), pl.BlockSpec((tm * (d // LANES), LANES), lambda i: (i, 0)), rows(LANES), rows(LANES)],
        out_shape=[jax.ShapeDtypeStruct((n, d), jnp.float32),
                   jax.ShapeDtypeStruct((n * (d // LANES), LANES), jnp.float32),
                   jax.ShapeDtypeStruct((n, LANES), jnp.int32),
                   jax.ShapeDtypeStruct((n, LANES), jnp.float32)],
        compiler_params=_cparams(("parallel",)),
        name="mix",
    )(hc, o_attn, z, z, z, z, x2, wco, bco, wao, wmx, gt1, g1, b1, sc2, sh2, w_r, b_r)


def _meta_kernel(eid_ref, dest_ref, tile_ref, cnt_ref, base_ref, *, tile_rows):
    ph = pl.program_id(0)
    r = pl.program_id(1)
    f32 = jnp.float32
    blk = eid_ref.shape[2]
    e = eid_ref[0]
    sub = lax.broadcasted_iota(jnp.int32, (N_EXPERTS, blk), 0)
    oh = (sub == e).astype(f32)
    blk_cnt = jnp.broadcast_to(jnp.sum(oh, axis=1, keepdims=True), (N_EXPERTS, LANES))

    @pl.when((ph == 0) & (r == 0))
    def _():
        cnt_ref[...] = jnp.zeros_like(cnt_ref)

    @pl.when(ph == 0)
    def _():
        cnt_ref[...] += blk_cnt

    @pl.when((ph == 1) & (r == 0))
    def _():
        padded = jnp.ceil(cnt_ref[...] * (1.0 / tile_rows)) * tile_rows
        er = lax.broadcasted_iota(jnp.int32, (N_EXPERTS, N_EXPERTS), 0)
        ec = lax.broadcasted_iota(jnp.int32, (N_EXPERTS, N_EXPERTS), 1)
        starts = jnp.dot((ec < er).astype(f32), padded, preferred_element_type=f32,
                         precision=lax.Precision.HIGHEST)
        base_ref[...] = starts
        ends = starts + padded
        lane = lax.broadcasted_iota(jnp.int32, (N_EXPERTS, LANES), 1)
        tile_start = (lane * tile_rows).astype(f32)
        owner = jnp.sum((ends <= tile_start).astype(f32), axis=0, keepdims=True)
        total_tiles = jnp.max(ends, axis=0, keepdims=True) * (1.0 / tile_rows)
        lane1 = lax.broadcasted_iota(jnp.int32, (1, LANES), 1)
        tile_ref[0:1, :] = jnp.where(lane1 == LANES - 1, total_tiles, owner).astype(jnp.int32)
        sub_e = lax.broadcasted_iota(jnp.int32, (N_EXPERTS, LANES), 0)
        diag = sub_e == lane
        tile_ref[1:2, :] = jnp.sum(jnp.where(diag, starts + cnt_ref[...], 0.0), axis=0,
                                   keepdims=True).astype(jnp.int32)
        tile_ref[2:3, :] = jnp.sum(jnp.where(diag, ends, 0.0), axis=0, keepdims=True).astype(jnp.int32)

    @pl.when(ph == 1)
    def _():
        ar = lax.broadcasted_iota(jnp.int32, (blk, blk), 0)
        ac = lax.broadcasted_iota(jnp.int32, (blk, blk), 1)
        before = (ar < ac).astype(jnp.bfloat16)
        cum = jnp.dot(oh.astype(jnp.bfloat16), before, preferred_element_type=f32)
        slot = jnp.sum(oh * (base_ref[:, 0:1] + cum), axis=0, keepdims=True)
        dest_ref[0] = slot.astype(jnp.int32)
        base_ref[...] += blk_cnt


def _meta(eid_flat, tile_rows):
    na = eid_flat.shape[0]
    nblk = na // META_BLK
    eid3 = eid_flat.reshape(nblk, 1, META_BLK)
    dest, tiles = pl.pallas_call(
        functools.partial(_meta_kernel, tile_rows=tile_rows),
        grid=(2, nblk),
        in_specs=[pl.BlockSpec((1, 1, META_BLK), lambda ph, r: (r, 0, 0))],
        out_specs=[pl.BlockSpec((1, 1, META_BLK), lambda ph, r: (ph * r, 0, 0)),
                   pl.BlockSpec((3, LANES), lambda ph, r: (0, 0))],
        out_shape=[jax.ShapeDtypeStruct((nblk, 1, META_BLK), jnp.int32),
                   jax.ShapeDtypeStruct((3, LANES), jnp.int32)],
        scratch_shapes=[pltpu.VMEM((N_EXPERTS, LANES), jnp.float32),
                        pltpu.VMEM((N_EXPERTS, LANES), jnp.float32)],
        compiler_params=_cparams(("arbitrary", "arbitrary")),
        name="meta",
    )(eid3)
    return dest.reshape(na), tiles.reshape(3 * LANES)


def _dispatch_kernel(dest_ref, seg_ref, u_ref, xs_ref, zero_ref, sem, zsem, *, n_tok, tile_rows):
    i = pl.program_id(0)
    slab = xs_ref.shape[1]
    tm = u_ref.shape[0] // slab
    base = i * tm

    @pl.when(i == 0)
    def _():
        zero_ref[...] = jnp.zeros_like(zero_ref)

        def fills(e, act):
            lo = seg_ref[LANES + e]
            pad = seg_ref[2 * LANES + e] - lo
            pos = lo
            for bit in reversed(range(tile_rows.bit_length() - 1)):
                rows = 1 << bit
                take = (pad >> bit) & 1

                @pl.when(take == 1)
                def _(pos=pos, rows=rows):
                    act(pltpu.make_async_copy(zero_ref.at[pl.ds(0, rows)], xs_ref.at[pl.ds(pos, rows)], zsem))

                pos = pos + take * rows

        def start_fills(e, c):
            fills(e, lambda cp: cp.start())
            return c

        def wait_fills(e, c):
            fills(e, lambda cp: cp.wait())
            return c

        run = zero_ref.shape[0]
        used = seg_ref[2 * LANES + N_EXPERTS - 1]
        n_runs = (xs_ref.shape[0] - used) // run

        def tail_fill(j):
            return pltpu.make_async_copy(zero_ref, xs_ref.at[pl.ds(used + j * run, run)], zsem)

        def start_tail(j, c):
            tail_fill(j).start()
            return c

        def wait_tail(j, c):
            tail_fill(j).wait()
            return c

        lax.fori_loop(0, N_EXPERTS, start_fills, 0)
        lax.fori_loop(0, n_runs, start_tail, 0)
        lax.fori_loop(0, N_EXPERTS, wait_fills, 0)
        lax.fori_loop(0, n_runs, wait_tail, 0)

    def row_copy(r, k):
        d = dest_ref[k * n_tok + base + r]
        return pltpu.make_async_copy(u_ref.at[pl.ds(r * slab, slab), :], xs_ref.at[d], sem)

    for r in range(tm):
        for k in range(TOPK_IN_GROUP):
            row_copy(r, k).start(priority=(r * TOPK_IN_GROUP + k) % DMA_PRIORITIES)

    def drain(r, c):
        for k in range(TOPK_IN_GROUP):
            row_copy(r, k).wait()
        return c

    lax.fori_loop(0, tm, drain, 0, unroll=8)


def _dispatch(dest, tiles, u2s, n_tok, n_slots, tile_rows):
    slab = u2s.shape[0] // n_tok
    tm = ROW_TM
    assert tile_rows & (tile_rows - 1) == 0
    return pl.pallas_call(
        functools.partial(_dispatch_kernel, n_tok=n_tok, tile_rows=tile_rows),
        grid_spec=pltpu.PrefetchScalarGridSpec(
            num_scalar_prefetch=2,
            grid=(n_tok // tm,),
            in_specs=[pl.BlockSpec((tm * slab, LANES), lambda i, dest, seg: (i, 0))],
            out_specs=pl.BlockSpec(memory_space=pl.ANY),
            scratch_shapes=[pltpu.VMEM((tile_rows // 2, slab, LANES), jnp.float32),
                            pltpu.SemaphoreType.DMA(()),
                            pltpu.SemaphoreType.DMA(())]),
        out_shape=jax.ShapeDtypeStruct((n_slots, slab, LANES), jnp.float32),
        compiler_params=_cparams(("arbitrary",)),
        name="dispatch",
    )(dest, tiles, u2s)


def _expert_kernel(tile_ref, x_ref, w1_ref, w3_ref, w2_ref, y_ref, wf1, wf3, wf2, w1b, w3b, w2b, wslot_ref, wsem):
    t = pl.program_id(0)
    tm = EXP_TM
    bf16 = jnp.bfloat16
    n_used = tile_ref[LANES - 1]
    e = tile_ref[t]
    changed = (t == 0) | (e != tile_ref[jnp.maximum(t - 1, 0)])
    active = t < n_used

    def weight_copies(ex, ws):
        return (pltpu.make_async_copy(w1_ref.at[ex], wf1.at[ws], wsem.at[ws, 0]),
                pltpu.make_async_copy(w3_ref.at[ex], wf3.at[ws], wsem.at[ws, 1]),
                pltpu.make_async_copy(w2_ref.at[ex], wf2.at[ws], wsem.at[ws, 2]))

    @pl.when(t == 0)
    def _():
        for cp in weight_copies(e, 0):
            cp.start()
        wslot_ref[0] = 0

    @pl.when(active & changed)
    def _():
        ws = wslot_ref[0]
        for cp in weight_copies(e, ws):
            cp.wait()
        nxt = lax.while_loop(lambda t2: (t2 < n_used) & (tile_ref[t2] == e), lambda t2: t2 + 1, t + 1)

        @pl.when(nxt < n_used)
        def _():
            for cp in weight_copies(tile_ref[nxt], 1 - ws):
                cp.start(priority=DMA_PRIORITIES - 1)

        w1b[...] = wf1[ws].astype(bf16)
        w3b[...] = wf3[ws].astype(bf16)
        w2b[...] = wf2[ws].astype(bf16)
        wslot_ref[0] = 1 - ws

    @pl.when(active)
    def _():
        xb = _load_slabs(x_ref, tm).astype(bf16)
        h1 = jnp.dot(xb, w1b[...], preferred_element_type=jnp.float32)
        h3 = jnp.dot(xb, w3b[...], preferred_element_type=jnp.float32)
        hh = (h1 * _sigmoid(h1) * h3).astype(bf16)
        _store_slabs(y_ref, jnp.dot(hh, w2b[...], preferred_element_type=jnp.float32))

    @pl.when(jnp.logical_not(active))
    def _():
        y_ref[...] = jnp.zeros_like(y_ref)


def _experts(tiles, xs2, w1, w3, w2, n_tiles):
    d, f = w1.shape[1], w1.shape[2]
    tm = EXP_TM
    slab = d // LANES
    any_spec = pl.BlockSpec(memory_space=pl.ANY)

    def x_idx(t, tiles):
        return (jnp.maximum(jnp.minimum(t, tiles[LANES - 1] - 1), 0), 0)

    return pl.pallas_call(
        _expert_kernel,
        grid_spec=pltpu.PrefetchScalarGridSpec(
            num_scalar_prefetch=1,
            grid=(n_tiles,),
            in_specs=[pl.BlockSpec((tm * slab, LANES), x_idx), any_spec, any_spec, any_spec],
            out_specs=pl.BlockSpec((tm * slab, LANES), lambda t, tiles: (t, 0)),
            scratch_shapes=[pltpu.VMEM((2, d, f), jnp.float32),
                            pltpu.VMEM((2, d, f), jnp.float32),
                            pltpu.VMEM((2, f, d), jnp.float32),
                            pltpu.VMEM((d, f), jnp.bfloat16),
                            pltpu.VMEM((d, f), jnp.bfloat16),
                            pltpu.VMEM((f, d), jnp.bfloat16),
                            pltpu.SMEM((1,), jnp.int32),
                            pltpu.SemaphoreType.DMA((2, 3))]),
        out_shape=jax.ShapeDtypeStruct((n_tiles * tm * slab, LANES), jnp.float32),
        compiler_params=_cparams(("arbitrary",)),
        name="experts",
    )(tiles, xs2, w1, w3, w2)


def _combine_kernel(dest_ref, ys_ref, x1_ref, rw_ref, gt_ref, g2_ref, b2_ref, o_ref, ybuf0, ybuf1, sem,
                    *, stride, alpha):
    i = pl.program_id(0)
    tm = x1_ref.shape[0]
    slab = ys_ref.shape[1]
    ybufs = (ybuf0, ybuf1)

    def row_copy(slot, buf, k, r):
        return pltpu.make_async_copy(ys_ref.at[slot], ybufs[buf].at[k, pl.ds(r * slab, slab), :], sem.at[buf])

    def gather(step, buf):
        for r in range(tm):
            for k in range(TOPK_IN_GROUP):
                row_copy(dest_ref[k * stride + step * tm + r], buf, k, r).start(
                    priority=(r * TOPK_IN_GROUP + k) % DMA_PRIORITIES)

    def gather_wait(buf):
        def one(r, c):
            for k in range(TOPK_IN_GROUP):
                row_copy(0, buf, k, r).wait()
            return c

        lax.fori_loop(0, tm, one, 0, unroll=8)

    @pl.when(i == 0)
    def _():
        gather(0, 0)

    for buf in range(2):
        @pl.when(i % 2 == buf)
        def _(buf=buf):
            gather_wait(buf)
            gather(i + 1, 1 - buf)
            rw = rw_ref[...]
            f = (rw[:, 0:1] * _load_slabs(ybufs[buf].at[0], tm)
                 + rw[:, 1:2] * _load_slabs(ybufs[buf].at[1], tm))
            y = _layer_norm_rows(alpha * x1_ref[...] + (1.0 + gt_ref[0]) * f)
            o_ref[...] = y * g2_ref[...] + b2_ref[...]

        @pl.when((i == pl.num_programs(0) - 1) & (i % 2 == buf))
        def _(buf=buf):
            gather_wait(1 - buf)


def _combine(dest, ys, x1, rw, gt2, g2, b2, seq, alpha):
    n, d = x1.shape
    tm = ROW_TM
    per_batch = seq // tm
    dest_pad = jnp.concatenate([dest.reshape(TOPK_IN_GROUP, n), jnp.zeros((TOPK_IN_GROUP, tm), jnp.int32)],
                               axis=1).reshape(TOPK_IN_GROUP * (n + tm))
    ybuf = pltpu.VMEM((TOPK_IN_GROUP, tm * (d // LANES), LANES), jnp.float32)
    return pl.pallas_call(
        functools.partial(_combine_kernel, stride=n + tm, alpha=alpha),
        grid_spec=pltpu.PrefetchScalarGridSpec(
            num_scalar_prefetch=1,
            grid=(n // tm,),
            in_specs=[pl.BlockSpec(memory_space=pl.ANY),
                      pl.BlockSpec((tm, d), lambda i, dest: (i, 0)),
                      pl.BlockSpec((tm, LANES), lambda i, dest: (i, 0)),
                      pl.BlockSpec((1, 1, d), lambda i, dest: (i // per_batch, 0, 0)),
                      pl.BlockSpec((1, d), lambda i, dest: (0, 0)),
                      pl.BlockSpec((1, d), lambda i, dest: (0, 0))],
            out_specs=pl.BlockSpec((tm, d), lambda i, dest: (i, 0)),
            scratch_shapes=[ybuf, ybuf, pltpu.SemaphoreType.DMA((2,))]),
        out_shape=jax.ShapeDtypeStruct((n, d), jnp.float32),
        compiler_params=_cparams(("arbitrary",)),
        name="combine",
    )(dest_pad, ys, x1, rw, gt2, g2, b2)


def kernel(x, c, positions, w_cond, b_cond, w_in, b_glu, w_dw, b_dw, g_cn, b_cn, w_conv_out, b_conv_out, w_attn_out, w_mix_out, g_ln1, b_ln1, w_grp, b_grp, w_erouter, b_erouter, w1, w3, w2, g_ln2, b_ln2):
    nb, seq, d = x.shape
    depth = w_cond.shape[0]
    n = nb * seq
    ch = w_dw.shape[2]
    aw = N_HEADS * HEAD_DIM
    in_cols = w_in.shape[2]
    alpha = (2.0 * depth) ** 0.25
    bf16 = jnp.bfloat16
    assert 2 * ch + 3 * aw + 2 * d == in_cols and ch == PROJ_TN and aw == PROJ_TN and d == 2 * PROJ_TN
    assert seq % MOBA_BLOCK == 0 and seq // MOBA_BLOCK <= LANES and N_GROUPS + N_EXPERTS <= LANES
    t_a, t_b, t_q, t_k, t_v, t_gc, t_ga = 0, 1, 2, 3, 4, 5, 7
    heads_per_tile = PROJ_TN // HEAD_DIM
    nblk = seq // MOBA_BLOCK
    n_tiles = (n * TOPK_IN_GROUP) // EXP_TM + N_EXPERTS
    assert (n * TOPK_IN_GROUP) % EXP_TM == 0 and n_tiles < LANES

    cos_t, sin_t = _rope_tables(positions)
    x2 = x.reshape(n, d)
    for l in range(depth):
        mod = _mod(c, w_cond[l], b_cond[l]).reshape(nb, 6, 1, d)
        sh1, sc1, gt1, sh2, sc2, gt2 = (mod[:, s] for s in range(6))

        bias = jnp.concatenate([b_glu[l], jnp.zeros((in_cols - 2 * ch,), jnp.float32)]).reshape(1, in_cols)
        z, km = _in_proj(x2, sc1, sh1, w_in[l], bias, cos_t, sin_t, seq,
                         rope_tiles=(t_q, t_k + 1), sig_from=t_gc)
        km = km.reshape(nb, nblk, aw)
        qx = _gate(z, km, nb, seq, t_q)
        o_attn = _attention(qx, z, nb, seq, t_k * heads_per_tile, t_v * heads_per_tile)
        hc = _conv(z, w_dw[l], b_dw[l], g_cn[l], b_cn[l], nb, seq, t_a, t_b)

        w_r = jnp.concatenate([w_grp[l], w_erouter[l],
                               jnp.zeros((d, LANES - N_GROUPS - N_EXPERTS), jnp.float32)], axis=1)
        b_r = jnp.concatenate([b_grp[l], b_erouter[l],
                               jnp.zeros((LANES - N_GROUPS - N_EXPERTS,), jnp.float32)]).reshape(1, LANES)
        x1, u2, ri, rw = _mix(hc, o_attn, z, x2, w_conv_out[l], b_conv_out[l].reshape(1, d),
                              w_attn_out[l], w_mix_out[l], gt1,
                              g_ln1[l].reshape(1, d), b_ln1[l].reshape(1, d), sc2, sh2, w_r, b_r,
                              seq, t_gc, t_ga, alpha)

        eid = ri[:, :TOPK_IN_GROUP].T.reshape(TOPK_IN_GROUP * n)
        dest, tiles = _meta(eid, EXP_TM)
        f_in = w1.shape[-1]
        slab = d // LANES
        xs = _dispatch(dest, tiles, u2, n, n_tiles * EXP_TM, EXP_TM)
        ys = _experts(tiles, xs.reshape(n_tiles * EXP_TM * slab, LANES), w1[l].reshape(N_EXPERTS, d, f_in),
                      w3[l].reshape(N_EXPERTS, d, f_in), w2[l].reshape(N_EXPERTS, f_in, d), n_tiles)
        x2 = _combine(dest, ys.reshape(n_tiles * EXP_TM, slab, LANES), x1, rw, gt2, g_ln2[l].reshape(1, d), b_ln2[l].reshape(1, d), seq, alpha)
    return x2.reshape(nb, seq, d)
```
